```python
import math
import jax
import jax.numpy as jnp
from jax import lax
import numpy as np


D_MODEL = 2048
BATCH = 32
SEQ = 256
DEPTH = 2
DEC_BATCH = 2
DEC_SEQ = 4096
PAST_LEN = 256

GRID_W = 64
QBLOCK = 128
CHUNK = 128
ROPE_THETA = 10000.0
EPS = 1e-6

A_GROUPS = 8
A_GROUP_CH = 128
A_WIDTH = A_GROUPS * A_GROUP_CH
B_HEADS = 8
B_KV_HEADS = 2
B_HEAD_DIM = 128
B_GROUP = B_HEADS // B_KV_HEADS
B_WIDTH = B_HEADS * B_HEAD_DIM
B_KV_WIDTH = B_KV_HEADS * B_HEAD_DIM
IN0_WIDTH = 3 * A_WIDTH + 2 * B_WIDTH + 2 * B_KV_WIDTH
OUT0_WIDTH = A_WIDTH + B_WIDTH
C_HEADS = 16
C_QK_DIM = 64
C_V_DIM = 2 * C_QK_DIM
C_QK_WIDTH = C_HEADS * 2 * C_QK_DIM
C_V_WIDTH = C_HEADS * C_V_DIM
IN1_WIDTH = 2 * C_QK_WIDTH + 2 * C_V_WIDTH
LAMBDA_INIT_1 = 0.8 - 0.6 * math.exp(-0.3 * 1)

kernel_name = "hybrid_gmlp_gqa_diffattn_dit_step"

F32 = jnp.float32


def rms_norm(x, g):
    xf = x.astype(F32)
    xf = xf * lax.rsqrt(jnp.mean(xf * xf, axis=-1, keepdims=True) + EPS)
    return (xf * g.astype(F32)).astype(x.dtype)


def ada_modulate(x, cond, w_ada, b_ada, norm_g):
    mod = jax.nn.silu(cond) @ w_ada + b_ada
    if mod.ndim == 1:
        mod = mod[None]
    shift, scale, gate = jnp.split(mod[:, None, :], 3, axis=-1)
    h = rms_norm(x, norm_g) * (1 + scale) + shift
    return h, gate


def rope_1d(x, pos):
    half = x.shape[-1] // 2
    freqs = ROPE_THETA ** (-jnp.arange(half, dtype=F32) / half)
    ang = pos[:, None] * freqs[None, :]
    shp = (pos.shape[0],) + (1,) * (x.ndim - 3) + (half,)
    cos = jnp.cos(ang).reshape(shp)
    sin = jnp.sin(ang).reshape(shp)
    xf = x.astype(F32)
    x1, x2 = xf[..., :half], xf[..., half:]
    return jnp.concatenate([x1 * cos - x2 * sin, x1 * sin + x2 * cos], axis=-1).astype(x.dtype)


def rope_2d(x, row, col):
    h = x.shape[-1] // 2
    return jnp.concatenate([rope_1d(x[..., :h], row), rope_1d(x[..., h:], col)], axis=-1)


def over_query_blocks(fn, q):
    b_, t = q.shape[0], q.shape[1]
    nb = t // QBLOCK
    qb = jnp.moveaxis(q.reshape((b_, nb, QBLOCK) + q.shape[2:]), 1, 0)
    out = lax.map(fn, qb)
    return jnp.moveaxis(out, 0, 1).reshape((b_, t) + out.shape[3:])


def gqa_attend(q, k, v):
    scale = B_HEAD_DIM ** -0.5

    def block(qb):
        s = jnp.einsum('bqhgd,bshd->bhgqs', qb, k).astype(F32) * scale
        p = jax.nn.softmax(s, axis=-1).astype(v.dtype)
        return jnp.einsum('bhgqs,bshd->bqhgd', p, v)

    return over_query_blocks(block, q)


def diff_attend(q, k, v, lam):
    scale = C_QK_DIM ** -0.5

    def block(qb):
        s = jnp.einsum('bqhmd,bshmd->bhmqs', qb, k).astype(F32) * scale
        p = jax.nn.softmax(s, axis=-1)
        w = (p[:, :, 0] - lam * p[:, :, 1]).astype(v.dtype)
        return jnp.einsum('bhqs,bshe->bqhe', w, v)

    return over_query_blocks(block, q)


def chunk_spatial_gate(u, v, w_s, b_s):
    b_, t = u.shape[0], u.shape[1]
    n = t // CHUNK
    vf = v.astype(F32)
    mu = jnp.mean(vf, axis=-1, keepdims=True)
    var = jnp.mean((vf - mu) ** 2, axis=-1, keepdims=True)
    vn = ((vf - mu) * lax.rsqrt(var + EPS)).astype(v.dtype)
    vc = vn.reshape(b_, n, CHUNK, A_GROUPS, A_GROUP_CH)
    s = jnp.einsum('gpq,bnqgc->bnpgc', w_s, vc) + b_s.T[None, None, :, :, None]
    return u * s.reshape(b_, t, A_WIDTH)


def mixer_ab(h, w_in, w_s, b_s, q_g, k_g, w_out, pos, ctx_k, ctx_v):
    b_, t = h.shape[0], h.shape[1]
    z = h @ w_in
    idx = np.cumsum([A_WIDTH, A_WIDTH, A_WIDTH, B_WIDTH, B_KV_WIDTH, B_KV_WIDTH]).tolist()
    u_a, v_a, g_a, q, k, v, g_b = jnp.split(z, idx, axis=-1)
    y_a = chunk_spatial_gate(u_a, v_a, w_s, b_s) * jax.nn.silu(g_a)
    q = rms_norm(q.reshape(b_, t, B_KV_HEADS, B_GROUP, B_HEAD_DIM), q_g)
    k = rms_norm(k.reshape(b_, t, B_KV_HEADS, B_HEAD_DIM), k_g)
    v = v.reshape(b_, t, B_KV_HEADS, B_HEAD_DIM)
    if pos is None:
        qr, k_all, v_all = q, k, v
    else:
        row, col = pos
        qr = rope_2d(q, row, col)
        k_all = jnp.concatenate([rope_2d(k, row, col), ctx_k], axis=1)
        v_all = jnp.concatenate([v, ctx_v], axis=1)
    y_b = gqa_attend(qr, k_all, v_all).reshape(b_, t, B_WIDTH) * jax.nn.silu(g_b)
    out = jnp.concatenate([y_a, y_b], axis=-1) @ w_out
    return out, k, v


def mixer_diff(h, w_in, lq1, lk1, lq2, lk2, sub_g, w_out, pos, ctx_k, ctx_v):
    b_, t = h.shape[0], h.shape[1]
    z = h @ w_in
    idx = [C_QK_WIDTH, 2 * C_QK_WIDTH, 2 * C_QK_WIDTH + C_V_WIDTH]
    q, k, v, g = jnp.split(z, idx, axis=-1)
    q = q.reshape(b_, t, C_HEADS, 2, C_QK_DIM)
    k = k.reshape(b_, t, C_HEADS, 2, C_QK_DIM)
    v = v.reshape(b_, t, C_HEADS, C_V_DIM)
    lam = (jnp.exp(jnp.sum(lq1.astype(F32) * lk1.astype(F32)))
           - jnp.exp(jnp.sum(lq2.astype(F32) * lk2.astype(F32))) + LAMBDA_INIT_1)
    if pos is None:
        qr, k_all, v_all = q, k, v
    else:
        row, col = pos
        qr = rope_2d(q, row, col)
        k_all = jnp.concatenate([rope_2d(k, row, col), ctx_k], axis=1)
        v_all = jnp.concatenate([v, ctx_v], axis=1)
    o = diff_attend(qr, k_all, v_all, lam)
    o = rms_norm(o, sub_g) * (1 - LAMBDA_INIT_1)
    out = (o.reshape(b_, t, C_V_WIDTH) * jax.nn.silu(g)) @ w_out
    return out, k, v


def setup_inputs(seed: int = 0) -> dict:
    key = jax.random.key(seed)
    ks = jax.random.split(key, 32)
    D = D_MODEL

    def nrm(k, shape, s):
        return jax.random.normal(k, shape, F32) * s

    return {
        "x_prompt": nrm(ks[0], (BATCH, SEQ, D), 1.0),
        "x_sample": nrm(ks[1], (DEC_BATCH, DEC_SEQ, D), 1.0),
        "cache_k0": nrm(ks[2], (DEC_BATCH, PAST_LEN, B_KV_HEADS, B_HEAD_DIM), 1.0),
        "cache_v0": nrm(ks[3], (DEC_BATCH, PAST_LEN, B_KV_HEADS, B_HEAD_DIM), 1.0),
        "cache_k1": nrm(ks[4], (DEC_BATCH, PAST_LEN, C_HEADS, 2, C_QK_DIM), 1.0),
        "cache_v1": nrm(ks[5], (DEC_BATCH, PAST_LEN, C_HEADS, C_V_DIM), 1.0),
        "c": nrm(ks[6], (DEC_BATCH, D), 1.0),
        "c_ctx": nrm(ks[7], (D,), 1.0),
        "w_ada0": nrm(ks[8], (D, 3 * D), D ** -0.5),
        "b_ada0": nrm(ks[9], (3 * D,), 0.02),
        "norm_g0": 1.0 + nrm(ks[10], (D,), 0.02),
        "w_in0": nrm(ks[11], (D, IN0_WIDTH), D ** -0.5),
        "w_s0": nrm(ks[12], (A_GROUPS, CHUNK, CHUNK), CHUNK ** -0.5),
        "b_s0": nrm(ks[13], (A_GROUPS, CHUNK), 0.02),
        "q_norm_g0": 1.0 + nrm(ks[14], (B_HEAD_DIM,), 0.02),
        "k_norm_g0": 1.0 + nrm(ks[15], (B_HEAD_DIM,), 0.02),
        "w_out0": nrm(ks[16], (OUT0_WIDTH, D), OUT0_WIDTH ** -0.5),
        "w_ada1": nrm(ks[17], (D, 3 * D), D ** -0.5),
        "b_ada1": nrm(ks[18], (3 * D,), 0.02),
        "norm_g1": 1.0 + nrm(ks[19], (D,), 0.02),
        "w_in1": nrm(ks[20], (D, IN1_WIDTH), D ** -0.5),
        "lambda_q1": nrm(ks[21], (C_QK_DIM,), 0.1),
        "lambda_k1": nrm(ks[22], (C_QK_DIM,), 0.1),
        "lambda_q2": nrm(ks[23], (C_QK_DIM,), 0.1),
        "lambda_k2": nrm(ks[24], (C_QK_DIM,), 0.1),
        "subln_g1": 1.0 + nrm(ks[25], (C_V_DIM,), 0.02),
        "w_out1": nrm(ks[26], (C_V_WIDTH, D), C_V_WIDTH ** -0.5),
        "final_g": 1.0 + nrm(ks[27], (D,), 0.02),
    }


def reference(x_prompt, x_sample, cache_k0, cache_v0, cache_k1, cache_v1, c, c_ctx,
              w_ada0, b_ada0, norm_g0, w_in0, w_s0, b_s0, q_norm_g0, k_norm_g0, w_out0,
              w_ada1, b_ada1, norm_g1, w_in1, lambda_q1, lambda_k1, lambda_q2, lambda_k2,
              subln_g1, w_out1, final_g):
    ada_params = [(w_ada0, b_ada0, norm_g0), (w_ada1, b_ada1, norm_g1)]
    caches = [(cache_k0, cache_v0), (cache_k1, cache_v1)]

    def run_mixer(layer, h, pos, ctx_k, ctx_v):
        if layer % 2 == 0:
            return mixer_ab(h, w_in0, w_s0, b_s0, q_norm_g0, k_norm_g0, w_out0, pos, ctx_k, ctx_v)
        return mixer_diff(h, w_in1, lambda_q1, lambda_k1, lambda_q2, lambda_k2, subln_g1, w_out1,
                          pos, ctx_k, ctx_v)

    x = x_prompt
    ctx_state = []
    for layer in range(DEPTH):
        w_ada, b_ada, ng = ada_params[layer]
        h, gate = ada_modulate(x, c_ctx, w_ada, b_ada, ng)
        out, k_ctx, v_ctx = run_mixer(layer, h, None, None, None)
        x = x + gate * out
        ctx_state.append((k_ctx, v_ctx))
    y_prompt = rms_norm(x, final_g)

    t = x_sample.shape[1]
    rows = t // GRID_W
    row_pos = jnp.repeat(jnp.arange(rows, dtype=F32), GRID_W)
    col_pos = jnp.tile(jnp.arange(GRID_W, dtype=F32), rows)
    pos = (row_pos, col_pos)
    x = x_sample
    for layer in range(DEPTH):
        w_ada, b_ada, ng = ada_params[layer]
        h, gate = ada_modulate(x, c, w_ada, b_ada, ng)
        ck, cv = caches[layer]
        out, _, _ = run_mixer(layer, h, pos, ck, cv)
        x = x + gate * out
    y_sample = rms_norm(x, final_g)

    (new_k0, new_v0), (new_k1, new_v1) = ctx_state
    return (y_prompt, y_sample, new_k0, new_v0, new_k1, new_v1)
```

```python
import functools
import math

import jax
import jax.numpy as jnp
from jax import lax
from jax.experimental import pallas as pl
from jax.experimental.pallas import tpu as pltpu

F32 = jnp.float32
BF16 = jnp.bfloat16

D_MODEL = 2048
SEQ = 256
DEC_SEQ = 4096
PAST_LEN = 256
GRID_W = 64
CHUNK = 128
ROPE_THETA = 10000.0
EPS = 1e-6

A_GROUPS = 8
A_WIDTH = 1024
B_HEADS = 8
B_KV_HEADS = 2
B_GROUP = B_HEADS // B_KV_HEADS
B_HEAD_DIM = 128
B_WIDTH = 1024
B_KV_WIDTH = 256
IN0_WIDTH = 3 * A_WIDTH + 2 * B_WIDTH + 2 * B_KV_WIDTH
C_HEADS = 16
C_QK_DIM = 64
C_V_DIM = 128
C_WIDTH = 2048
IN1_WIDTH = 4 * C_WIDTH
LAMBDA_INIT_1 = 0.8 - 0.6 * math.exp(-0.3 * 1)

LANES = 128
LOG2E = math.log2(math.e)
VMEM_LIMIT = 56 * 1024 * 1024

Q0_OFF = 3 * A_WIDTH
K0_OFF = Q0_OFF + B_WIDTH
V0_OFF = K0_OFF + B_KV_WIDTH
G0_OFF = V0_OFF + B_KV_WIDTH


def _params(*sem):
    return pltpu.CompilerParams(dimension_semantics=sem, vmem_limit_bytes=VMEM_LIMIT)


def _silu(x):
    return x * jax.nn.sigmoid(x)


def _ada_kernel(c_ref, w_ref, b_ref, o_ref):
    s = _silu(c_ref[...]).astype(BF16)
    o_ref[...] = jnp.dot(s, w_ref[...].astype(BF16), preferred_element_type=F32) + b_ref[...]


def _ada(cond, w_ada, b_ada):
    bn = 512
    n = w_ada.shape[1]
    return pl.pallas_call(
        _ada_kernel,
        grid=(n // bn,),
        in_specs=[
            pl.BlockSpec((8, D_MODEL), lambda j: (0, 0)),
            pl.BlockSpec((D_MODEL, bn), lambda j: (0, j)),
            pl.BlockSpec((1, bn), lambda j: (0, j)),
        ],
        out_specs=pl.BlockSpec((8, bn), lambda j: (0, j)),
        out_shape=jax.ShapeDtypeStruct((8, n), F32),
        compiler_params=_params("parallel"),
        name="ada",
    )(cond, w_ada, b_ada.reshape(1, n))


def _norm_mod_kernel(x_ref, g_ref, shift_ref, scale_ref, h_ref):
    x = x_ref[...]
    ms = jnp.mean(x * x, axis=-1, keepdims=True)
    xn = x * lax.rsqrt(ms + EPS) * g_ref[...]
    h_ref[...] = (xn * (1.0 + scale_ref[...]) + shift_ref[...]).astype(h_ref.dtype)


def _norm_mod(x, norm_g, mod3, row_of_tile, tm):
    t = x.shape[0]
    return pl.pallas_call(
        _norm_mod_kernel,
        grid=(t // tm,),
        in_specs=[
            pl.BlockSpec((tm, D_MODEL), lambda i: (i, 0)),
            pl.BlockSpec((1, D_MODEL), lambda i: (0, 0)),
            pl.BlockSpec((None, 1, D_MODEL), lambda i: (row_of_tile(i), 0, 0)),
            pl.BlockSpec((None, 1, D_MODEL), lambda i: (row_of_tile(i), 0, 1)),
        ],
        out_specs=pl.BlockSpec((tm, D_MODEL), lambda i: (i, 0)),
        out_shape=jax.ShapeDtypeStruct((t, D_MODEL), BF16),
        compiler_params=_params("parallel"),
        name="norm_mod",
    )(x, norm_g.reshape(1, D_MODEL), mod3, mod3)


def _mm_kernel(h_ref, w_ref, o_ref):
    o_ref[...] = jnp.dot(h_ref[...], w_ref[...], preferred_element_type=F32).astype(o_ref.dtype)


def _mm(h, w, out_dtype, tm=1024):
    t, k = h.shape
    n = w.shape[1]
    tn = 1024 if n % 1024 == 0 else 512
    assert t % tm == 0 and n % tn == 0
    return pl.pallas_call(
        _mm_kernel,
        grid=(t // tm, n // tn),
        in_specs=[
            pl.BlockSpec((tm, k), lambda i, j: (i, 0)),
            pl.BlockSpec((k, tn), lambda i, j: (0, j)),
        ],
        out_specs=pl.BlockSpec((tm, tn), lambda i, j: (i, j)),
        out_shape=jax.ShapeDtypeStruct((t, n), out_dtype),
        compiler_params=_params("parallel", "parallel"),
        name="in_proj",
    )(h, w)


def _sgate_kernel(u_ref, v_ref, g_ref, ws_ref, bs_ref, o_ref):
    v = v_ref[...].astype(F32)
    mu = jnp.mean(v, axis=-1, keepdims=True)
    vc = v - mu
    var = jnp.mean(vc * vc, axis=-1, keepdims=True)
    vn = (vc * lax.rsqrt(var + EPS)).astype(BF16)
    tm = v.shape[0]
    for ch in range(tm // CHUNK):
        rows = slice(ch * CHUNK, (ch + 1) * CHUNK)
        for g in range(A_GROUPS):
            cols = slice(g * LANES, (g + 1) * LANES)
            s = jnp.dot(ws_ref[g], vn[rows, cols], preferred_element_type=F32) + bs_ref[g]
            u = u_ref[rows, cols].astype(F32)
            gate = g_ref[rows, cols].astype(F32)
            o_ref[rows, cols] = (u * s * _silu(gate)).astype(o_ref.dtype)


def _sgate(z, ws_b, bs_b, tm=256):
    t = z.shape[0]
    blk = lambda c: pl.BlockSpec((tm, A_WIDTH), lambda i, c=c: (i, c))
    full = pl.BlockSpec((A_GROUPS, CHUNK, CHUNK), lambda i: (0, 0, 0))
    return pl.pallas_call(
        _sgate_kernel,
        grid=(t // tm,),
        in_specs=[blk(0), blk(1), blk(2), full, full],
        out_specs=pl.BlockSpec((tm, A_WIDTH), lambda i: (i, 0)),
        out_shape=jax.ShapeDtypeStruct((t, A_WIDTH), BF16),
        compiler_params=_params("parallel"),
        name="spatial_gate",
    )(z, z, z, ws_b, bs_b)


def _rope_tables(half):
    rows = DEC_SEQ // GRID_W
    row_pos = jnp.repeat(jnp.arange(rows, dtype=F32), GRID_W)
    col_pos = jnp.tile(jnp.arange(GRID_W, dtype=F32), rows)
    freqs = ROPE_THETA ** (-jnp.arange(half, dtype=F32) / half)
    ang_r = row_pos[:, None] * freqs[None, :]
    ang_c = col_pos[:, None] * freqs[None, :]
    cr, sr, cc, sc = jnp.cos(ang_r), jnp.sin(ang_r), jnp.cos(ang_c), jnp.sin(ang_c)
    z = jnp.zeros_like(sr)
    reps = LANES // (4 * half)
    cos_t = jnp.tile(jnp.concatenate([cr, cr, cc, cc], axis=-1), (1, reps))
    sin_a = jnp.tile(jnp.concatenate([z, sr, z, sc], axis=-1), (1, reps))
    sin_b = jnp.tile(jnp.concatenate([-sr, z, -sc, z], axis=-1), (1, reps))
    return cos_t, sin_a, sin_b


def _rope(x, cos_t, sin_a, sin_b, half):
    return x * cos_t + pltpu.roll(x, half, 1) * sin_a + pltpu.roll(x, LANES - half, 1) * sin_b


def _prep0_kernel(*refs, rope, emit_f32):
    q_ref, k_ref, v_ref, qg_ref, kg_ref = refs[:5]
    refs = refs[5:]
    if rope:
        cos_ref, sa_ref, sb_ref = refs[:3]
        refs = refs[3:]
    qo_ref, ko_ref, vo_ref = refs[:3]
    if emit_f32:
        kf_ref, vf_ref = refs[3:5]

    def norm(x, g):
        ms = jnp.mean(x * x, axis=-1, keepdims=True)
        return x * lax.rsqrt(ms + EPS) * g

    def rot(x):
        if not rope:
            return x
        return _rope(x, cos_ref[...], sa_ref[...], sb_ref[...], B_HEAD_DIM // 4)

    qscale = (B_HEAD_DIM ** -0.5) * LOG2E
    for h in range(B_HEADS):
        cols = slice(h * LANES, (h + 1) * LANES)
        qn = norm(q_ref[:, cols].astype(F32), qg_ref[...])
        qo_ref[:, cols] = (rot(qn) * qscale).astype(qo_ref.dtype)
    for h in range(B_KV_HEADS):
        cols = slice(h * LANES, (h + 1) * LANES)
        kn = norm(k_ref[:, cols].astype(F32), kg_ref[...])
        ko_ref[:, cols] = rot(kn).astype(ko_ref.dtype)
        if emit_f32:
            kf_ref[:, cols] = kn
    v = v_ref[...]
    vo_ref[...] = v.astype(vo_ref.dtype)
    if emit_f32:
        vf_ref[...] = v.astype(F32)


def _prep0(z, q_g, k_g, tables, emit_f32, tm=256):
    t = z.shape[0]
    rope = tables is not None
    in_specs = [
        pl.BlockSpec((tm, B_WIDTH), lambda i: (i, Q0_OFF // B_WIDTH)),
        pl.BlockSpec((tm, B_KV_WIDTH), lambda i: (i, K0_OFF // B_KV_WIDTH)),
        pl.BlockSpec((tm, B_KV_WIDTH), lambda i: (i, V0_OFF // B_KV_WIDTH)),
        pl.BlockSpec((1, LANES), lambda i: (0, 0)),
        pl.BlockSpec((1, LANES), lambda i: (0, 0)),
    ]
    args = [z, z, z, q_g.reshape(1, LANES), k_g.reshape(1, LANES)]
    if rope:
        nt = DEC_SEQ // tm
        in_specs += [pl.BlockSpec((tm, LANES), lambda i: (i % nt, 0))] * 3
        args += list(tables)
    out_specs = [
        pl.BlockSpec((tm, B_WIDTH), lambda i: (i, 0)),
        pl.BlockSpec((tm, B_KV_WIDTH), lambda i: (i, 0)),
        pl.BlockSpec((tm, B_KV_WIDTH), lambda i: (i, 0)),
    ]
    out_shape = [
        jax.ShapeDtypeStruct((t, B_WIDTH), BF16),
        jax.ShapeDtypeStruct((t, B_KV_WIDTH), BF16),
        jax.ShapeDtypeStruct((t, B_KV_WIDTH), BF16),
    ]
    if emit_f32:
        out_specs += [pl.BlockSpec((tm, B_KV_WIDTH), lambda i: (i, 0))] * 2
        out_shape += [jax.ShapeDtypeStruct((t, B_KV_WIDTH), F32)] * 2
    return pl.pallas_call(
        functools.partial(_prep0_kernel, rope=rope, emit_f32=emit_f32),
        grid=(t // tm,),
        in_specs=in_specs,
        out_specs=out_specs,
        out_shape=out_shape,
        compiler_params=_params("parallel"),
        name="prep0",
    )(*args)


def _prep1_kernel(q_ref, k_ref, cos_ref, sa_ref, sb_ref, qo_ref, ko_ref):
    cos_t, sa, sb = cos_ref[...], sa_ref[...], sb_ref[...]
    qscale = (C_QK_DIM ** -0.5) * LOG2E
    for h in range(C_HEADS):
        cols = slice(h * LANES, (h + 1) * LANES)
        q = _rope(q_ref[:, cols].astype(F32), cos_t, sa, sb, C_QK_DIM // 4)
        qo_ref[:, cols] = (q * qscale).astype(qo_ref.dtype)
        k = _rope(k_ref[:, cols].astype(F32), cos_t, sa, sb, C_QK_DIM // 4)
        ko_ref[:, cols] = k.astype(ko_ref.dtype)


def _prep1(z, tables, tm=256):
    t = z.shape[0]
    nt = DEC_SEQ // tm
    tab = pl.BlockSpec((tm, LANES), lambda i: (i % nt, 0))
    return pl.pallas_call(
        _prep1_kernel,
        grid=(t // tm,),
        in_specs=[
            pl.BlockSpec((tm, C_WIDTH), lambda i: (i, 0)),
            pl.BlockSpec((tm, C_WIDTH), lambda i: (i, 1)),
            tab, tab, tab,
        ],
        out_specs=[pl.BlockSpec((tm, C_WIDTH), lambda i: (i, 0))] * 2,
        out_shape=[jax.ShapeDtypeStruct((t, C_WIDTH), BF16)] * 2,
        compiler_params=_params("parallel"),
        name="prep1",
    )(z, z, *tables)


def _scores(q, k):
    return lax.dot_general(q, k, (((1,), (1,)), ((), ())), preferred_element_type=F32)


def _softmax_pv(q, kvs):
    ss = [_scores(q, k) for k, _ in kvs]
    m = ss[0].max(axis=-1, keepdims=True)
    for s in ss[1:]:
        m = jnp.maximum(m, s.max(axis=-1, keepdims=True))
    l = None
    acc = None
    for s, (_, v) in zip(ss, kvs):
        p = jnp.exp2(s - m)
        ls = p.sum(axis=-1, keepdims=True)
        a = jnp.dot(p.astype(BF16), v, preferred_element_type=F32)
        l = ls if l is None else l + ls
        acc = a if acc is None else acc + a
    return acc, l


def _gqa_kernel(*refs, has_cache):
    q_ref, k_ref, v_ref = refs[:3]
    refs = refs[3:]
    kvs = [(k_ref[...], v_ref[...])]
    if has_cache:
        kvs.append((refs[0][...].astype(BF16), refs[1][...].astype(BF16)))
        refs = refs[2:]
    g_ref, o_ref = refs
    acc, l = _softmax_pv(q_ref[...], kvs)
    gate = g_ref[...].astype(F32)
    o_ref[...] = (acc / l * _silu(gate)).astype(o_ref.dtype)


def _gqa(qp, kp, vp, cache, z, nb, t, tq):
    nq = t // tq
    in_specs = [
        pl.BlockSpec((tq, LANES), lambda b, h, i: (b * nq + i, h)),
        pl.BlockSpec((t, LANES), lambda b, h, i: (b, h // B_GROUP)),
        pl.BlockSpec((t, LANES), lambda b, h, i: (b, h // B_GROUP)),
    ]
    args = [qp, kp, vp]
    if cache is not None:
        in_specs += [pl.BlockSpec((PAST_LEN, LANES), lambda b, h, i: (b, h // B_GROUP))] * 2
        args += list(cache)
    in_specs.append(pl.BlockSpec((tq, LANES), lambda b, h, i: (b * nq + i, G0_OFF // LANES + h)))
    args.append(z)
    return pl.pallas_call(
        functools.partial(_gqa_kernel, has_cache=cache is not None),
        grid=(nb, B_HEADS, nq),
        in_specs=in_specs,
        out_specs=pl.BlockSpec((tq, LANES), lambda b, h, i: (b * nq + i, h)),
        out_shape=jax.ShapeDtypeStruct((nb * t, B_WIDTH), BF16),
        compiler_params=_params("parallel", "parallel", "parallel"),
        name="gqa_attn",
    )(*args)


def _diff_kernel(*refs, has_cache, q_scale):
    q_ref, k_ref, v_ref = refs[:3]
    refs = refs[3:]
    kvs = [(k_ref[...].astype(BF16), v_ref[...].astype(BF16))]
    if has_cache:
        kvs.append((refs[0][...].astype(BF16), refs[1][...].astype(BF16)))
        refs = refs[2:]
    g_ref, lam_ref, sg_ref, o_ref = refs

    q = q_ref[...]
    if q_scale is not None:
        q = q.astype(F32) * q_scale
    lane = lax.broadcasted_iota(jnp.int32, q.shape, 1)
    zero = jnp.zeros_like(q)
    q0 = jnp.where(lane < C_QK_DIM, q, zero).astype(BF16)
    q1 = jnp.where(lane >= C_QK_DIM, q, zero).astype(BF16)

    lp = lam_ref[...]
    lam = (jnp.exp(jnp.sum(lp[0:1] * lp[1:2], axis=-1, keepdims=True))
           - jnp.exp(jnp.sum(lp[2:3] * lp[3:4], axis=-1, keepdims=True)) + LAMBDA_INIT_1)

    acc0, l0 = _softmax_pv(q0, kvs)
    acc1, l1 = _softmax_pv(q1, kvs)
    o = acc0 / l0 - lam * (acc1 / l1)
    ms = jnp.mean(o * o, axis=-1, keepdims=True)
    o = o * lax.rsqrt(ms + EPS) * sg_ref[...] * (1.0 - LAMBDA_INIT_1)
    gate = g_ref[...].astype(F32)
    o_ref[...] = (o * _silu(gate)).astype(o_ref.dtype)


def _diff(q_arr, q_col, k_arr, k_col, z, cache, lam_p, sub_g, nb, t, tq, q_scale):
    nq = t // tq
    v_col, g_col = 2 * C_HEADS, 3 * C_HEADS
    in_specs = [
        pl.BlockSpec((tq, LANES), lambda b, h, i: (b * nq + i, q_col + h)),
        pl.BlockSpec((t, LANES), lambda b, h, i: (b, k_col + h)),
        pl.BlockSpec((t, LANES), lambda b, h, i: (b, v_col + h)),
    ]
    args = [q_arr, k_arr, z]
    if cache is not None:
        in_specs += [pl.BlockSpec((PAST_LEN, LANES), lambda b, h, i: (b, h))] * 2
        args += list(cache)
    in_specs += [
        pl.BlockSpec((tq, LANES), lambda b, h, i: (b * nq + i, g_col + h)),
        pl.BlockSpec((4, C_QK_DIM), lambda b, h, i: (0, 0)),
        pl.BlockSpec((1, LANES), lambda b, h, i: (0, 0)),
    ]
    args += [z, lam_p, sub_g.reshape(1, LANES)]
    return pl.pallas_call(
        functools.partial(_diff_kernel, has_cache=cache is not None, q_scale=q_scale),
        grid=(nb, C_HEADS, nq),
        in_specs=in_specs,
        out_specs=pl.BlockSpec((tq, LANES), lambda b, h, i: (b * nq + i, h)),
        out_shape=jax.ShapeDtypeStruct((nb * t, C_WIDTH), BF16),
        compiler_params=_params("parallel", "parallel", "parallel"),
        name="diff_attn",
    )(*args)


def _out_kernel(*refs, n_in, final):
    y_refs = refs[:n_in]
    w_ref, x_ref, gate_ref = refs[n_in:n_in + 3]
    refs = refs[n_in + 3:]
    acc = None
    k0 = 0
    for y_ref in y_refs:
        kk = y_ref.shape[1]
        a = jnp.dot(y_ref[...], w_ref[k0:k0 + kk, :], preferred_element_type=F32)
        acc = a if acc is None else acc + a
        k0 += kk
    x = x_ref[...] + gate_ref[...] * acc
    if final:
        fg_ref, o_ref = refs
        ms = jnp.mean(x * x, axis=-1, keepdims=True)
        o_ref[...] = x * lax.rsqrt(ms + EPS) * fg_ref[...]
    else:
        (o_ref,) = refs
        o_ref[...] = x


def _out_proj(ys, w, x, mod3, row_of_tile, final_g=None, tm=512):
    t = x.shape[0]
    final = final_g is not None
    in_specs = [pl.BlockSpec((tm, y.shape[1]), lambda i: (i, 0)) for y in ys]
    in_specs += [
        pl.BlockSpec(w.shape, lambda i: (0, 0)),
        pl.BlockSpec((tm, D_MODEL), lambda i: (i, 0)),
        pl.BlockSpec((None, 1, D_MODEL), lambda i: (row_of_tile(i), 0, 2)),
    ]
    args = list(ys) + [w, x, mod3]
    if final:
        in_specs.append(pl.BlockSpec((1, D_MODEL), lambda i: (0, 0)))
        args.append(final_g.reshape(1, D_MODEL))
    return pl.pallas_call(
        functools.partial(_out_kernel, n_in=len(ys), final=final),
        grid=(t // tm,),
        in_specs=in_specs,
        out_specs=pl.BlockSpec((tm, D_MODEL), lambda i: (i, 0)),
        out_shape=jax.ShapeDtypeStruct((t, D_MODEL), F32),
        compiler_params=_params("parallel"),
        name="out_proj",
    )(*args)


def kernel(x_prompt, x_sample, cache_k0, cache_v0, cache_k1, cache_v1, c, c_ctx, w_ada0, b_ada0, norm_g0, w_in0, w_s0, b_s0, q_norm_g0, k_norm_g0, w_out0, w_ada1, b_ada1, norm_g1, w_in1, lambda_q1, lambda_k1, lambda_q2, lambda_k2, subln_g1, w_out1, final_g):
    n_ctx, n_smp = x_prompt.shape[0], x_sample.shape[0]

    cond = jnp.concatenate([c_ctx[None], c, jnp.zeros((8 - 1 - n_smp, D_MODEL), F32)], axis=0)
    mod0 = _ada(cond, w_ada0, b_ada0).reshape(8, 1, 3 * D_MODEL)
    mod1 = _ada(cond, w_ada1, b_ada1).reshape(8, 1, 3 * D_MODEL)

    w_in0_b, w_out0_b = w_in0.astype(BF16), w_out0.astype(BF16)
    w_in1_b, w_out1_b = w_in1.astype(BF16), w_out1.astype(BF16)
    ws_b = w_s0.astype(BF16)
    bs_b = jnp.broadcast_to(b_s0[:, :, None], (A_GROUPS, CHUNK, LANES))
    lam_p = jnp.stack([lambda_q1, lambda_k1, lambda_q2, lambda_k2]).astype(F32)
    tab0 = _rope_tables(B_HEAD_DIM // 4)
    tab1 = _rope_tables(C_QK_DIM // 4)

    def run(x, nb, t, smp):
        def row_fn(tm):
            if not smp:
                return lambda i: 0
            return lambda i: 1 + (i * tm) // t

        h = _norm_mod(x, norm_g0, mod0, row_fn(512), 512)
        z = _mm(h, w_in0_b, BF16 if smp else F32)
        y_a = _sgate(z, ws_b, bs_b)
        if smp:
            qp, kp, vp = _prep0(z, q_norm_g0, k_norm_g0, tab0, False)
            cache = (cache_k0.reshape(nb * PAST_LEN, B_KV_WIDTH), cache_v0.reshape(nb * PAST_LEN, B_KV_WIDTH))
            k0 = v0 = None
        else:
            qp, kp, vp, k0, v0 = _prep0(z, q_norm_g0, k_norm_g0, None, True)
            cache = None
        y_b = _gqa(qp, kp, vp, cache, z, nb, t, 256)
        x1 = _out_proj([y_a, y_b], w_out0_b, x, mod0, row_fn(512))

        h = _norm_mod(x1, norm_g1, mod1, row_fn(512), 512)
        z = _mm(h, w_in1_b, BF16 if smp else F32)
        if smp:
            qr, kr = _prep1(z, tab1)
            cache = (cache_k1.reshape(nb * PAST_LEN, C_WIDTH), cache_v1.reshape(nb * PAST_LEN, C_WIDTH))
            y_c = _diff(qr, 0, kr, 0, z, cache, lam_p, subln_g1, nb, t, 256, None)
            k1 = v1 = None
        else:
            y_c = _diff(z, 0, z, C_HEADS, z, None, lam_p, subln_g1, nb, t, 256, (C_QK_DIM ** -0.5) * LOG2E)
            k1 = z[:, C_WIDTH:2 * C_WIDTH]
            v1 = z[:, 2 * C_WIDTH:3 * C_WIDTH]
        y = _out_proj([y_c], w_out1_b, x1, mod1, row_fn(512), final_g)
        return y, k0, v0, k1, v1

    y_p, k0, v0, k1, v1 = run(x_prompt.reshape(n_ctx * SEQ, D_MODEL), n_ctx, SEQ, False)
    y_s, _, _, _, _ = run(x_sample.reshape(n_smp * DEC_SEQ, D_MODEL), n_smp, DEC_SEQ, True)

    return (
        y_p.reshape(n_ctx, SEQ, D_MODEL),
        y_s.reshape(n_smp, DEC_SEQ, D_MODEL),
        k0.reshape(n_ctx, SEQ, B_KV_HEADS, B_HEAD_DIM),
        v0.reshape(n_ctx, SEQ, B_KV_HEADS, B_HEAD_DIM),
        k1.reshape(n_ctx, SEQ, C_HEADS, 2, C_QK_DIM),
        v1.reshape(n_ctx, SEQ, C_HEADS, C_V_DIM),
    )
```

```python
import functools
import math

import jax
import jax.numpy as jnp
from jax import lax
from jax.experimental import pallas as pl
from jax.experimental.pallas import tpu as pltpu

F32 = jnp.float32
BF16 = jnp.bfloat16

D_MODEL = 2048
SEQ = 256
DEC_SEQ = 4096
PAST_LEN = 256
GRID_W = 64
CHUNK = 128
ROPE_THETA = 10000.0
EPS = 1e-6

A_GROUPS = 8
A_WIDTH = 1024
B_HEADS = 8
B_KV_HEADS = 2
B_GROUP = B_HEADS // B_KV_HEADS
B_HEAD_DIM = 128
B_WIDTH = 1024
B_KV_WIDTH = 256
IN0_WIDTH = 3 * A_WIDTH + 2 * B_WIDTH + 2 * B_KV_WIDTH
C_HEADS = 16
C_QK_DIM = 64
C_V_DIM = 128
C_WIDTH = 2048
IN1_WIDTH = 4 * C_WIDTH
LAMBDA_INIT_1 = 0.8 - 0.6 * math.exp(-0.3 * 1)

LANES = 128
LOG2E = math.log2(math.e)
VMEM_LIMIT = 56 * 1024 * 1024

Q0_OFF = 3 * A_WIDTH
K0_OFF = Q0_OFF + B_WIDTH
V0_OFF = K0_OFF + B_KV_WIDTH
G0_OFF = V0_OFF + B_KV_WIDTH


def _params(*sem):
    return pltpu.CompilerParams(dimension_semantics=sem, vmem_limit_bytes=VMEM_LIMIT)


def _silu(x):
    return x * jax.nn.sigmoid(x)


def _ada_kernel(c_ref, w_ref, b_ref, o_ref):
    s = _silu(c_ref[...]).astype(BF16)
    o_ref[...] = jnp.dot(s, w_ref[...].astype(BF16), preferred_element_type=F32) + b_ref[...]


def _ada(cond, w_ada, b_ada):
    bn = 512
    n = w_ada.shape[1]
    return pl.pallas_call(
        _ada_kernel,
        grid=(n // bn,),
        in_specs=[
            pl.BlockSpec((8, D_MODEL), lambda j: (0, 0)),
            pl.BlockSpec((D_MODEL, bn), lambda j: (0, j)),
            pl.BlockSpec((1, bn), lambda j: (0, j)),
        ],
        out_specs=pl.BlockSpec((8, bn), lambda j: (0, j)),
        out_shape=jax.ShapeDtypeStruct((8, n), F32),
        compiler_params=_params("parallel"),
        name="ada",
    )(cond, w_ada, b_ada.reshape(1, n))


def _norm_mod_kernel(x_ref, g_ref, shift_ref, scale_ref, h_ref):
    x = x_ref[...]
    ms = jnp.mean(x * x, axis=-1, keepdims=True)
    xn = x * lax.rsqrt(ms + EPS) * g_ref[...]
    h_ref[...] = (xn * (1.0 + scale_ref[...]) + shift_ref[...]).astype(h_ref.dtype)


def _norm_mod(x, norm_g, mod3, row_of_tile, tm):
    t = x.shape[0]
    return pl.pallas_call(
        _norm_mod_kernel,
        grid=(t // tm,),
        in_specs=[
            pl.BlockSpec((tm, D_MODEL), lambda i: (i, 0)),
            pl.BlockSpec((1, D_MODEL), lambda i: (0, 0)),
            pl.BlockSpec((None, 1, D_MODEL), lambda i: (row_of_tile(i), 0, 0)),
            pl.BlockSpec((None, 1, D_MODEL), lambda i: (row_of_tile(i), 0, 1)),
        ],
        out_specs=pl.BlockSpec((tm, D_MODEL), lambda i: (i, 0)),
        out_shape=jax.ShapeDtypeStruct((t, D_MODEL), BF16),
        compiler_params=_params("parallel"),
        name="norm_mod",
    )(x, norm_g.reshape(1, D_MODEL), mod3, mod3)


def _mm_kernel(h_ref, w_ref, o_ref):
    o_ref[...] = jnp.dot(h_ref[...], w_ref[...], preferred_element_type=F32).astype(o_ref.dtype)


def _mm(h, w, out_dtype, tm=1024):
    t, k = h.shape
    n = w.shape[1]
    tn = 1024 if n % 1024 == 0 else 512
    assert t % tm == 0 and n % tn == 0
    return pl.pallas_call(
        _mm_kernel,
        grid=(t // tm, n // tn),
        in_specs=[
            pl.BlockSpec((tm, k), lambda i, j: (i, 0)),
            pl.BlockSpec((k, tn), lambda i, j: (0, j)),
        ],
        out_specs=pl.BlockSpec((tm, tn), lambda i, j: (i, j)),
        out_shape=jax.ShapeDtypeStruct((t, n), out_dtype),
        compiler_params=_params("parallel", "parallel"),
        name="in_proj",
    )(h, w)


def _sgate_kernel(u_ref, v_ref, g_ref, ws_ref, bs_ref, o_ref):
    v = v_ref[...].astype(F32)
    mu = jnp.mean(v, axis=-1, keepdims=True)
    vc = v - mu
    var = jnp.mean(vc * vc, axis=-1, keepdims=True)
    vn = (vc * lax.rsqrt(var + EPS)).astype(BF16)
    tm = v.shape[0]
    for ch in range(tm // CHUNK):
        rows = slice(ch * CHUNK, (ch + 1) * CHUNK)
        for g in range(A_GROUPS):
            cols = slice(g * LANES, (g + 1) * LANES)
            s = jnp.dot(ws_ref[g], vn[rows, cols], preferred_element_type=F32) + bs_ref[g]
            u = u_ref[rows, cols].astype(F32)
            gate = g_ref[rows, cols].astype(F32)
            o_ref[rows, cols] = (u * s * _silu(gate)).astype(o_ref.dtype)


def _sgate(z, ws_b, bs_b, tm=256):
    t = z.shape[0]
    blk = lambda c: pl.BlockSpec((tm, A_WIDTH), lambda i, c=c: (i, c))
    full = pl.BlockSpec((A_GROUPS, CHUNK, CHUNK), lambda i: (0, 0, 0))
    return pl.pallas_call(
        _sgate_kernel,
        grid=(t // tm,),
        in_specs=[blk(0), blk(1), blk(2), full, full],
        out_specs=pl.BlockSpec((tm, A_WIDTH), lambda i: (i, 0)),
        out_shape=jax.ShapeDtypeStruct((t, A_WIDTH), BF16),
        compiler_params=_params("parallel"),
        name="spatial_gate",
    )(z, z, z, ws_b, bs_b)


def _rope_tables(half):
    rows = DEC_SEQ // GRID_W
    row_pos = jnp.repeat(jnp.arange(rows, dtype=F32), GRID_W)
    col_pos = jnp.tile(jnp.arange(GRID_W, dtype=F32), rows)
    freqs = ROPE_THETA ** (-jnp.arange(half, dtype=F32) / half)
    ang_r = row_pos[:, None] * freqs[None, :]
    ang_c = col_pos[:, None] * freqs[None, :]
    cr, sr, cc, sc = jnp.cos(ang_r), jnp.sin(ang_r), jnp.cos(ang_c), jnp.sin(ang_c)
    z = jnp.zeros_like(sr)
    reps = LANES // (4 * half)
    cos_t = jnp.tile(jnp.concatenate([cr, cr, cc, cc], axis=-1), (1, reps))
    sin_a = jnp.tile(jnp.concatenate([z, sr, z, sc], axis=-1), (1, reps))
    sin_b = jnp.tile(jnp.concatenate([-sr, z, -sc, z], axis=-1), (1, reps))
    return cos_t, sin_a, sin_b


def _rope(x, cos_t, sin_a, sin_b, half):
    return x * cos_t + pltpu.roll(x, half, 1) * sin_a + pltpu.roll(x, LANES - half, 1) * sin_b


def _prep0_kernel(*refs, rope, emit_f32):
    q_ref, k_ref, v_ref, qg_ref, kg_ref = refs[:5]
    refs = refs[5:]
    if rope:
        cos_ref, sa_ref, sb_ref = refs[:3]
        refs = refs[3:]
    qo_ref, ko_ref, vo_ref = refs[:3]
    if emit_f32:
        kf_ref, vf_ref = refs[3:5]

    def norm(x, g):
        ms = jnp.mean(x * x, axis=-1, keepdims=True)
        return x * lax.rsqrt(ms + EPS) * g

    def rot(x):
        if not rope:
            return x
        return _rope(x, cos_ref[...], sa_ref[...], sb_ref[...], B_HEAD_DIM // 4)

    qscale = (B_HEAD_DIM ** -0.5) * LOG2E
    for h in range(B_HEADS):
        cols = slice(h * LANES, (h + 1) * LANES)
        qn = norm(q_ref[:, cols].astype(F32), qg_ref[...])
        qo_ref[:, cols] = (rot(qn) * qscale).astype(qo_ref.dtype)
    for h in range(B_KV_HEADS):
        cols = slice(h * LANES, (h + 1) * LANES)
        kn = norm(k_ref[:, cols].astype(F32), kg_ref[...])
        ko_ref[:, cols] = rot(kn).astype(ko_ref.dtype)
        if emit_f32:
            kf_ref[:, cols] = kn
    v = v_ref[...]
    vo_ref[...] = v.astype(vo_ref.dtype)
    if emit_f32:
        vf_ref[...] = v.astype(F32)


def _prep0(z, q_g, k_g, tables, emit_f32, tm=256):
    t = z.shape[0]
    rope = tables is not None
    in_specs = [
        pl.BlockSpec((tm, B_WIDTH), lambda i: (i, Q0_OFF // B_WIDTH)),
        pl.BlockSpec((tm, B_KV_WIDTH), lambda i: (i, K0_OFF // B_KV_WIDTH)),
        pl.BlockSpec((tm, B_KV_WIDTH), lambda i: (i, V0_OFF // B_KV_WIDTH)),
        pl.BlockSpec((1, LANES), lambda i: (0, 0)),
        pl.BlockSpec((1, LANES), lambda i: (0, 0)),
    ]
    args = [z, z, z, q_g.reshape(1, LANES), k_g.reshape(1, LANES)]
    if rope:
        nt = DEC_SEQ // tm
        in_specs += [pl.BlockSpec((tm, LANES), lambda i: (i % nt, 0))] * 3
        args += list(tables)
    out_specs = [
        pl.BlockSpec((tm, B_WIDTH), lambda i: (i, 0)),
        pl.BlockSpec((tm, B_KV_WIDTH), lambda i: (i, 0)),
        pl.BlockSpec((tm, B_KV_WIDTH), lambda i: (i, 0)),
    ]
    out_shape = [
        jax.ShapeDtypeStruct((t, B_WIDTH), BF16),
        jax.ShapeDtypeStruct((t, B_KV_WIDTH), BF16),
        jax.ShapeDtypeStruct((t, B_KV_WIDTH), BF16),
    ]
    if emit_f32:
        out_specs += [pl.BlockSpec((tm, B_KV_WIDTH), lambda i: (i, 0))] * 2
        out_shape += [jax.ShapeDtypeStruct((t, B_KV_WIDTH), F32)] * 2
    return pl.pallas_call(
        functools.partial(_prep0_kernel, rope=rope, emit_f32=emit_f32),
        grid=(t // tm,),
        in_specs=in_specs,
        out_specs=out_specs,
        out_shape=out_shape,
        compiler_params=_params("parallel"),
        name="prep0",
    )(*args)


def _prep1_kernel(q_ref, k_ref, cos_ref, sa_ref, sb_ref, qo_ref, ko_ref):
    cos_t, sa, sb = cos_ref[...], sa_ref[...], sb_ref[...]
    qscale = (C_QK_DIM ** -0.5) * LOG2E
    for h in range(C_HEADS):
        cols = slice(h * LANES, (h + 1) * LANES)
        q = _rope(q_ref[:, cols].astype(F32), cos_t, sa, sb, C_QK_DIM // 4)
        qo_ref[:, cols] = (q * qscale).astype(qo_ref.dtype)
        k = _rope(k_ref[:, cols].astype(F32), cos_t, sa, sb, C_QK_DIM // 4)
        ko_ref[:, cols] = k.astype(ko_ref.dtype)


def _prep1(z, tables, tm=256):
    t = z.shape[0]
    nt = DEC_SEQ // tm
    tab = pl.BlockSpec((tm, LANES), lambda i: (i % nt, 0))
    return pl.pallas_call(
        _prep1_kernel,
        grid=(t // tm,),
        in_specs=[
            pl.BlockSpec((tm, C_WIDTH), lambda i: (i, 0)),
            pl.BlockSpec((tm, C_WIDTH), lambda i: (i, 1)),
            tab, tab, tab,
        ],
        out_specs=[pl.BlockSpec((tm, C_WIDTH), lambda i: (i, 0))] * 2,
        out_shape=[jax.ShapeDtypeStruct((t, C_WIDTH), BF16)] * 2,
        compiler_params=_params("parallel"),
        name="prep1",
    )(z, z, *tables)


KEY_CHUNK = 256
SUBLANES = 8


MAX_KEY_BLOCK = 1024


def _key_blocks(s_new, has_cache):
    size = min(s_new, MAX_KEY_BLOCK)
    blocks = [(r, size) for r in range(0, s_new, size)]
    if has_cache:
        blocks.append((s_new, PAST_LEN))
    return blocks


def _attn_pipeline(units, blocks, s_scr, p_scr):
    n = s_scr.shape[2]
    st = [dict() for _ in units]

    def stage1(u, b):
        r0, size = blocks[b]
        if b == 0:
            st[u]["qq"] = units[u][0]()
        s = lax.dot_general(units[u][1](b), st[u]["qq"], (((1,), (1,)), ((), ())), preferred_element_type=F32)
        s_scr[u % 2, r0:r0 + size, :] = s
        m8 = s.reshape(size // SUBLANES, SUBLANES, n).max(axis=0)
        st[u]["m8"] = m8 if b == 0 else jnp.maximum(st[u]["m8"], m8)
        if b == len(blocks) - 1:
            st[u]["m"] = st[u]["m8"].max(axis=0, keepdims=True)

    def stage2(u, b):
        r0, size = blocks[b]
        for r in range(r0, r0 + size, KEY_CHUNK):
            p = jnp.exp2(s_scr[u % 2, r:r + KEY_CHUNK, :] - st[u]["m"])
            l8 = p.reshape(KEY_CHUNK // SUBLANES, SUBLANES, n).sum(axis=0)
            st[u]["l8"] = l8 if r == 0 else st[u]["l8"] + l8
            p_scr[u % 2, r:r + KEY_CHUNK, :] = p.astype(BF16)

    def stage3(u, b):
        r0, size = blocks[b]
        a = jnp.dot(units[u][2][:, r0:r0 + size], p_scr[u % 2, r0:r0 + size, :], preferred_element_type=F32)
        st[u]["acc"] = a if b == 0 else st[u]["acc"] + a
        if b == len(blocks) - 1:
            l = st[u]["l8"].sum(axis=0, keepdims=True)
            units[u][3](st[u]["acc"] * (1.0 / l))

    for slot in range(len(units) + 2):
        for b in range(len(blocks)):
            if 0 <= slot - 2 < len(units):
                stage3(slot - 2, b)
            if slot < len(units):
                stage1(slot, b)
            if 0 <= slot - 1 < len(units):
                stage2(slot - 1, b)


def _fill_vt(vt_scr, v_ref, vc_ref, n_kv):
    s_new = v_ref.shape[0]
    for j in range(n_kv):
        cols = slice(j * LANES, (j + 1) * LANES)
        vt_scr[j, :, 0:s_new] = v_ref[:, cols].astype(F32).T.astype(BF16)
        if vc_ref is not None:
            vt_scr[j, :, s_new:] = vc_ref[:, cols].astype(F32).T.astype(BF16)


def _attn_scratch(n_kv, s_tot, n):
    return [
        pltpu.VMEM((n_kv, LANES, s_tot), BF16),
        pltpu.VMEM((2, s_tot, n), F32),
        pltpu.VMEM((2, s_tot, n), BF16),
    ]


TQ = 256


def _gqa_kernel(*refs, has_cache):
    q_ref, k_ref, v_ref = refs[:3]
    refs = refs[3:]
    kc_ref = vc_ref = None
    if has_cache:
        (kc_ref, vc_ref), refs = refs[:2], refs[2:]
    g_ref, o_ref, vt_scr, s_scr, p_scr = refs
    blocks = _key_blocks(k_ref.shape[0], has_cache)

    @pl.when(pl.program_id(2) == 0)
    def _():
        _fill_vt(vt_scr, v_ref, vc_ref, 1)

    def key_block(b):
        r0, size = blocks[b]
        if r0 < k_ref.shape[0]:
            return k_ref[r0:r0 + size, :].astype(BF16)
        return kc_ref[...].astype(BF16)

    def make_unit(rows, heads):
        def make_qq():
            return jnp.concatenate([q_ref[rows, h * LANES:(h + 1) * LANES] for h in heads], axis=0)

        def finish(ot):
            for i, h in enumerate(heads):
                cols = slice(h * LANES, (h + 1) * LANES)
                gate = g_ref[rows, cols].astype(F32)
                o_ref[rows, cols] = (ot[:, i * TQ:(i + 1) * TQ].T * _silu(gate)).astype(o_ref.dtype)

        return make_qq, key_block, vt_scr.at[0], finish

    units = [make_unit(slice(r, r + TQ), (h, h + 1))
             for r in range(0, q_ref.shape[0], TQ) for h in range(0, B_GROUP, 2)]
    _attn_pipeline(units, blocks, s_scr, p_scr)


def _gqa(qp, kp, vp, cache, z, nb, t, tqs):
    nq = t // tqs
    qw = B_GROUP * LANES
    s_tot = t + (PAST_LEN if cache is not None else 0)
    in_specs = [
        pl.BlockSpec((tqs, qw), lambda b, h, i: (b * nq + i, h)),
        pl.BlockSpec((t, LANES), lambda b, h, i: (b, h)),
        pl.BlockSpec((t, LANES), lambda b, h, i: (b, h)),
    ]
    args = [qp, kp, vp]
    if cache is not None:
        in_specs += [pl.BlockSpec((PAST_LEN, LANES), lambda b, h, i: (b, h))] * 2
        args += list(cache)
    in_specs.append(pl.BlockSpec((tqs, qw), lambda b, h, i: (b * nq + i, G0_OFF // qw + h)))
    args.append(z)
    return pl.pallas_call(
        functools.partial(_gqa_kernel, has_cache=cache is not None),
        grid=(nb, B_KV_HEADS, nq),
        in_specs=in_specs,
        out_specs=pl.BlockSpec((tqs, qw), lambda b, h, i: (b * nq + i, h)),
        out_shape=jax.ShapeDtypeStruct((nb * t, B_WIDTH), BF16),
        scratch_shapes=_attn_scratch(1, s_tot, 2 * TQ),
        compiler_params=_params("parallel", "parallel", "arbitrary"),
        name="gqa_attn",
    )(*args)


def _diff_kernel(*refs, has_cache, hb, q_scale):
    q_ref, k_ref, v_ref = refs[:3]
    refs = refs[3:]
    kc_ref = vc_ref = None
    if has_cache:
        (kc_ref, vc_ref), refs = refs[:2], refs[2:]
    g_ref, lam_ref, sg_ref, o_ref, vt_scr, s_scr, p_scr = refs
    blocks = _key_blocks(k_ref.shape[0], has_cache)

    @pl.when(pl.program_id(2) == 0)
    def _():
        _fill_vt(vt_scr, v_ref, vc_ref, hb)

    lp = lam_ref[...]
    lam = (jnp.exp(jnp.sum(lp[0:1] * lp[1:2], axis=-1, keepdims=True))
           - jnp.exp(jnp.sum(lp[2:3] * lp[3:4], axis=-1, keepdims=True)) + LAMBDA_INIT_1)
    lane = lax.broadcasted_iota(jnp.int32, (TQ, LANES), 1)

    def make_unit(rows, j):
        cols = slice(j * LANES, (j + 1) * LANES)

        def make_qq():
            q = q_ref[rows, cols]
            if q_scale is not None:
                q = q.astype(F32) * q_scale
            zero = jnp.zeros_like(q)
            maps = [jnp.where(lane < C_QK_DIM, q, zero), jnp.where(lane >= C_QK_DIM, q, zero)]
            return jnp.concatenate(maps, axis=0).astype(BF16)

        def key_block(b):
            r0, size = blocks[b]
            if r0 < k_ref.shape[0]:
                return k_ref[r0:r0 + size, cols].astype(BF16)
            return kc_ref[:, cols].astype(BF16)

        def finish(ot):
            ot = ot[:, :TQ] - lam * ot[:, TQ:]
            ms = jnp.mean(ot * ot, axis=0, keepdims=True)
            o = (ot * lax.rsqrt(ms + EPS)).T * sg_ref[...] * (1.0 - LAMBDA_INIT_1)
            gate = g_ref[rows, cols].astype(F32)
            o_ref[rows, cols] = (o * _silu(gate)).astype(o_ref.dtype)

        return make_qq, key_block, vt_scr.at[j], finish

    units = [make_unit(slice(r, r + TQ), j) for j in range(hb) for r in range(0, q_ref.shape[0], TQ)]
    _attn_pipeline(units, blocks, s_scr, p_scr)


def _diff(q_arr, q_col, k_arr, k_col, z, cache, lam_p, sub_g, nb, t, tq, hb, q_scale):
    nq = t // tq
    w = hb * LANES
    v_col, g_col = 2 * C_WIDTH // w, 3 * C_WIDTH // w
    s_tot = t + (PAST_LEN if cache is not None else 0)
    in_specs = [
        pl.BlockSpec((tq, w), lambda b, h, i: (b * nq + i, q_col + h)),
        pl.BlockSpec((t, w), lambda b, h, i: (b, k_col + h)),
        pl.BlockSpec((t, w), lambda b, h, i: (b, v_col + h)),
    ]
    args = [q_arr, k_arr, z]
    if cache is not None:
        in_specs += [pl.BlockSpec((PAST_LEN, w), lambda b, h, i: (b, h))] * 2
        args += list(cache)
    in_specs += [
        pl.BlockSpec((tq, w), lambda b, h, i: (b * nq + i, g_col + h)),
        pl.BlockSpec((4, C_QK_DIM), lambda b, h, i: (0, 0)),
        pl.BlockSpec((1, LANES), lambda b, h, i: (0, 0)),
    ]
    args += [z, lam_p, sub_g.reshape(1, LANES)]
    return pl.pallas_call(
        functools.partial(_diff_kernel, has_cache=cache is not None, hb=hb, q_scale=q_scale),
        grid=(nb, C_HEADS // hb, nq),
        in_specs=in_specs,
        out_specs=pl.BlockSpec((tq, w), lambda b, h, i: (b * nq + i, h)),
        out_shape=jax.ShapeDtypeStruct((nb * t, C_WIDTH), BF16),
        scratch_shapes=_attn_scratch(hb, s_tot, 2 * TQ),
        compiler_params=_params("parallel", "parallel", "arbitrary"),
        name="diff_attn",
    )(*args)


def _out_kernel(*refs, n_in, final):
    y_refs = refs[:n_in]
    w_ref, x_ref, gate_ref = refs[n_in:n_in + 3]
    refs = refs[n_in + 3:]
    acc = None
    k0 = 0
    for y_ref in y_refs:
        kk = y_ref.shape[1]
        a = jnp.dot(y_ref[...], w_ref[k0:k0 + kk, :], preferred_element_type=F32)
        acc = a if acc is None else acc + a
        k0 += kk
    x = x_ref[...] + gate_ref[...] * acc
    if final:
        fg_ref, o_ref = refs
        ms = jnp.mean(x * x, axis=-1, keepdims=True)
        o_ref[...] = x * lax.rsqrt(ms + EPS) * fg_ref[...]
    else:
        (o_ref,) = refs
        o_ref[...] = x


def _out_proj(ys, w, x, mod3, row_of_tile, final_g=None, tm=512):
    t = x.shape[0]
    final = final_g is not None
    in_specs = [pl.BlockSpec((tm, y.shape[1]), lambda i: (i, 0)) for y in ys]
    in_specs += [
        pl.BlockSpec(w.shape, lambda i: (0, 0)),
        pl.BlockSpec((tm, D_MODEL), lambda i: (i, 0)),
        pl.BlockSpec((None, 1, D_MODEL), lambda i: (row_of_tile(i), 0, 2)),
    ]
    args = list(ys) + [w, x, mod3]
    if final:
        in_specs.append(pl.BlockSpec((1, D_MODEL), lambda i: (0, 0)))
        args.append(final_g.reshape(1, D_MODEL))
    return pl.pallas_call(
        functools.partial(_out_kernel, n_in=len(ys), final=final),
        grid=(t // tm,),
        in_specs=in_specs,
        out_specs=pl.BlockSpec((tm, D_MODEL), lambda i: (i, 0)),
        out_shape=jax.ShapeDtypeStruct((t, D_MODEL), F32),
        compiler_params=_params("parallel"),
        name="out_proj",
    )(*args)


def kernel(x_prompt, x_sample, cache_k0, cache_v0, cache_k1, cache_v1, c, c_ctx, w_ada0, b_ada0, norm_g0, w_in0, w_s0, b_s0, q_norm_g0, k_norm_g0, w_out0, w_ada1, b_ada1, norm_g1, w_in1, lambda_q1, lambda_k1, lambda_q2, lambda_k2, subln_g1, w_out1, final_g):
    n_ctx, n_smp = x_prompt.shape[0], x_sample.shape[0]

    cond = jnp.concatenate([c_ctx[None], c, jnp.zeros((8 - 1 - n_smp, D_MODEL), F32)], axis=0)
    mod0 = _ada(cond, w_ada0, b_ada0).reshape(8, 1, 3 * D_MODEL)
    mod1 = _ada(cond, w_ada1, b_ada1).reshape(8, 1, 3 * D_MODEL)

    w_in0_b, w_out0_b = w_in0.astype(BF16), w_out0.astype(BF16)
    w_in1_b, w_out1_b = w_in1.astype(BF16), w_out1.astype(BF16)
    ws_b = w_s0.astype(BF16)
    bs_b = jnp.broadcast_to(b_s0[:, :, None], (A_GROUPS, CHUNK, LANES))
    lam_p = jnp.stack([lambda_q1, lambda_k1, lambda_q2, lambda_k2]).astype(F32)
    tab0 = _rope_tables(B_HEAD_DIM // 4)
    tab1 = _rope_tables(C_QK_DIM // 4)

    def run(x, nb, t, smp):
        def row_fn(tm):
            if not smp:
                return lambda i: 0
            return lambda i: 1 + (i * tm) // t

        h = _norm_mod(x, norm_g0, mod0, row_fn(512), 512)
        z = _mm(h, w_in0_b, BF16 if smp else F32)
        y_a = _sgate(z, ws_b, bs_b)
        if smp:
            qp, kp, vp = _prep0(z, q_norm_g0, k_norm_g0, tab0, False)
            cache = (cache_k0.reshape(nb * PAST_LEN, B_KV_WIDTH), cache_v0.reshape(nb * PAST_LEN, B_KV_WIDTH))
            k0 = v0 = None
        else:
            qp, kp, vp, k0, v0 = _prep0(z, q_norm_g0, k_norm_g0, None, True)
            cache = None
        y_b = _gqa(qp, kp, vp, cache, z, nb, t, 512 if smp else SEQ)
        x1 = _out_proj([y_a, y_b], w_out0_b, x, mod0, row_fn(512))

        h = _norm_mod(x1, norm_g1, mod1, row_fn(512), 512)
        z = _mm(h, w_in1_b, BF16 if smp else F32)
        if smp:
            qr, kr = _prep1(z, tab1)
            cache = (cache_k1.reshape(nb * PAST_LEN, C_WIDTH), cache_v1.reshape(nb * PAST_LEN, C_WIDTH))
            y_c = _diff(qr, 0, kr, 0, z, cache, lam_p, subln_g1, nb, t, 1024, 1, None)
            k1 = v1 = None
        else:
            y_c = _diff(z, 0, z, C_WIDTH // (8 * LANES), z, None, lam_p, subln_g1, nb, t, 256, 8,
                        (C_QK_DIM ** -0.5) * LOG2E)
            k1 = z[:, C_WIDTH:2 * C_WIDTH]
            v1 = z[:, 2 * C_WIDTH:3 * C_WIDTH]
        y = _out_proj([y_c], w_out1_b, x1, mod1, row_fn(512), final_g)
        return y, k0, v0, k1, v1

    y_p, k0, v0, k1, v1 = run(x_prompt.reshape(n_ctx * SEQ, D_MODEL), n_ctx, SEQ, False)
    y_s, _, _, _, _ = run(x_sample.reshape(n_smp * DEC_SEQ, D_MODEL), n_smp, DEC_SEQ, True)

    return (
        y_p.reshape(n_ctx, SEQ, D_MODEL),
        y_s.reshape(n_smp, DEC_SEQ, D_MODEL),
        k0.reshape(n_ctx, SEQ, B_KV_HEADS, B_HEAD_DIM),
        v0.reshape(n_ctx, SEQ, B_KV_HEADS, B_HEAD_DIM),
        k1.reshape(n_ctx, SEQ, C_HEADS, 2, C_QK_DIM),
        v1.reshape(n_ctx, SEQ, C_HEADS, C_V_DIM),
    )
```

```python
import functools
import math

import jax
import jax.numpy as jnp
from jax import lax
from jax.experimental import pallas as pl
from jax.experimental.pallas import tpu as pltpu

F32 = jnp.float32
BF16 = jnp.bfloat16

D_MODEL = 2048
SEQ = 256
DEC_SEQ = 4096
PAST_LEN = 256
GRID_W = 64
CHUNK = 128
ROPE_THETA = 10000.0
EPS = 1e-6

A_GROUPS = 8
A_WIDTH = 1024
B_HEADS = 8
B_KV_HEADS = 2
B_GROUP = B_HEADS // B_KV_HEADS
B_HEAD_DIM = 128
B_WIDTH = 1024
B_KV_WIDTH = 256
IN0_WIDTH = 3 * A_WIDTH + 2 * B_WIDTH + 2 * B_KV_WIDTH
C_HEADS = 16
C_QK_DIM = 64
C_V_DIM = 128
C_WIDTH = 2048
IN1_WIDTH = 4 * C_WIDTH
LAMBDA_INIT_1 = 0.8 - 0.6 * math.exp(-0.3 * 1)

LANES = 128
LOG2E = math.log2(math.e)
VMEM_LIMIT = 56 * 1024 * 1024

Q0_OFF = 3 * A_WIDTH
K0_OFF = Q0_OFF + B_WIDTH
V0_OFF = K0_OFF + B_KV_WIDTH
G0_OFF = V0_OFF + B_KV_WIDTH


def _params(*sem):
    return pltpu.CompilerParams(dimension_semantics=sem, vmem_limit_bytes=VMEM_LIMIT)


def _silu(x):
    return x * jax.nn.sigmoid(x)


def _ada_kernel(c_ref, w_ref, b_ref, o_ref):
    s = _silu(c_ref[...]).astype(BF16)
    o_ref[...] = jnp.dot(s, w_ref[...].astype(BF16), preferred_element_type=F32) + b_ref[...]


def _ada(cond, w_ada, b_ada):
    bn = 512
    n = w_ada.shape[1]
    return pl.pallas_call(
        _ada_kernel,
        grid=(n // bn,),
        in_specs=[
            pl.BlockSpec((8, D_MODEL), lambda j: (0, 0)),
            pl.BlockSpec((D_MODEL, bn), lambda j: (0, j)),
            pl.BlockSpec((1, bn), lambda j: (0, j)),
        ],
        out_specs=pl.BlockSpec((8, bn), lambda j: (0, j)),
        out_shape=jax.ShapeDtypeStruct((8, n), F32),
        compiler_params=_params("parallel"),
        name="ada",
    )(cond, w_ada, b_ada.reshape(1, n))


def _norm_mod_kernel(x_ref, g_ref, shift_ref, scale_ref, h_ref):
    x = x_ref[...]
    ms = jnp.mean(x * x, axis=-1, keepdims=True)
    xn = x * lax.rsqrt(ms + EPS) * g_ref[...]
    h_ref[...] = (xn * (1.0 + scale_ref[...]) + shift_ref[...]).astype(h_ref.dtype)


def _norm_mod(x, norm_g, mod3, row_of_tile, tm):
    t = x.shape[0]
    return pl.pallas_call(
        _norm_mod_kernel,
        grid=(t // tm,),
        in_specs=[
            pl.BlockSpec((tm, D_MODEL), lambda i: (i, 0)),
            pl.BlockSpec((1, D_MODEL), lambda i: (0, 0)),
            pl.BlockSpec((None, 1, D_MODEL), lambda i: (row_of_tile(i), 0, 0)),
            pl.BlockSpec((None, 1, D_MODEL), lambda i: (row_of_tile(i), 0, 1)),
        ],
        out_specs=pl.BlockSpec((tm, D_MODEL), lambda i: (i, 0)),
        out_shape=jax.ShapeDtypeStruct((t, D_MODEL), BF16),
        compiler_params=_params("parallel"),
        name="norm_mod",
    )(x, norm_g.reshape(1, D_MODEL), mod3, mod3)


def _mm_kernel(h_ref, w_ref, o_ref):
    o_ref[...] = jnp.dot(h_ref[...], w_ref[...], preferred_element_type=F32).astype(o_ref.dtype)


def _mm(h, w, out_dtype, tn, col_blocks=None, tm=1024):
    t, k = h.shape
    first, stride, count = col_blocks if col_blocks is not None else (0, 1, w.shape[1] // tn)
    assert t % tm == 0 and w.shape[1] % tn == 0
    return pl.pallas_call(
        _mm_kernel,
        grid=(t // tm, count),
        in_specs=[
            pl.BlockSpec((tm, k), lambda i, j: (i, 0)),
            pl.BlockSpec((k, tn), lambda i, j: (0, first + stride * j)),
        ],
        out_specs=pl.BlockSpec((tm, tn), lambda i, j: (i, j)),
        out_shape=jax.ShapeDtypeStruct((t, count * tn), out_dtype),
        compiler_params=_params("parallel", "parallel"),
        name="in_proj",
    )(h, w)


def _mm_kt_kernel(h_ref, w_ref, kb_ref, kt_ref):
    acc = jnp.dot(h_ref[...], w_ref[...], preferred_element_type=F32)
    kb_ref[...] = acc.astype(kb_ref.dtype)
    for b in range(kt_ref.shape[0]):
        kt_ref[b] = acc[b * SEQ:(b + 1) * SEQ, :].T


def _mm_kt(h, w, col_block, n_batch, tb=2):
    t, k = h.shape
    tm = tb * SEQ
    return pl.pallas_call(
        _mm_kt_kernel,
        grid=(t // tm,),
        in_specs=[
            pl.BlockSpec((tm, k), lambda i: (i, 0)),
            pl.BlockSpec((k, C_WIDTH), lambda i: (0, col_block)),
        ],
        out_specs=[
            pl.BlockSpec((tm, C_WIDTH), lambda i: (i, 0)),
            pl.BlockSpec((tb, C_WIDTH, SEQ), lambda i: (i, 0, 0)),
        ],
        out_shape=[
            jax.ShapeDtypeStruct((t, C_WIDTH), BF16),
            jax.ShapeDtypeStruct((n_batch, C_WIDTH, SEQ), F32),
        ],
        compiler_params=_params("parallel"),
        name="in_proj_kt",
    )(h, w)


def _sgate_kernel(u_ref, v_ref, g_ref, ws_ref, bs_ref, o_ref):
    v = v_ref[...].astype(F32)
    mu = jnp.mean(v, axis=-1, keepdims=True)
    vc = v - mu
    var = jnp.mean(vc * vc, axis=-1, keepdims=True)
    vn = (vc * lax.rsqrt(var + EPS)).astype(BF16)
    tm = v.shape[0]
    for ch in range(tm // CHUNK):
        rows = slice(ch * CHUNK, (ch + 1) * CHUNK)
        for g in range(A_GROUPS):
            cols = slice(g * LANES, (g + 1) * LANES)
            s = jnp.dot(ws_ref[g], vn[rows, cols], preferred_element_type=F32) + bs_ref[g]
            u = u_ref[rows, cols].astype(F32)
            gate = g_ref[rows, cols].astype(F32)
            o_ref[rows, cols] = (u * s * _silu(gate)).astype(o_ref.dtype)


def _sgate(z, ws_b, bs_b, tm=256):
    t = z.shape[0]
    blk = lambda c: pl.BlockSpec((tm, A_WIDTH), lambda i, c=c: (i, c))
    full = pl.BlockSpec((A_GROUPS, CHUNK, CHUNK), lambda i: (0, 0, 0))
    return pl.pallas_call(
        _sgate_kernel,
        grid=(t // tm,),
        in_specs=[blk(0), blk(1), blk(2), full, full],
        out_specs=pl.BlockSpec((tm, A_WIDTH), lambda i: (i, 0)),
        out_shape=jax.ShapeDtypeStruct((t, A_WIDTH), BF16),
        compiler_params=_params("parallel"),
        name="spatial_gate",
    )(z, z, z, ws_b, bs_b)


def _rope_tables(half):
    rows = DEC_SEQ // GRID_W
    row_pos = jnp.repeat(jnp.arange(rows, dtype=F32), GRID_W)
    col_pos = jnp.tile(jnp.arange(GRID_W, dtype=F32), rows)
    freqs = ROPE_THETA ** (-jnp.arange(half, dtype=F32) / half)
    ang_r = row_pos[:, None] * freqs[None, :]
    ang_c = col_pos[:, None] * freqs[None, :]
    cr, sr, cc, sc = jnp.cos(ang_r), jnp.sin(ang_r), jnp.cos(ang_c), jnp.sin(ang_c)
    z = jnp.zeros_like(sr)
    reps = LANES // (4 * half)
    cos_t = jnp.tile(jnp.concatenate([cr, cr, cc, cc], axis=-1), (1, reps))
    sin_a = jnp.tile(jnp.concatenate([z, sr, z, sc], axis=-1), (1, reps))
    sin_b = jnp.tile(jnp.concatenate([-sr, z, -sc, z], axis=-1), (1, reps))
    return cos_t, sin_a, sin_b


def _rope(x, cos_t, sin_a, sin_b, half):
    return x * cos_t + pltpu.roll(x, half, 1) * sin_a + pltpu.roll(x, LANES - half, 1) * sin_b


def _prep0_kernel(*refs, rope, emit_f32):
    q_ref, k_ref, v_ref, qg_ref, kg_ref = refs[:5]
    refs = refs[5:]
    if rope:
        cos_ref, sa_ref, sb_ref = refs[:3]
        refs = refs[3:]
    qo_ref, ko_ref, vo_ref = refs[:3]
    if emit_f32:
        kf_ref, vf_ref = refs[3:5]

    def norm(x, g):
        ms = jnp.mean(x * x, axis=-1, keepdims=True)
        return x * lax.rsqrt(ms + EPS) * g

    def rot(x):
        if not rope:
            return x
        return _rope(x, cos_ref[...], sa_ref[...], sb_ref[...], B_HEAD_DIM // 4)

    qscale = (B_HEAD_DIM ** -0.5) * LOG2E
    for h in range(B_HEADS):
        cols = slice(h * LANES, (h + 1) * LANES)
        qn = norm(q_ref[:, cols].astype(F32), qg_ref[...])
        qo_ref[:, cols] = (rot(qn) * qscale).astype(qo_ref.dtype)
    for h in range(B_KV_HEADS):
        cols = slice(h * LANES, (h + 1) * LANES)
        kn = norm(k_ref[:, cols].astype(F32), kg_ref[...])
        ko_ref[:, cols] = rot(kn).astype(ko_ref.dtype)
        if emit_f32:
            kf_ref[:, cols] = kn
    v = v_ref[...]
    vo_ref[...] = v.astype(vo_ref.dtype)
    if emit_f32:
        vf_ref[...] = v.astype(F32)


def _prep0(z, q_g, k_g, tables, emit_f32, tm=256):
    t = z.shape[0]
    rope = tables is not None
    in_specs = [
        pl.BlockSpec((tm, B_WIDTH), lambda i: (i, Q0_OFF // B_WIDTH)),
        pl.BlockSpec((tm, B_KV_WIDTH), lambda i: (i, K0_OFF // B_KV_WIDTH)),
        pl.BlockSpec((tm, B_KV_WIDTH), lambda i: (i, V0_OFF // B_KV_WIDTH)),
        pl.BlockSpec((1, LANES), lambda i: (0, 0)),
        pl.BlockSpec((1, LANES), lambda i: (0, 0)),
    ]
    args = [z, z, z, q_g.reshape(1, LANES), k_g.reshape(1, LANES)]
    if rope:
        nt = DEC_SEQ // tm
        in_specs += [pl.BlockSpec((tm, LANES), lambda i: (i % nt, 0))] * 3
        args += list(tables)
    out_specs = [
        pl.BlockSpec((tm, B_WIDTH), lambda i: (i, 0)),
        pl.BlockSpec((tm, B_KV_WIDTH), lambda i: (i, 0)),
        pl.BlockSpec((tm, B_KV_WIDTH), lambda i: (i, 0)),
    ]
    out_shape = [
        jax.ShapeDtypeStruct((t, B_WIDTH), BF16),
        jax.ShapeDtypeStruct((t, B_KV_WIDTH), BF16),
        jax.ShapeDtypeStruct((t, B_KV_WIDTH), BF16),
    ]
    if emit_f32:
        out_specs += [pl.BlockSpec((tm, B_KV_WIDTH), lambda i: (i, 0))] * 2
        out_shape += [jax.ShapeDtypeStruct((t, B_KV_WIDTH), F32)] * 2
    return pl.pallas_call(
        functools.partial(_prep0_kernel, rope=rope, emit_f32=emit_f32),
        grid=(t // tm,),
        in_specs=in_specs,
        out_specs=out_specs,
        out_shape=out_shape,
        compiler_params=_params("parallel"),
        name="prep0",
    )(*args)


def _mm_rope_kernel(h_ref, w_ref, cos_ref, sa_ref, sb_ref, o_ref, *, q_tiles, q_scale):
    acc = jnp.dot(h_ref[...], w_ref[...], preferred_element_type=F32)
    scale = jnp.where(pl.program_id(1) < q_tiles, q_scale, 1.0)
    cos_t, sa, sb = cos_ref[...], sa_ref[...], sb_ref[...]
    for c in range(acc.shape[1] // LANES):
        cols = slice(c * LANES, (c + 1) * LANES)
        o_ref[:, cols] = (_rope(acc[:, cols], cos_t, sa, sb, C_QK_DIM // 4) * scale).astype(o_ref.dtype)


def _mm_rope(h, w, tables, q_scale, tm=1024, tn=1024):
    t, k = h.shape
    nt = DEC_SEQ // tm
    tab = pl.BlockSpec((tm, LANES), lambda i, j: (i % nt, 0))
    return pl.pallas_call(
        functools.partial(_mm_rope_kernel, q_tiles=C_WIDTH // tn, q_scale=q_scale),
        grid=(t // tm, 2 * C_WIDTH // tn),
        in_specs=[
            pl.BlockSpec((tm, k), lambda i, j: (i, 0)),
            pl.BlockSpec((k, tn), lambda i, j: (0, j)),
            tab, tab, tab,
        ],
        out_specs=pl.BlockSpec((tm, tn), lambda i, j: (i, j)),
        out_shape=jax.ShapeDtypeStruct((t, 2 * C_WIDTH), BF16),
        compiler_params=_params("parallel", "parallel"),
        name="in_proj_rope",
    )(h, w, *tables)


KEY_CHUNK = 256
SUBLANES = 8


MAX_KEY_BLOCK = 1024


def _key_blocks(s_new, has_cache):
    size = min(s_new, MAX_KEY_BLOCK)
    blocks = [(r, size) for r in range(0, s_new, size)]
    if has_cache:
        blocks.append((s_new, PAST_LEN))
    return blocks


def _attn_pipeline(units, blocks, s_scr, p_bufs):
    n = s_scr.shape[2]
    st = [dict() for _ in units]

    def stage1(u, b):
        r0, size = blocks[b]
        if b == 0:
            st[u]["qq"] = units[u][0]()
        s = lax.dot_general(units[u][1](b), st[u]["qq"], (((1,), (1,)), ((), ())), preferred_element_type=F32)
        s_scr[u % 2, r0:r0 + size, :] = s
        m8 = s.reshape(size // SUBLANES, SUBLANES, n).max(axis=0)
        st[u]["m8"] = m8 if b == 0 else jnp.maximum(st[u]["m8"], m8)
        if b == len(blocks) - 1:
            st[u]["m"] = st[u]["m8"].max(axis=0, keepdims=True)

    def stage2(u, b):
        r0, size = blocks[b]
        for r in range(r0, r0 + size, KEY_CHUNK):
            p = jnp.exp2(s_scr[u % 2, r:r + KEY_CHUNK, :] - st[u]["m"])
            l8 = p.reshape(KEY_CHUNK // SUBLANES, SUBLANES, n).sum(axis=0)
            st[u]["l8"] = l8 if r == 0 else st[u]["l8"] + l8
            p_bufs[u % 2][r:r + KEY_CHUNK, :] = p.astype(BF16)

    def stage3(u, b):
        r0, size = blocks[b]
        a = jnp.dot(units[u][2][:, r0:r0 + size], p_bufs[u % 2][r0:r0 + size, :], preferred_element_type=F32)
        st[u]["acc"] = a if b == 0 else st[u]["acc"] + a
        if b == len(blocks) - 1:
            l = st[u]["l8"].sum(axis=0, keepdims=True)
            units[u][3](st[u]["acc"] * (1.0 / l))

    for slot in range(len(units) + 2):
        for b in range(len(blocks)):
            if 0 <= slot - 2 < len(units):
                stage3(slot - 2, b)
            if slot < len(units):
                stage1(slot, b)
            if 0 <= slot - 1 < len(units):
                stage2(slot - 1, b)


def _fill_vt(vt_scr, v_ref, vc_ref, n_kv):
    s_new = v_ref.shape[0]
    for j in range(n_kv):
        cols = slice(j * LANES, (j + 1) * LANES)
        vt_scr[j, :, 0:s_new] = v_ref[:, cols].astype(F32).T.astype(BF16)
        if vc_ref is not None:
            vt_scr[j, :, s_new:] = vc_ref[:, cols].astype(F32).T.astype(BF16)


def _attn_scratch(n_kv, s_tot, n):
    return [
        pltpu.VMEM((n_kv, LANES, s_tot), BF16),
        pltpu.VMEM((2, s_tot, n), F32),
        pltpu.VMEM((1, s_tot, n), BF16),
        pltpu.VMEM((1, s_tot, n), BF16),
    ]


def _prob_bufs(p0_scr, p1_scr):
    return p0_scr.at[0], p1_scr.at[0]


TQ = 256


def _gqa_kernel(*refs, has_cache):
    q_ref, k_ref, v_ref = refs[:3]
    refs = refs[3:]
    kc_ref = vc_ref = None
    if has_cache:
        (kc_ref, vc_ref), refs = refs[:2], refs[2:]
    g_ref, o_ref, vt_scr, s_scr, p0_scr, p1_scr = refs
    p_bufs = _prob_bufs(p0_scr, p1_scr)
    blocks = _key_blocks(k_ref.shape[0], has_cache)

    @pl.when(pl.program_id(2) == 0)
    def _():
        _fill_vt(vt_scr, v_ref, vc_ref, 1)

    def key_block(b):
        r0, size = blocks[b]
        if r0 < k_ref.shape[0]:
            return k_ref[r0:r0 + size, :].astype(BF16)
        return kc_ref[...].astype(BF16)

    def make_unit(rows, heads):
        def make_qq():
            return jnp.concatenate([q_ref[rows, h * LANES:(h + 1) * LANES] for h in heads], axis=0)

        def finish(ot):
            for i, h in enumerate(heads):
                cols = slice(h * LANES, (h + 1) * LANES)
                gate = g_ref[rows, cols].astype(F32)
                o_ref[rows, cols] = (ot[:, i * TQ:(i + 1) * TQ].T * _silu(gate)).astype(o_ref.dtype)

        return make_qq, key_block, vt_scr.at[0], finish

    units = [make_unit(slice(r, r + TQ), (h, h + 1))
             for r in range(0, q_ref.shape[0], TQ) for h in range(0, B_GROUP, 2)]
    _attn_pipeline(units, blocks, s_scr, p_bufs)


def _gqa(qp, kp, vp, cache, z, nb, t, tqs):
    nq = t // tqs
    qw = B_GROUP * LANES
    s_tot = t + (PAST_LEN if cache is not None else 0)
    in_specs = [
        pl.BlockSpec((tqs, qw), lambda b, h, i: (b * nq + i, h)),
        pl.BlockSpec((t, LANES), lambda b, h, i: (b, h)),
        pl.BlockSpec((t, LANES), lambda b, h, i: (b, h)),
    ]
    args = [qp, kp, vp]
    if cache is not None:
        in_specs += [pl.BlockSpec((PAST_LEN, LANES), lambda b, h, i: (b, h))] * 2
        args += list(cache)
    in_specs.append(pl.BlockSpec((tqs, qw), lambda b, h, i: (b * nq + i, G0_OFF // qw + h)))
    args.append(z)
    return pl.pallas_call(
        functools.partial(_gqa_kernel, has_cache=cache is not None),
        grid=(nb, B_KV_HEADS, nq),
        in_specs=in_specs,
        out_specs=pl.BlockSpec((tqs, qw), lambda b, h, i: (b * nq + i, h)),
        out_shape=jax.ShapeDtypeStruct((nb * t, B_WIDTH), BF16),
        scratch_shapes=_attn_scratch(1, s_tot, 2 * TQ),
        compiler_params=_params("parallel", "parallel", "arbitrary"),
        name="gqa_attn",
    )(*args)


def _diff_kernel(*refs, has_cache, hb, q_scale):
    q_ref, k_ref, v_ref = refs[:3]
    refs = refs[3:]
    kc_ref = vc_ref = None
    if has_cache:
        (kc_ref, vc_ref), refs = refs[:2], refs[2:]
    g_ref, lam_ref, sg_ref, o_ref, vt_scr, s_scr, p0_scr, p1_scr = refs
    p_bufs = _prob_bufs(p0_scr, p1_scr)
    blocks = _key_blocks(k_ref.shape[0], has_cache)

    @pl.when(pl.program_id(2) == 0)
    def _():
        _fill_vt(vt_scr, v_ref, vc_ref, hb)

    lp = lam_ref[...]
    lam = (jnp.exp(jnp.sum(lp[0:1] * lp[1:2], axis=-1, keepdims=True))
           - jnp.exp(jnp.sum(lp[2:3] * lp[3:4], axis=-1, keepdims=True)) + LAMBDA_INIT_1)
    lane = lax.broadcasted_iota(jnp.int32, (TQ, LANES), 1)

    def make_unit(rows, j):
        cols = slice(j * LANES, (j + 1) * LANES)

        def make_qq():
            q = q_ref[rows, cols]
            if q_scale is not None:
                q = q.astype(F32) * q_scale
            zero = jnp.zeros_like(q)
            maps = [jnp.where(lane < C_QK_DIM, q, zero), jnp.where(lane >= C_QK_DIM, q, zero)]
            return jnp.concatenate(maps, axis=0).astype(BF16)

        def key_block(b):
            r0, size = blocks[b]
            if r0 < k_ref.shape[0]:
                return k_ref[r0:r0 + size, cols].astype(BF16)
            return kc_ref[:, cols].astype(BF16)

        def finish(ot):
            ot = ot[:, :TQ] - lam * ot[:, TQ:]
            ms = jnp.mean(ot * ot, axis=0, keepdims=True)
            o = (ot * lax.rsqrt(ms + EPS)).T * sg_ref[...] * (1.0 - LAMBDA_INIT_1)
            gate = g_ref[rows, cols].astype(F32)
            o_ref[rows, cols] = (o * _silu(gate)).astype(o_ref.dtype)

        return make_qq, key_block, vt_scr.at[j], finish

    units = [make_unit(slice(r, r + TQ), j) for j in range(hb) for r in range(0, q_ref.shape[0], TQ)]
    _attn_pipeline(units, blocks, s_scr, p_bufs)


def _diff(q, k, v, g, cache, lam_p, sub_g, nb, t, tq, hb, q_scale):
    nq = t // tq
    w = hb * LANES
    s_tot = t + (PAST_LEN if cache is not None else 0)
    in_specs = [
        pl.BlockSpec((tq, w), lambda b, h, i: (b * nq + i, q[1] + h)),
        pl.BlockSpec((t, w), lambda b, h, i: (b, k[1] + h)),
        pl.BlockSpec((t, w), lambda b, h, i: (b, v[1] + h)),
    ]
    args = [q[0], k[0], v[0]]
    if cache is not None:
        in_specs += [pl.BlockSpec((PAST_LEN, w), lambda b, h, i: (b, h))] * 2
        args += list(cache)
    in_specs += [
        pl.BlockSpec((tq, w), lambda b, h, i: (b * nq + i, g[1] + h)),
        pl.BlockSpec((4, C_QK_DIM), lambda b, h, i: (0, 0)),
        pl.BlockSpec((1, LANES), lambda b, h, i: (0, 0)),
    ]
    args += [g[0], lam_p, sub_g.reshape(1, LANES)]
    return pl.pallas_call(
        functools.partial(_diff_kernel, has_cache=cache is not None, hb=hb, q_scale=q_scale),
        grid=(nb, C_HEADS // hb, nq),
        in_specs=in_specs,
        out_specs=pl.BlockSpec((tq, w), lambda b, h, i: (b * nq + i, h)),
        out_shape=jax.ShapeDtypeStruct((nb * t, C_WIDTH), BF16),
        scratch_shapes=_attn_scratch(hb, s_tot, 2 * TQ),
        compiler_params=_params("parallel", "parallel", "arbitrary"),
        name="diff_attn",
    )(*args)


def _out_kernel(*refs, n_in, final):
    y_refs = refs[:n_in]
    w_ref, x_ref, gate_ref = refs[n_in:n_in + 3]
    refs = refs[n_in + 3:]
    acc = None
    k0 = 0
    for y_ref in y_refs:
        kk = y_ref.shape[1]
        a = jnp.dot(y_ref[...], w_ref[k0:k0 + kk, :], preferred_element_type=F32)
        acc = a if acc is None else acc + a
        k0 += kk
    x = x_ref[...] + gate_ref[...] * acc
    xn = x * lax.rsqrt(jnp.mean(x * x, axis=-1, keepdims=True) + EPS)
    if final:
        fg_ref, o_ref = refs
        o_ref[...] = xn * fg_ref[...]
    else:
        g_ref, shift_ref, scale_ref, o_ref, h_ref = refs
        o_ref[...] = x
        h_ref[...] = (xn * g_ref[...] * (1.0 + scale_ref[...]) + shift_ref[...]).astype(h_ref.dtype)


def _out_proj(ys, w, x, mod3, row_of_tile, final_g=None, next_norm=None, tm=512):
    t = x.shape[0]
    final = final_g is not None
    vec = pl.BlockSpec((1, D_MODEL), lambda i: (0, 0))
    mod_row = lambda part: pl.BlockSpec((None, 1, D_MODEL), lambda i: (row_of_tile(i), 0, part))
    in_specs = [pl.BlockSpec((tm, y.shape[1]), lambda i: (i, 0)) for y in ys]
    in_specs += [
        pl.BlockSpec(w.shape, lambda i: (0, 0)),
        pl.BlockSpec((tm, D_MODEL), lambda i: (i, 0)),
        mod_row(2),
    ]
    args = list(ys) + [w, x, mod3]
    row_blk = pl.BlockSpec((tm, D_MODEL), lambda i: (i, 0))
    if final:
        in_specs.append(vec)
        args.append(final_g.reshape(1, D_MODEL))
        out_specs, out_shape = row_blk, jax.ShapeDtypeStruct((t, D_MODEL), F32)
    else:
        next_g, next_mod3 = next_norm
        in_specs += [vec, mod_row(0), mod_row(1)]
        args += [next_g.reshape(1, D_MODEL), next_mod3, next_mod3]
        out_specs = [row_blk, row_blk]
        out_shape = [jax.ShapeDtypeStruct((t, D_MODEL), F32), jax.ShapeDtypeStruct((t, D_MODEL), BF16)]
    return pl.pallas_call(
        functools.partial(_out_kernel, n_in=len(ys), final=final),
        grid=(t // tm,),
        in_specs=in_specs,
        out_specs=out_specs,
        out_shape=out_shape,
        compiler_params=_params("parallel"),
        name="out_proj",
    )(*args)


def kernel(x_prompt, x_sample, cache_k0, cache_v0, cache_k1, cache_v1, c, c_ctx, w_ada0, b_ada0, norm_g0, w_in0, w_s0, b_s0, q_norm_g0, k_norm_g0, w_out0, w_ada1, b_ada1, norm_g1, w_in1, lambda_q1, lambda_k1, lambda_q2, lambda_k2, subln_g1, w_out1, final_g):
    n_ctx, n_smp = x_prompt.shape[0], x_sample.shape[0]

    cond = jnp.concatenate([c_ctx[None], c, jnp.zeros((8 - 1 - n_smp, D_MODEL), F32)], axis=0)
    mod0 = _ada(cond, w_ada0, b_ada0).reshape(8, 1, 3 * D_MODEL)
    mod1 = _ada(cond, w_ada1, b_ada1).reshape(8, 1, 3 * D_MODEL)

    w_in0_b, w_out0_b = w_in0.astype(BF16), w_out0.astype(BF16)
    w_in1_b, w_out1_b = w_in1.astype(BF16), w_out1.astype(BF16)
    ws_b = w_s0.astype(BF16)
    bs_b = jnp.broadcast_to(b_s0[:, :, None], (A_GROUPS, CHUNK, LANES))
    lam_p = jnp.stack([lambda_q1, lambda_k1, lambda_q2, lambda_k2]).astype(F32)
    tab0 = _rope_tables(B_HEAD_DIM // 4)
    tab1 = _rope_tables(C_QK_DIM // 4)

    def run(x, nb, t, smp):
        def row_fn(tm):
            if not smp:
                return lambda i: 0
            return lambda i: 1 + (i * tm) // t

        h = _norm_mod(x, norm_g0, mod0, row_fn(512), 512)
        z = _mm(h, w_in0_b, BF16 if smp else F32, IN0_WIDTH // 4)
        y_a = _sgate(z, ws_b, bs_b)
        if smp:
            qp, kp, vp = _prep0(z, q_norm_g0, k_norm_g0, tab0, False)
            cache = (cache_k0.reshape(nb * PAST_LEN, B_KV_WIDTH), cache_v0.reshape(nb * PAST_LEN, B_KV_WIDTH))
            k0 = v0 = None
        else:
            qp, kp, vp, k0, v0 = _prep0(z, q_norm_g0, k_norm_g0, None, True)
            cache = None
        y_b = _gqa(qp, kp, vp, cache, z, nb, t, 512 if smp else SEQ)
        x1, h = _out_proj([y_a, y_b], w_out0_b, x, mod0, row_fn(512), next_norm=(norm_g1, mod1))

        if smp:
            qk = _mm_rope(h, w_in1_b, tab1, (C_QK_DIM ** -0.5) * LOG2E)
            vg = _mm(h, w_in1_b, BF16, 1024, (2 * C_WIDTH // 1024, 1, 2 * C_WIDTH // 1024))
            cache = (cache_k1.reshape(nb * PAST_LEN, C_WIDTH), cache_v1.reshape(nb * PAST_LEN, C_WIDTH))
            hb = 1
            second = C_WIDTH // (hb * LANES)
            y_c = _diff((qk, 0), (qk, second), (vg, 0), (vg, second), cache, lam_p, subln_g1,
                        nb, t, 1024, hb, None)
            k1 = v1 = None
        else:
            qg = _mm(h, w_in1_b, BF16, C_WIDTH, (0, 3, 2), tm=512)
            kb, k1 = _mm_kt(h, w_in1_b, 1, nb)
            v1 = _mm(h, w_in1_b, F32, C_WIDTH, (2, 1, 1), tm=512)
            hb = 8
            y_c = _diff((qg, 0), (kb, 0), (v1, 0), (qg, C_WIDTH // (hb * LANES)), None, lam_p, subln_g1,
                        nb, t, SEQ, hb, (C_QK_DIM ** -0.5) * LOG2E)
        y = _out_proj([y_c], w_out1_b, x1, mod1, row_fn(512), final_g)
        return y, k0, v0, k1, v1

    y_p, k0, v0, k1, v1 = run(x_prompt.reshape(n_ctx * SEQ, D_MODEL), n_ctx, SEQ, False)
    y_s, _, _, _, _ = run(x_sample.reshape(n_smp * DEC_SEQ, D_MODEL), n_smp, DEC_SEQ, True)

    return (
        y_p.reshape(n_ctx, SEQ, D_MODEL),
        y_s.reshape(n_smp, DEC_SEQ, D_MODEL),
        k0.reshape(n_ctx, SEQ, B_KV_HEADS, B_HEAD_DIM),
        v0.reshape(n_ctx, SEQ, B_KV_HEADS, B_HEAD_DIM),
        k1.reshape(n_ctx, C_HEADS, 2, C_QK_DIM, SEQ).transpose(0, 4, 1, 2, 3),
        v1.reshape(n_ctx, SEQ, C_HEADS, C_V_DIM),
    )
```

```python
import functools
import math

import jax
import jax.numpy as jnp
from jax import lax
from jax.experimental import pallas as pl
from jax.experimental.pallas import tpu as pltpu

F32 = jnp.float32
BF16 = jnp.bfloat16

D_MODEL = 2048
SEQ = 256
DEC_SEQ = 4096
PAST_LEN = 256
GRID_W = 64
CHUNK = 128
ROPE_THETA = 10000.0
EPS = 1e-6

A_GROUPS = 8
A_WIDTH = 1024
B_HEADS = 8
B_KV_HEADS = 2
B_GROUP = B_HEADS // B_KV_HEADS
B_HEAD_DIM = 128
B_WIDTH = 1024
B_KV_WIDTH = 256
IN0_WIDTH = 3 * A_WIDTH + 2 * B_WIDTH + 2 * B_KV_WIDTH
C_HEADS = 16
C_QK_DIM = 64
C_V_DIM = 128
C_WIDTH = 2048
IN1_WIDTH = 4 * C_WIDTH
LAMBDA_INIT_1 = 0.8 - 0.6 * math.exp(-0.3 * 1)

LANES = 128
LOG2E = math.log2(math.e)
VMEM_LIMIT = 56 * 1024 * 1024

Q0_OFF = 3 * A_WIDTH
K0_OFF = Q0_OFF + B_WIDTH
V0_OFF = K0_OFF + B_KV_WIDTH
G0_OFF = V0_OFF + B_KV_WIDTH


def _params(*sem):
    return pltpu.CompilerParams(dimension_semantics=sem, vmem_limit_bytes=VMEM_LIMIT)


def _silu(x):
    return x * jax.nn.sigmoid(x)


def _ada_kernel(c_ref, w_ref, b_ref, o_ref):
    s = _silu(c_ref[...]).astype(BF16)
    o_ref[...] = jnp.dot(s, w_ref[...].astype(BF16), preferred_element_type=F32) + b_ref[...]


def _ada(cond, w_ada, b_ada):
    bn = 512
    n = w_ada.shape[1]
    return pl.pallas_call(
        _ada_kernel,
        grid=(n // bn,),
        in_specs=[
            pl.BlockSpec((8, D_MODEL), lambda j: (0, 0)),
            pl.BlockSpec((D_MODEL, bn), lambda j: (0, j)),
            pl.BlockSpec((1, bn), lambda j: (0, j)),
        ],
        out_specs=pl.BlockSpec((8, bn), lambda j: (0, j)),
        out_shape=jax.ShapeDtypeStruct((8, n), F32),
        compiler_params=_params("parallel"),
        name="ada",
    )(cond, w_ada, b_ada.reshape(1, n))


def _norm_mod_kernel(x_ref, g_ref, shift_ref, scale_ref, h_ref):
    x = x_ref[...]
    ms = jnp.mean(x * x, axis=-1, keepdims=True)
    xn = x * lax.rsqrt(ms + EPS) * g_ref[...]
    h_ref[...] = (xn * (1.0 + scale_ref[...]) + shift_ref[...]).astype(h_ref.dtype)


def _norm_mod(x, norm_g, mod3, row_of_tile, tm):
    t = x.shape[0]
    return pl.pallas_call(
        _norm_mod_kernel,
        grid=(t // tm,),
        in_specs=[
            pl.BlockSpec((tm, D_MODEL), lambda i: (i, 0)),
            pl.BlockSpec((1, D_MODEL), lambda i: (0, 0)),
            pl.BlockSpec((None, 1, D_MODEL), lambda i: (row_of_tile(i), 0, 0)),
            pl.BlockSpec((None, 1, D_MODEL), lambda i: (row_of_tile(i), 0, 1)),
        ],
        out_specs=pl.BlockSpec((tm, D_MODEL), lambda i: (i, 0)),
        out_shape=jax.ShapeDtypeStruct((t, D_MODEL), BF16),
        compiler_params=_params("parallel"),
        name="norm_mod",
    )(x, norm_g.reshape(1, D_MODEL), mod3, mod3)


def _mm_kernel(h_ref, w_ref, o_ref):
    o_ref[...] = jnp.dot(h_ref[...], w_ref[...], preferred_element_type=F32).astype(o_ref.dtype)


def _mm(h, w, out_dtype, tn, col_blocks=None, tm=1024):
    t, k = h.shape
    first, stride, count = col_blocks if col_blocks is not None else (0, 1, w.shape[1] // tn)
    assert t % tm == 0 and w.shape[1] % tn == 0
    return pl.pallas_call(
        _mm_kernel,
        grid=(t // tm, count),
        in_specs=[
            pl.BlockSpec((tm, k), lambda i, j: (i, 0)),
            pl.BlockSpec((k, tn), lambda i, j: (0, first + stride * j)),
        ],
        out_specs=pl.BlockSpec((tm, tn), lambda i, j: (i, j)),
        out_shape=jax.ShapeDtypeStruct((t, count * tn), out_dtype),
        compiler_params=_params("parallel", "parallel"),
        name="in_proj",
    )(h, w)


def _mm_kt_kernel(h_ref, w_ref, kb_ref, kt_ref):
    acc = jnp.dot(h_ref[...], w_ref[...], preferred_element_type=F32)
    kb_ref[...] = acc.astype(kb_ref.dtype)
    for b in range(kt_ref.shape[0]):
        kt_ref[b] = acc[b * SEQ:(b + 1) * SEQ, :].T


def _mm_kt(h, w, col_block, n_batch, tb=2):
    t, k = h.shape
    tm = tb * SEQ
    return pl.pallas_call(
        _mm_kt_kernel,
        grid=(t // tm,),
        in_specs=[
            pl.BlockSpec((tm, k), lambda i: (i, 0)),
            pl.BlockSpec((k, C_WIDTH), lambda i: (0, col_block)),
        ],
        out_specs=[
            pl.BlockSpec((tm, C_WIDTH), lambda i: (i, 0)),
            pl.BlockSpec((tb, C_WIDTH, SEQ), lambda i: (i, 0, 0)),
        ],
        out_shape=[
            jax.ShapeDtypeStruct((t, C_WIDTH), BF16),
            jax.ShapeDtypeStruct((n_batch, C_WIDTH, SEQ), F32),
        ],
        compiler_params=_params("parallel"),
        name="in_proj_kt",
    )(h, w)


def _mm_sgate_kernel(h_ref, w_ref, ws_ref, bs_ref, o_ref):
    z = jnp.dot(h_ref[...], w_ref[...], preferred_element_type=F32)
    v = z[:, A_WIDTH:2 * A_WIDTH]
    mu = jnp.mean(v, axis=-1, keepdims=True)
    vc = v - mu
    var = jnp.mean(vc * vc, axis=-1, keepdims=True)
    vn = (vc * lax.rsqrt(var + EPS)).astype(BF16)
    for ch in range(z.shape[0] // CHUNK):
        rows = slice(ch * CHUNK, (ch + 1) * CHUNK)
        for g in range(A_GROUPS):
            cols = slice(g * LANES, (g + 1) * LANES)
            s = jnp.dot(ws_ref[g], vn[rows, cols], preferred_element_type=F32) + bs_ref[g]
            u = z[rows, g * LANES:(g + 1) * LANES]
            gate = z[rows, 2 * A_WIDTH + g * LANES:2 * A_WIDTH + (g + 1) * LANES]
            o_ref[rows, cols] = (u * s * _silu(gate)).astype(o_ref.dtype)


def _mm_sgate(h, w, ws_b, bs_b, tm=512):
    t, k = h.shape
    full = pl.BlockSpec((A_GROUPS, CHUNK, CHUNK), lambda i: (0, 0, 0))
    return pl.pallas_call(
        _mm_sgate_kernel,
        grid=(t // tm,),
        in_specs=[
            pl.BlockSpec((tm, k), lambda i: (i, 0)),
            pl.BlockSpec((k, 3 * A_WIDTH), lambda i: (0, 0)),
            full, full,
        ],
        out_specs=pl.BlockSpec((tm, A_WIDTH), lambda i: (i, 0)),
        out_shape=jax.ShapeDtypeStruct((t, A_WIDTH), BF16),
        compiler_params=_params("parallel"),
        name="in_proj_sgate",
    )(h, w, ws_b, bs_b)


def _rope_tables(half):
    rows = DEC_SEQ // GRID_W
    row_pos = jnp.repeat(jnp.arange(rows, dtype=F32), GRID_W)
    col_pos = jnp.tile(jnp.arange(GRID_W, dtype=F32), rows)
    freqs = ROPE_THETA ** (-jnp.arange(half, dtype=F32) / half)
    ang_r = row_pos[:, None] * freqs[None, :]
    ang_c = col_pos[:, None] * freqs[None, :]
    cr, sr, cc, sc = jnp.cos(ang_r), jnp.sin(ang_r), jnp.cos(ang_c), jnp.sin(ang_c)
    z = jnp.zeros_like(sr)
    reps = LANES // (4 * half)
    cos_t = jnp.tile(jnp.concatenate([cr, cr, cc, cc], axis=-1), (1, reps))
    sin_a = jnp.tile(jnp.concatenate([z, sr, z, sc], axis=-1), (1, reps))
    sin_b = jnp.tile(jnp.concatenate([-sr, z, -sc, z], axis=-1), (1, reps))
    return cos_t, sin_a, sin_b


def _rope_partner_matrix(half):
    lane = jnp.arange(LANES)
    first = (lane % (2 * half)) < half
    src = jnp.where(first, lane + half, lane - half)
    sign = jnp.where(first, -1.0, 1.0)
    return jnp.zeros((LANES, LANES), F32).at[src, lane].set(sign).astype(BF16)


def _rope(x, cos_t, sin_a, sin_b, half):
    return x * cos_t + pltpu.roll(x, half, 1) * sin_a + pltpu.roll(x, LANES - half, 1) * sin_b


def _mm_qkv0_kernel(*refs, rope, emit_f32):
    h_ref, w_ref, qg_ref, kg_ref = refs[:4]
    refs = refs[4:]
    if rope:
        cos_ref, sin_ref, perm_ref = refs[:3]
        refs = refs[3:]
    qo_ref, ko_ref, vo_ref = refs[:3]
    if emit_f32:
        kf_ref, vf_ref = refs[3:5]
    z = jnp.dot(h_ref[...], w_ref[...], preferred_element_type=F32)

    def norm(x, g):
        ms = jnp.mean(x * x, axis=-1, keepdims=True)
        return x * lax.rsqrt(ms + EPS) * g

    def rot(x):
        if not rope:
            return x
        partner = jnp.dot(x.astype(BF16), perm_ref[...], preferred_element_type=F32)
        return x * cos_ref[...] + partner * sin_ref[...]

    qscale = (B_HEAD_DIM ** -0.5) * LOG2E
    for h in range(B_HEADS):
        cols = slice(h * LANES, (h + 1) * LANES)
        qn = norm(z[:, cols], qg_ref[...])
        qo_ref[:, cols] = (rot(qn) * qscale).astype(qo_ref.dtype)
    for h in range(B_KV_HEADS):
        cols = slice(h * LANES, (h + 1) * LANES)
        kn = norm(z[:, B_WIDTH + h * LANES:B_WIDTH + (h + 1) * LANES], kg_ref[...])
        ko_ref[:, cols] = rot(kn).astype(ko_ref.dtype)
        if emit_f32:
            kf_ref[:, cols] = kn
    v = z[:, B_WIDTH + B_KV_WIDTH:]
    vo_ref[...] = v.astype(vo_ref.dtype)
    if emit_f32:
        vf_ref[...] = v


def _mm_qkv0(h, w, q_g, k_g, tables, emit_f32, tm=1024):
    t, kdim = h.shape
    rope = tables is not None
    qkv_w = B_WIDTH + 2 * B_KV_WIDTH
    in_specs = [
        pl.BlockSpec((tm, kdim), lambda i: (i, 0)),
        pl.BlockSpec((kdim, qkv_w), lambda i: (0, Q0_OFF // qkv_w)),
        pl.BlockSpec((1, LANES), lambda i: (0, 0)),
        pl.BlockSpec((1, LANES), lambda i: (0, 0)),
    ]
    args = [h, w, q_g.reshape(1, LANES), k_g.reshape(1, LANES)]
    if rope:
        nt = DEC_SEQ // tm
        in_specs += [pl.BlockSpec((tm, LANES), lambda i: (i % nt, 0))] * 2
        in_specs.append(pl.BlockSpec((LANES, LANES), lambda i: (0, 0)))
        args += list(tables)
    out_specs = [
        pl.BlockSpec((tm, B_WIDTH), lambda i: (i, 0)),
        pl.BlockSpec((tm, B_KV_WIDTH), lambda i: (i, 0)),
        pl.BlockSpec((tm, B_KV_WIDTH), lambda i: (i, 0)),
    ]
    out_shape = [
        jax.ShapeDtypeStruct((t, B_WIDTH), BF16),
        jax.ShapeDtypeStruct((t, B_KV_WIDTH), BF16),
        jax.ShapeDtypeStruct((t, B_KV_WIDTH), BF16),
    ]
    if emit_f32:
        out_specs += [pl.BlockSpec((tm, B_KV_WIDTH), lambda i: (i, 0))] * 2
        out_shape += [jax.ShapeDtypeStruct((t, B_KV_WIDTH), F32)] * 2
    assert Q0_OFF % qkv_w == 0
    return pl.pallas_call(
        functools.partial(_mm_qkv0_kernel, rope=rope, emit_f32=emit_f32),
        grid=(t // tm,),
        in_specs=in_specs,
        out_specs=out_specs,
        out_shape=out_shape,
        compiler_params=_params("parallel"),
        name="in_proj_qkv",
    )(*args)


def _mm_rope_kernel(h_ref, w_ref, cos_ref, sa_ref, sb_ref, o_ref, *, q_tiles, q_scale):
    acc = jnp.dot(h_ref[...], w_ref[...], preferred_element_type=F32)
    scale = jnp.where(pl.program_id(1) < q_tiles, q_scale, 1.0)
    cos_t, sa, sb = cos_ref[...], sa_ref[...], sb_ref[...]
    for c in range(acc.shape[1] // LANES):
        cols = slice(c * LANES, (c + 1) * LANES)
        o_ref[:, cols] = (_rope(acc[:, cols], cos_t, sa, sb, C_QK_DIM // 4) * scale).astype(o_ref.dtype)


def _mm_rope(h, w, tables, q_scale, tm=1024, tn=1024):
    t, k = h.shape
    nt = DEC_SEQ // tm
    tab = pl.BlockSpec((tm, LANES), lambda i, j: (i % nt, 0))
    return pl.pallas_call(
        functools.partial(_mm_rope_kernel, q_tiles=C_WIDTH // tn, q_scale=q_scale),
        grid=(t // tm, 2 * C_WIDTH // tn),
        in_specs=[
            pl.BlockSpec((tm, k), lambda i, j: (i, 0)),
            pl.BlockSpec((k, tn), lambda i, j: (0, j)),
            tab, tab, tab,
        ],
        out_specs=pl.BlockSpec((tm, tn), lambda i, j: (i, j)),
        out_shape=jax.ShapeDtypeStruct((t, 2 * C_WIDTH), BF16),
        compiler_params=_params("parallel", "parallel"),
        name="in_proj_rope",
    )(h, w, *tables)


KEY_CHUNK = 256
SUBLANES = 8


MAX_KEY_BLOCK = 1024


def _key_blocks(s_new, has_cache):
    size = min(s_new, MAX_KEY_BLOCK)
    blocks = [(r, size) for r in range(0, s_new, size)]
    if has_cache:
        blocks.append((s_new, PAST_LEN))
    return blocks


def _attn_pipeline(units, blocks, s_scr):
    n = s_scr.shape[2]
    st = [dict() for _ in units]

    def stage1(u, b):
        r0, size = blocks[b]
        if b == 0:
            st[u]["qq"] = units[u][0]()
        s = lax.dot_general(units[u][1](b), st[u]["qq"], (((1,), (1,)), ((), ())), preferred_element_type=F32)
        s_scr[u % 2, r0:r0 + size, :] = s
        m8 = s.reshape(size // SUBLANES, SUBLANES, n).max(axis=0)
        st[u]["m8"] = m8 if b == 0 else jnp.maximum(st[u]["m8"], m8)
        if b == len(blocks) - 1:
            st[u]["m"] = st[u]["m8"].max(axis=0, keepdims=True)

    def stage2(u, b):
        r0, size = blocks[b]
        for r in range(r0, r0 + size, KEY_CHUNK):
            p = jnp.exp2(s_scr[u % 2, r:r + KEY_CHUNK, :] - st[u]["m"])
            l8 = p.reshape(KEY_CHUNK // SUBLANES, SUBLANES, n).sum(axis=0)
            a = jnp.dot(units[u][2][:, r:r + KEY_CHUNK], p.astype(BF16), preferred_element_type=F32)
            st[u]["l8"] = l8 if r == 0 else st[u]["l8"] + l8
            st[u]["acc"] = a if r == 0 else st[u]["acc"] + a
        if b == len(blocks) - 1:
            l = st[u]["l8"].sum(axis=0, keepdims=True)
            units[u][3](st[u]["acc"] * (1.0 / l))

    for slot in range(len(units) + 1):
        for b in range(len(blocks)):
            if slot < len(units):
                stage1(slot, b)
            if slot >= 1:
                stage2(slot - 1, b)


def _fill_vt(vt_scr, v_ref, vc_ref, n_kv):
    s_new = v_ref.shape[0]
    for j in range(n_kv):
        cols = slice(j * LANES, (j + 1) * LANES)
        vt_scr[j, :, 0:s_new] = v_ref[:, cols].astype(F32).T.astype(BF16)
        if vc_ref is not None:
            vt_scr[j, :, s_new:] = vc_ref[:, cols].astype(F32).T.astype(BF16)


def _attn_scratch(n_kv, s_tot, n):
    return [
        pltpu.VMEM((n_kv, LANES, s_tot), BF16),
        pltpu.VMEM((2, s_tot, n), F32),
    ]


TQ = 256


def _gqa_kernel(*refs, has_cache):
    q_ref, k_ref, v_ref = refs[:3]
    refs = refs[3:]
    kc_ref = vc_ref = None
    if has_cache:
        (kc_ref, vc_ref), refs = refs[:2], refs[2:]
    g_ref, o_ref, vt_scr, s_scr = refs
    blocks = _key_blocks(k_ref.shape[0], has_cache)

    @pl.when(pl.program_id(2) == 0)
    def _():
        _fill_vt(vt_scr, v_ref, vc_ref, 1)

    def key_block(b):
        r0, size = blocks[b]
        if r0 < k_ref.shape[0]:
            return k_ref[r0:r0 + size, :].astype(BF16)
        return kc_ref[...].astype(BF16)

    def make_unit(rows, heads):
        def make_qq():
            return jnp.concatenate([q_ref[rows, h * LANES:(h + 1) * LANES] for h in heads], axis=0)

        def finish(ot):
            for i, h in enumerate(heads):
                cols = slice(h * LANES, (h + 1) * LANES)
                gate = g_ref[rows, cols].astype(F32)
                o_ref[rows, cols] = (ot[:, i * TQ:(i + 1) * TQ].T * _silu(gate)).astype(o_ref.dtype)

        return make_qq, key_block, vt_scr.at[0], finish

    units = [make_unit(slice(r, r + TQ), (h, h + 1))
             for r in range(0, q_ref.shape[0], TQ) for h in range(0, B_GROUP, 2)]
    _attn_pipeline(units, blocks, s_scr)


def _gqa(qp, kp, vp, cache, gate, nb, t, tqs):
    nq = t // tqs
    qw = B_GROUP * LANES
    s_tot = t + (PAST_LEN if cache is not None else 0)
    in_specs = [
        pl.BlockSpec((tqs, qw), lambda b, h, i: (b * nq + i, h)),
        pl.BlockSpec((t, LANES), lambda b, h, i: (b, h)),
        pl.BlockSpec((t, LANES), lambda b, h, i: (b, h)),
    ]
    args = [qp, kp, vp]
    if cache is not None:
        in_specs += [pl.BlockSpec((PAST_LEN, LANES), lambda b, h, i: (b, h))] * 2
        args += list(cache)
    in_specs.append(pl.BlockSpec((tqs, qw), lambda b, h, i: (b * nq + i, h)))
    args.append(gate)
    return pl.pallas_call(
        functools.partial(_gqa_kernel, has_cache=cache is not None),
        grid=(nb, B_KV_HEADS, nq),
        in_specs=in_specs,
        out_specs=pl.BlockSpec((tqs, qw), lambda b, h, i: (b * nq + i, h)),
        out_shape=jax.ShapeDtypeStruct((nb * t, B_WIDTH), BF16),
        scratch_shapes=_attn_scratch(1, s_tot, 2 * TQ),
        compiler_params=_params("parallel", "parallel", "arbitrary"),
        name="gqa_attn",
    )(*args)


def _diff_kernel(*refs, has_cache, hb, q_scale):
    q_ref, k_ref, v_ref = refs[:3]
    refs = refs[3:]
    kc_ref = vc_ref = None
    if has_cache:
        (kc_ref, vc_ref), refs = refs[:2], refs[2:]
    g_ref, lam_ref, sg_ref, o_ref, vt_scr, s_scr = refs
    blocks = _key_blocks(k_ref.shape[0], has_cache)

    @pl.when(pl.program_id(2) == 0)
    def _():
        _fill_vt(vt_scr, v_ref, vc_ref, hb)

    lp = lam_ref[...]
    lam = (jnp.exp(jnp.sum(lp[0:1] * lp[1:2], axis=-1, keepdims=True))
           - jnp.exp(jnp.sum(lp[2:3] * lp[3:4], axis=-1, keepdims=True)) + LAMBDA_INIT_1)
    lane = lax.broadcasted_iota(jnp.int32, (TQ, LANES), 1)

    def make_unit(rows, j):
        cols = slice(j * LANES, (j + 1) * LANES)

        def make_qq():
            q = q_ref[rows, cols]
            if q_scale is not None:
                q = q.astype(F32) * q_scale
            zero = jnp.zeros_like(q)
            maps = [jnp.where(lane < C_QK_DIM, q, zero), jnp.where(lane >= C_QK_DIM, q, zero)]
            return jnp.concatenate(maps, axis=0).astype(BF16)

        def key_block(b):
            r0, size = blocks[b]
            if r0 < k_ref.shape[0]:
                return k_ref[r0:r0 + size, cols].astype(BF16)
            return kc_ref[:, cols].astype(BF16)

        def finish(ot):
            ot = ot[:, :TQ] - lam * ot[:, TQ:]
            ms = jnp.mean(ot * ot, axis=0, keepdims=True)
            o = (ot * lax.rsqrt(ms + EPS)).T * sg_ref[...] * (1.0 - LAMBDA_INIT_1)
            gate = g_ref[rows, cols].astype(F32)
            o_ref[rows, cols] = (o * _silu(gate)).astype(o_ref.dtype)

        return make_qq, key_block, vt_scr.at[j], finish

    units = [make_unit(slice(r, r + TQ), j) for j in range(hb) for r in range(0, q_ref.shape[0], TQ)]
    _attn_pipeline(units, blocks, s_scr)


def _diff(q, k, v, g, cache, lam_p, sub_g, nb, t, tq, hb, q_scale):
    nq = t // tq
    w = hb * LANES
    s_tot = t + (PAST_LEN if cache is not None else 0)
    in_specs = [
        pl.BlockSpec((tq, w), lambda b, h, i: (b * nq + i, q[1] + h)),
        pl.BlockSpec((t, w), lambda b, h, i: (b, k[1] + h)),
        pl.BlockSpec((t, w), lambda b, h, i: (b, v[1] + h)),
    ]
    args = [q[0], k[0], v[0]]
    if cache is not None:
        in_specs += [pl.BlockSpec((PAST_LEN, w), lambda b, h, i: (b, h))] * 2
        args += list(cache)
    in_specs += [
        pl.BlockSpec((tq, w), lambda b, h, i: (b * nq + i, g[1] + h)),
        pl.BlockSpec((4, C_QK_DIM), lambda b, h, i: (0, 0)),
        pl.BlockSpec((1, LANES), lambda b, h, i: (0, 0)),
    ]
    args += [g[0], lam_p, sub_g.reshape(1, LANES)]
    return pl.pallas_call(
        functools.partial(_diff_kernel, has_cache=cache is not None, hb=hb, q_scale=q_scale),
        grid=(nb, C_HEADS // hb, nq),
        in_specs=in_specs,
        out_specs=pl.BlockSpec((tq, w), lambda b, h, i: (b * nq + i, h)),
        out_shape=jax.ShapeDtypeStruct((nb * t, C_WIDTH), BF16),
        scratch_shapes=_attn_scratch(hb, s_tot, 2 * TQ),
        compiler_params=_params("parallel", "parallel", "arbitrary"),
        name="diff_attn",
    )(*args)


def _out_kernel(*refs, n_in, final):
    y_refs = refs[:n_in]
    w_ref, x_ref, gate_ref = refs[n_in:n_in + 3]
    refs = refs[n_in + 3:]
    acc = None
    k0 = 0
    for y_ref in y_refs:
        kk = y_ref.shape[1]
        a = jnp.dot(y_ref[...], w_ref[k0:k0 + kk, :], preferred_element_type=F32)
        acc = a if acc is None else acc + a
        k0 += kk
    x = x_ref[...] + gate_ref[...] * acc
    xn = x * lax.rsqrt(jnp.mean(x * x, axis=-1, keepdims=True) + EPS)
    if final:
        fg_ref, o_ref = refs
        o_ref[...] = xn * fg_ref[...]
    else:
        g_ref, shift_ref, scale_ref, o_ref, h_ref = refs
        o_ref[...] = x
        h_ref[...] = (xn * g_ref[...] * (1.0 + scale_ref[...]) + shift_ref[...]).astype(h_ref.dtype)


def _out_proj(ys, w, x, mod3, row_of_tile, final_g=None, next_norm=None, tm=512):
    t = x.shape[0]
    final = final_g is not None
    vec = pl.BlockSpec((1, D_MODEL), lambda i: (0, 0))
    mod_row = lambda part: pl.BlockSpec((None, 1, D_MODEL), lambda i: (row_of_tile(i), 0, part))
    in_specs = [pl.BlockSpec((tm, y.shape[1]), lambda i: (i, 0)) for y in ys]
    in_specs += [
        pl.BlockSpec(w.shape, lambda i: (0, 0)),
        pl.BlockSpec((tm, D_MODEL), lambda i: (i, 0)),
        mod_row(2),
    ]
    args = list(ys) + [w, x, mod3]
    row_blk = pl.BlockSpec((tm, D_MODEL), lambda i: (i, 0))
    if final:
        in_specs.append(vec)
        args.append(final_g.reshape(1, D_MODEL))
        out_specs, out_shape = row_blk, jax.ShapeDtypeStruct((t, D_MODEL), F32)
    else:
        next_g, next_mod3 = next_norm
        in_specs += [vec, mod_row(0), mod_row(1)]
        args += [next_g.reshape(1, D_MODEL), next_mod3, next_mod3]
        out_specs = [row_blk, row_blk]
        out_shape = [jax.ShapeDtypeStruct((t, D_MODEL), F32), jax.ShapeDtypeStruct((t, D_MODEL), BF16)]
    return pl.pallas_call(
        functools.partial(_out_kernel, n_in=len(ys), final=final),
        grid=(t // tm,),
        in_specs=in_specs,
        out_specs=out_specs,
        out_shape=out_shape,
        compiler_params=_params("parallel"),
        name="out_proj",
    )(*args)


def kernel(x_prompt, x_sample, cache_k0, cache_v0, cache_k1, cache_v1, c, c_ctx, w_ada0, b_ada0, norm_g0, w_in0, w_s0, b_s0, q_norm_g0, k_norm_g0, w_out0, w_ada1, b_ada1, norm_g1, w_in1, lambda_q1, lambda_k1, lambda_q2, lambda_k2, subln_g1, w_out1, final_g):
    n_ctx, n_smp = x_prompt.shape[0], x_sample.shape[0]

    cond = jnp.concatenate([c_ctx[None], c, jnp.zeros((8 - 1 - n_smp, D_MODEL), F32)], axis=0)
    mod0 = _ada(cond, w_ada0, b_ada0).reshape(8, 1, 3 * D_MODEL)
    mod1 = _ada(cond, w_ada1, b_ada1).reshape(8, 1, 3 * D_MODEL)

    w_in0_b, w_out0_b = w_in0.astype(BF16), w_out0.astype(BF16)
    w_in1_b, w_out1_b = w_in1.astype(BF16), w_out1.astype(BF16)
    ws_b = w_s0.astype(BF16)
    bs_b = jnp.broadcast_to(b_s0[:, :, None], (A_GROUPS, CHUNK, LANES))
    lam_p = jnp.stack([lambda_q1, lambda_k1, lambda_q2, lambda_k2]).astype(F32)
    cos0, sin_a0, sin_b0 = _rope_tables(B_HEAD_DIM // 4)
    tab0 = (cos0, sin_a0 - sin_b0, _rope_partner_matrix(B_HEAD_DIM // 4))
    tab1 = _rope_tables(C_QK_DIM // 4)

    def run(x, nb, t, smp):
        def row_fn(tm):
            if not smp:
                return lambda i: 0
            return lambda i: 1 + (i * tm) // t

        h = _norm_mod(x, norm_g0, mod0, row_fn(512), 512)
        y_a = _mm_sgate(h, w_in0_b, ws_b, bs_b)
        if smp:
            qp, kp, vp = _mm_qkv0(h, w_in0_b, q_norm_g0, k_norm_g0, tab0, False)
            cache = (cache_k0.reshape(nb * PAST_LEN, B_KV_WIDTH), cache_v0.reshape(nb * PAST_LEN, B_KV_WIDTH))
            k0 = v0 = None
        else:
            qp, kp, vp, k0, v0 = _mm_qkv0(h, w_in0_b, q_norm_g0, k_norm_g0, None, True)
            cache = None
        gate_b = _mm(h, w_in0_b, BF16, 512, (G0_OFF // 512, 1, B_WIDTH // 512))
        y_b = _gqa(qp, kp, vp, cache, gate_b, nb, t, 512 if smp else SEQ)
        x1, h = _out_proj([y_a, y_b], w_out0_b, x, mod0, row_fn(512), next_norm=(norm_g1, mod1))

        if smp:
            qk = _mm_rope(h, w_in1_b, tab1, (C_QK_DIM ** -0.5) * LOG2E)
            vg = _mm(h, w_in1_b, BF16, 1024, (2 * C_WIDTH // 1024, 1, 2 * C_WIDTH // 1024))
            cache = (cache_k1.reshape(nb * PAST_LEN, C_WIDTH), cache_v1.reshape(nb * PAST_LEN, C_WIDTH))
            hb = 1
            second = C_WIDTH // (hb * LANES)
            y_c = _diff((qk, 0), (qk, second), (vg, 0), (vg, second), cache, lam_p, subln_g1,
                        nb, t, 1024, hb, None)
            k1 = v1 = None
        else:
            qg = _mm(h, w_in1_b, BF16, C_WIDTH, (0, 3, 2))
            kb, k1 = _mm_kt(h, w_in1_b, 1, nb)
            v1 = _mm(h, w_in1_b, F32, C_WIDTH, (2, 1, 1), tm=512)
            hb = 8
            y_c = _diff((qg, 0), (kb, 0), (v1, 0), (qg, C_WIDTH // (hb * LANES)), None, lam_p, subln_g1,
                        nb, t, SEQ, hb, (C_QK_DIM ** -0.5) * LOG2E)
        y = _out_proj([y_c], w_out1_b, x1, mod1, row_fn(512), final_g)
        return y, k0, v0, k1, v1

    y_p, k0, v0, k1, v1 = run(x_prompt.reshape(n_ctx * SEQ, D_MODEL), n_ctx, SEQ, False)
    y_s, _, _, _, _ = run(x_sample.reshape(n_smp * DEC_SEQ, D_MODEL), n_smp, DEC_SEQ, True)

    return (
        y_p.reshape(n_ctx, SEQ, D_MODEL),
        y_s.reshape(n_smp, DEC_SEQ, D_MODEL),
        k0.reshape(n_ctx, SEQ, B_KV_HEADS, B_HEAD_DIM),
        v0.reshape(n_ctx, SEQ, B_KV_HEADS, B_HEAD_DIM),
        k1.reshape(n_ctx, C_HEADS, 2, C_QK_DIM, SEQ).transpose(0, 4, 1, 2, 3),
        v1.reshape(n_ctx, SEQ, C_HEADS, C_V_DIM),
    )
```

```python
import functools
import math

import jax
import jax.numpy as jnp
from jax import lax
from jax.experimental import pallas as pl
from jax.experimental.pallas import tpu as pltpu

F32 = jnp.float32
BF16 = jnp.bfloat16

D_MODEL = 2048
SEQ = 256
DEC_SEQ = 4096
PAST_LEN = 256
GRID_W = 64
CHUNK = 128
ROPE_THETA = 10000.0
EPS = 1e-6

A_GROUPS = 8
A_WIDTH = 1024
B_HEADS = 8
B_KV_HEADS = 2
B_GROUP = B_HEADS // B_KV_HEADS
B_HEAD_DIM = 128
B_WIDTH = 1024
B_KV_WIDTH = 256
IN0_WIDTH = 3 * A_WIDTH + 2 * B_WIDTH + 2 * B_KV_WIDTH
C_HEADS = 16
C_QK_DIM = 64
C_V_DIM = 128
C_WIDTH = 2048
IN1_WIDTH = 4 * C_WIDTH
LAMBDA_INIT_1 = 0.8 - 0.6 * math.exp(-0.3 * 1)

LANES = 128
LOG2E = math.log2(math.e)
VMEM_LIMIT = 56 * 1024 * 1024

Q0_OFF = 3 * A_WIDTH
K0_OFF = Q0_OFF + B_WIDTH
V0_OFF = K0_OFF + B_KV_WIDTH
G0_OFF = V0_OFF + B_KV_WIDTH


def _params(*sem):
    return pltpu.CompilerParams(dimension_semantics=sem, vmem_limit_bytes=VMEM_LIMIT)


def _silu(x):
    return x * jax.nn.sigmoid(x)


def _ada_kernel(c_ref, w_ref, b_ref, o_ref):
    s = _silu(c_ref[...]).astype(BF16)
    o_ref[...] = jnp.dot(s, w_ref[...].astype(BF16), preferred_element_type=F32) + b_ref[...]


def _ada(cond, w_ada, b_ada):
    bn = 512
    n = w_ada.shape[1]
    return pl.pallas_call(
        _ada_kernel,
        grid=(n // bn,),
        in_specs=[
            pl.BlockSpec((8, D_MODEL), lambda j: (0, 0)),
            pl.BlockSpec((D_MODEL, bn), lambda j: (0, j)),
            pl.BlockSpec((1, bn), lambda j: (0, j)),
        ],
        out_specs=pl.BlockSpec((8, bn), lambda j: (0, j)),
        out_shape=jax.ShapeDtypeStruct((8, n), F32),
        compiler_params=_params("parallel"),
        name="ada",
    )(cond, w_ada, b_ada.reshape(1, n))


def _norm_mod_kernel(x_ref, g_ref, shift_ref, scale_ref, h_ref):
    x = x_ref[...]
    ms = jnp.mean(x * x, axis=-1, keepdims=True)
    xn = x * lax.rsqrt(ms + EPS) * g_ref[...]
    h_ref[...] = (xn * (1.0 + scale_ref[...]) + shift_ref[...]).astype(h_ref.dtype)


def _norm_mod(x, norm_g, mod3, row_of_tile, tm):
    t = x.shape[0]
    return pl.pallas_call(
        _norm_mod_kernel,
        grid=(t // tm,),
        in_specs=[
            pl.BlockSpec((tm, D_MODEL), lambda i: (i, 0)),
            pl.BlockSpec((1, D_MODEL), lambda i: (0, 0)),
            pl.BlockSpec((None, 1, D_MODEL), lambda i: (row_of_tile(i), 0, 0)),
            pl.BlockSpec((None, 1, D_MODEL), lambda i: (row_of_tile(i), 0, 1)),
        ],
        out_specs=pl.BlockSpec((tm, D_MODEL), lambda i: (i, 0)),
        out_shape=jax.ShapeDtypeStruct((t, D_MODEL), BF16),
        compiler_params=_params("parallel"),
        name="norm_mod",
    )(x, norm_g.reshape(1, D_MODEL), mod3, mod3)


def _mm_kernel(h_ref, w_ref, o_ref):
    o_ref[...] = jnp.dot(h_ref[...], w_ref[...], preferred_element_type=F32).astype(o_ref.dtype)


def _mm(h, w, out_dtype, tn, col_blocks=None, tm=1024):
    t, k = h.shape
    first, stride, count = col_blocks if col_blocks is not None else (0, 1, w.shape[1] // tn)
    assert t % tm == 0 and w.shape[1] % tn == 0
    return pl.pallas_call(
        _mm_kernel,
        grid=(t // tm, count),
        in_specs=[
            pl.BlockSpec((tm, k), lambda i, j: (i, 0)),
            pl.BlockSpec((k, tn), lambda i, j: (0, first + stride * j)),
        ],
        out_specs=pl.BlockSpec((tm, tn), lambda i, j: (i, j)),
        out_shape=jax.ShapeDtypeStruct((t, count * tn), out_dtype),
        compiler_params=_params("parallel", "parallel"),
        name="in_proj",
    )(h, w)


def _mm_kt_kernel(h_ref, w_ref, kb_ref, kt_ref):
    acc = jnp.dot(h_ref[...], w_ref[...], preferred_element_type=F32)
    kb_ref[...] = acc.astype(kb_ref.dtype)
    for b in range(kt_ref.shape[0]):
        kt_ref[b] = acc[b * SEQ:(b + 1) * SEQ, :].T


def _mm_kt(h, w, col_block, n_batch, tb=2):
    t, k = h.shape
    tm = tb * SEQ
    return pl.pallas_call(
        _mm_kt_kernel,
        grid=(t // tm,),
        in_specs=[
            pl.BlockSpec((tm, k), lambda i: (i, 0)),
            pl.BlockSpec((k, C_WIDTH), lambda i: (0, col_block)),
        ],
        out_specs=[
            pl.BlockSpec((tm, C_WIDTH), lambda i: (i, 0)),
            pl.BlockSpec((tb, C_WIDTH, SEQ), lambda i: (i, 0, 0)),
        ],
        out_shape=[
            jax.ShapeDtypeStruct((t, C_WIDTH), BF16),
            jax.ShapeDtypeStruct((n_batch, C_WIDTH, SEQ), F32),
        ],
        compiler_params=_params("parallel"),
        name="in_proj_kt",
    )(h, w)


def _mm_sgate_kernel(h_ref, w_ref, ws_ref, bs_ref, o_ref):
    z = jnp.dot(h_ref[...], w_ref[...], preferred_element_type=F32)
    v = z[:, A_WIDTH:2 * A_WIDTH]
    mu = jnp.mean(v, axis=-1, keepdims=True)
    vc = v - mu
    var = jnp.mean(vc * vc, axis=-1, keepdims=True)
    vn = (vc * lax.rsqrt(var + EPS)).astype(BF16)
    for ch in range(z.shape[0] // CHUNK):
        rows = slice(ch * CHUNK, (ch + 1) * CHUNK)
        for g in range(A_GROUPS):
            cols = slice(g * LANES, (g + 1) * LANES)
            s = jnp.dot(ws_ref[g], vn[rows, cols], preferred_element_type=F32) + bs_ref[g]
            u = z[rows, g * LANES:(g + 1) * LANES]
            gate = z[rows, 2 * A_WIDTH + g * LANES:2 * A_WIDTH + (g + 1) * LANES]
            o_ref[rows, cols] = (u * s * _silu(gate)).astype(o_ref.dtype)


def _mm_sgate(h, w, ws_b, bs_b, tm=512):
    t, k = h.shape
    full = pl.BlockSpec((A_GROUPS, CHUNK, CHUNK), lambda i: (0, 0, 0))
    return pl.pallas_call(
        _mm_sgate_kernel,
        grid=(t // tm,),
        in_specs=[
            pl.BlockSpec((tm, k), lambda i: (i, 0)),
            pl.BlockSpec((k, 3 * A_WIDTH), lambda i: (0, 0)),
            full, full,
        ],
        out_specs=pl.BlockSpec((tm, A_WIDTH), lambda i: (i, 0)),
        out_shape=jax.ShapeDtypeStruct((t, A_WIDTH), BF16),
        compiler_params=_params("parallel"),
        name="in_proj_sgate",
    )(h, w, ws_b, bs_b)


def _rope_tables(half):
    rows = DEC_SEQ // GRID_W
    row_pos = jnp.repeat(jnp.arange(rows, dtype=F32), GRID_W)
    col_pos = jnp.tile(jnp.arange(GRID_W, dtype=F32), rows)
    freqs = ROPE_THETA ** (-jnp.arange(half, dtype=F32) / half)
    ang_r = row_pos[:, None] * freqs[None, :]
    ang_c = col_pos[:, None] * freqs[None, :]
    cr, sr, cc, sc = jnp.cos(ang_r), jnp.sin(ang_r), jnp.cos(ang_c), jnp.sin(ang_c)
    z = jnp.zeros_like(sr)
    reps = LANES // (4 * half)
    cos_t = jnp.tile(jnp.concatenate([cr, cr, cc, cc], axis=-1), (1, reps))
    sin_a = jnp.tile(jnp.concatenate([z, sr, z, sc], axis=-1), (1, reps))
    sin_b = jnp.tile(jnp.concatenate([-sr, z, -sc, z], axis=-1), (1, reps))
    return cos_t, sin_a, sin_b


def _rope_partner_matrix(half):
    lane = jnp.arange(LANES)
    first = (lane % (2 * half)) < half
    src = jnp.where(first, lane + half, lane - half)
    sign = jnp.where(first, -1.0, 1.0)
    return jnp.zeros((LANES, LANES), F32).at[src, lane].set(sign).astype(BF16)


def _rope(x, cos_t, sin_a, sin_b, half):
    return x * cos_t + pltpu.roll(x, half, 1) * sin_a + pltpu.roll(x, LANES - half, 1) * sin_b


def _mm_qkv0_kernel(*refs, rope, emit_f32):
    h_ref, w_ref, qg_ref, kg_ref = refs[:4]
    refs = refs[4:]
    if rope:
        cos_ref, sin_ref, perm_ref = refs[:3]
        refs = refs[3:]
    qo_ref, ko_ref, vo_ref = refs[:3]
    if emit_f32:
        kf_ref, vf_ref = refs[3:5]
    z = jnp.dot(h_ref[...], w_ref[...], preferred_element_type=F32)

    def norm(x, g):
        ms = jnp.mean(x * x, axis=-1, keepdims=True)
        return x * lax.rsqrt(ms + EPS) * g

    def rot(x):
        if not rope:
            return x
        partner = jnp.dot(x.astype(BF16), perm_ref[...], preferred_element_type=F32)
        return x * cos_ref[...] + partner * sin_ref[...]

    qscale = (B_HEAD_DIM ** -0.5) * LOG2E
    for h in range(B_HEADS):
        cols = slice(h * LANES, (h + 1) * LANES)
        qn = norm(z[:, cols], qg_ref[...])
        qo_ref[:, cols] = (rot(qn) * qscale).astype(qo_ref.dtype)
    for h in range(B_KV_HEADS):
        cols = slice(h * LANES, (h + 1) * LANES)
        kn = norm(z[:, B_WIDTH + h * LANES:B_WIDTH + (h + 1) * LANES], kg_ref[...])
        ko_ref[:, cols] = rot(kn).astype(ko_ref.dtype)
        if emit_f32:
            kf_ref[:, cols] = kn
    v = z[:, B_WIDTH + B_KV_WIDTH:]
    vo_ref[...] = v.astype(vo_ref.dtype)
    if emit_f32:
        vf_ref[...] = v


def _mm_qkv0(h, w, q_g, k_g, tables, emit_f32, tm=1024):
    t, kdim = h.shape
    rope = tables is not None
    qkv_w = B_WIDTH + 2 * B_KV_WIDTH
    in_specs = [
        pl.BlockSpec((tm, kdim), lambda i: (i, 0)),
        pl.BlockSpec((kdim, qkv_w), lambda i: (0, Q0_OFF // qkv_w)),
        pl.BlockSpec((1, LANES), lambda i: (0, 0)),
        pl.BlockSpec((1, LANES), lambda i: (0, 0)),
    ]
    args = [h, w, q_g.reshape(1, LANES), k_g.reshape(1, LANES)]
    if rope:
        nt = DEC_SEQ // tm
        in_specs += [pl.BlockSpec((tm, LANES), lambda i: (i % nt, 0))] * 2
        in_specs.append(pl.BlockSpec((LANES, LANES), lambda i: (0, 0)))
        args += list(tables)
    out_specs = [
        pl.BlockSpec((tm, B_WIDTH), lambda i: (i, 0)),
        pl.BlockSpec((tm, B_KV_WIDTH), lambda i: (i, 0)),
        pl.BlockSpec((tm, B_KV_WIDTH), lambda i: (i, 0)),
    ]
    out_shape = [
        jax.ShapeDtypeStruct((t, B_WIDTH), BF16),
        jax.ShapeDtypeStruct((t, B_KV_WIDTH), BF16),
        jax.ShapeDtypeStruct((t, B_KV_WIDTH), BF16),
    ]
    if emit_f32:
        out_specs += [pl.BlockSpec((tm, B_KV_WIDTH), lambda i: (i, 0))] * 2
        out_shape += [jax.ShapeDtypeStruct((t, B_KV_WIDTH), F32)] * 2
    assert Q0_OFF % qkv_w == 0
    return pl.pallas_call(
        functools.partial(_mm_qkv0_kernel, rope=rope, emit_f32=emit_f32),
        grid=(t // tm,),
        in_specs=in_specs,
        out_specs=out_specs,
        out_shape=out_shape,
        compiler_params=_params("parallel"),
        name="in_proj_qkv",
    )(*args)


ROPE_SUB_COLS = 1024


def _mm_rope_kernel(h_ref, w_ref, cos_ref, sa_ref, sb_ref, o_ref, *, q_tiles, q_scale):
    scale = jnp.where(pl.program_id(1) < q_tiles, q_scale, 1.0)
    cos_t, sa, sb = cos_ref[...], sa_ref[...], sb_ref[...]
    h = h_ref[...]
    for c0 in range(0, w_ref.shape[1], ROPE_SUB_COLS):
        acc = jnp.dot(h, w_ref[:, c0:c0 + ROPE_SUB_COLS], preferred_element_type=F32)
        for c in range(ROPE_SUB_COLS // LANES):
            x = acc[:, c * LANES:(c + 1) * LANES]
            cols = slice(c0 + c * LANES, c0 + (c + 1) * LANES)
            o_ref[:, cols] = (_rope(x, cos_t, sa, sb, C_QK_DIM // 4) * scale).astype(o_ref.dtype)


def _mm_rope(h, w, tables, q_scale, tm=1024, tn=C_WIDTH):
    t, k = h.shape
    nt = DEC_SEQ // tm
    tab = pl.BlockSpec((tm, LANES), lambda i, j: (i % nt, 0))
    return pl.pallas_call(
        functools.partial(_mm_rope_kernel, q_tiles=C_WIDTH // tn, q_scale=q_scale),
        grid=(t // tm, 2 * C_WIDTH // tn),
        in_specs=[
            pl.BlockSpec((tm, k), lambda i, j: (i, 0)),
            pl.BlockSpec((k, tn), lambda i, j: (0, j)),
            tab, tab, tab,
        ],
        out_specs=pl.BlockSpec((tm, tn), lambda i, j: (i, j)),
        out_shape=jax.ShapeDtypeStruct((t, 2 * C_WIDTH), BF16),
        compiler_params=_params("parallel", "parallel"),
        name="in_proj_rope",
    )(h, w, *tables)


KEY_CHUNK = 256
SUBLANES = 8


MAX_KEY_BLOCK = 1024


def _key_blocks(s_new, has_cache):
    size = min(s_new, MAX_KEY_BLOCK)
    blocks = [(r, size) for r in range(0, s_new, size)]
    if has_cache:
        blocks.append((s_new, PAST_LEN))
    return blocks


def _attn_pipeline(units, blocks, s_scr, p_scr):
    n = s_scr.shape[2]
    st = [dict() for _ in units]

    def stage1(u, b):
        r0, size = blocks[b]
        if b == 0:
            st[u]["qq"] = units[u][0]()
        s = lax.dot_general(units[u][1](b), st[u]["qq"], (((1,), (1,)), ((), ())), preferred_element_type=F32)
        s_scr[u % 2, r0:r0 + size, :] = s
        m8 = s.reshape(size // SUBLANES, SUBLANES, n).max(axis=0)
        st[u]["m8"] = m8 if b == 0 else jnp.maximum(st[u]["m8"], m8)
        if b == len(blocks) - 1:
            st[u]["m"] = st[u]["m8"].max(axis=0, keepdims=True)

    def stage2(u, b):
        r0, size = blocks[b]
        for r in range(r0, r0 + size, KEY_CHUNK):
            p = jnp.exp2(s_scr[u % 2, r:r + KEY_CHUNK, :] - st[u]["m"])
            l8 = p.reshape(KEY_CHUNK // SUBLANES, SUBLANES, n).sum(axis=0)
            st[u]["l8"] = l8 if r == 0 else st[u]["l8"] + l8
            p_scr[u % 2, r:r + KEY_CHUNK, :] = p.astype(BF16)

    def stage3(u, b):
        r0, size = blocks[b]
        a = jnp.dot(units[u][2][:, r0:r0 + size], p_scr[u % 2, r0:r0 + size, :], preferred_element_type=F32)
        st[u]["acc"] = a if b == 0 else st[u]["acc"] + a
        if b == len(blocks) - 1:
            l = st[u]["l8"].sum(axis=0, keepdims=True)
            units[u][3](st[u]["acc"] * (1.0 / l))

    for slot in range(len(units) + 2):
        for b in range(len(blocks)):
            if 0 <= slot - 2 < len(units):
                stage3(slot - 2, b)
            if slot < len(units):
                stage1(slot, b)
            if 0 <= slot - 1 < len(units):
                stage2(slot - 1, b)


def _fill_vt(vt_scr, v_ref, vc_ref, n_kv):
    s_new = v_ref.shape[0]
    for j in range(n_kv):
        cols = slice(j * LANES, (j + 1) * LANES)
        vt_scr[j, :, 0:s_new] = v_ref[:, cols].astype(F32).T.astype(BF16)
        if vc_ref is not None:
            vt_scr[j, :, s_new:] = vc_ref[:, cols].astype(F32).T.astype(BF16)


def _attn_scratch(n_kv, s_tot, n):
    return [
        pltpu.VMEM((n_kv, LANES, s_tot), BF16),
        pltpu.VMEM((2, s_tot, n), F32),
        pltpu.VMEM((2, s_tot, n), BF16),
    ]


TQ = 256


def _gqa_kernel(*refs, has_cache):
    q_ref, k_ref, v_ref = refs[:3]
    refs = refs[3:]
    kc_ref = vc_ref = None
    if has_cache:
        (kc_ref, vc_ref), refs = refs[:2], refs[2:]
    g_ref, o_ref, vt_scr, s_scr, p_scr = refs
    blocks = _key_blocks(k_ref.shape[0], has_cache)

    @pl.when(pl.program_id(2) == 0)
    def _():
        _fill_vt(vt_scr, v_ref, vc_ref, 1)

    def key_block(b):
        r0, size = blocks[b]
        if r0 < k_ref.shape[0]:
            return k_ref[r0:r0 + size, :].astype(BF16)
        return kc_ref[...].astype(BF16)

    def make_unit(rows, heads):
        def make_qq():
            return jnp.concatenate([q_ref[rows, h * LANES:(h + 1) * LANES] for h in heads], axis=0)

        def finish(ot):
            for i, h in enumerate(heads):
                cols = slice(h * LANES, (h + 1) * LANES)
                gate = g_ref[rows, cols].astype(F32)
                o_ref[rows, cols] = (ot[:, i * TQ:(i + 1) * TQ].T * _silu(gate)).astype(o_ref.dtype)

        return make_qq, key_block, vt_scr.at[0], finish

    units = [make_unit(slice(r, r + TQ), (h, h + 1))
             for r in range(0, q_ref.shape[0], TQ) for h in range(0, B_GROUP, 2)]
    _attn_pipeline(units, blocks, s_scr, p_scr)


def _gqa(qp, kp, vp, cache, gate, nb, t, tqs):
    nq = t // tqs
    qw = B_GROUP * LANES
    s_tot = t + (PAST_LEN if cache is not None else 0)
    in_specs = [
        pl.BlockSpec((tqs, qw), lambda b, h, i: (b * nq + i, h)),
        pl.BlockSpec((t, LANES), lambda b, h, i: (b, h)),
        pl.BlockSpec((t, LANES), lambda b, h, i: (b, h)),
    ]
    args = [qp, kp, vp]
    if cache is not None:
        in_specs += [pl.BlockSpec((PAST_LEN, LANES), lambda b, h, i: (b, h))] * 2
        args += list(cache)
    in_specs.append(pl.BlockSpec((tqs, qw), lambda b, h, i: (b * nq + i, h)))
    args.append(gate)
    return pl.pallas_call(
        functools.partial(_gqa_kernel, has_cache=cache is not None),
        grid=(nb, B_KV_HEADS, nq),
        in_specs=in_specs,
        out_specs=pl.BlockSpec((tqs, qw), lambda b, h, i: (b * nq + i, h)),
        out_shape=jax.ShapeDtypeStruct((nb * t, B_WIDTH), BF16),
        scratch_shapes=_attn_scratch(1, s_tot, 2 * TQ),
        compiler_params=_params("parallel", "parallel", "arbitrary"),
        name="gqa_attn",
    )(*args)


def _diff_kernel(*refs, has_cache, hb, q_scale):
    q_ref, k_ref, v_ref = refs[:3]
    refs = refs[3:]
    kc_ref = vc_ref = None
    if has_cache:
        (kc_ref, vc_ref), refs = refs[:2], refs[2:]
    g_ref, lam_ref, sg_ref, o_ref, vt_scr, s_scr, p_scr = refs
    blocks = _key_blocks(k_ref.shape[0], has_cache)

    @pl.when(pl.program_id(2) == 0)
    def _():
        _fill_vt(vt_scr, v_ref, vc_ref, hb)

    lp = lam_ref[...]
    lam = (jnp.exp(jnp.sum(lp[0:1] * lp[1:2], axis=-1, keepdims=True))
           - jnp.exp(jnp.sum(lp[2:3] * lp[3:4], axis=-1, keepdims=True)) + LAMBDA_INIT_1)
    lane = lax.broadcasted_iota(jnp.int32, (TQ, LANES), 1)

    def make_unit(rows, j):
        cols = slice(j * LANES, (j + 1) * LANES)

        def make_qq():
            q = q_ref[rows, cols]
            if q_scale is not None:
                q = q.astype(F32) * q_scale
            zero = jnp.zeros_like(q)
            maps = [jnp.where(lane < C_QK_DIM, q, zero), jnp.where(lane >= C_QK_DIM, q, zero)]
            return jnp.concatenate(maps, axis=0).astype(BF16)

        def key_block(b):
            r0, size = blocks[b]
            if r0 < k_ref.shape[0]:
                return k_ref[r0:r0 + size, cols].astype(BF16)
            return kc_ref[:, cols].astype(BF16)

        def finish(ot):
            ot = ot[:, :TQ] - lam * ot[:, TQ:]
            ms = jnp.mean(ot * ot, axis=0, keepdims=True)
            o = (ot * lax.rsqrt(ms + EPS)).T * sg_ref[...] * (1.0 - LAMBDA_INIT_1)
            gate = g_ref[rows, cols].astype(F32)
            o_ref[rows, cols] = (o * _silu(gate)).astype(o_ref.dtype)

        return make_qq, key_block, vt_scr.at[j], finish

    units = [make_unit(slice(r, r + TQ), j) for j in range(hb) for r in range(0, q_ref.shape[0], TQ)]
    _attn_pipeline(units, blocks, s_scr, p_scr)


def _diff(q, k, v, g, cache, lam_p, sub_g, nb, t, tq, hb, q_scale):
    nq = t // tq
    w = hb * LANES
    s_tot = t + (PAST_LEN if cache is not None else 0)
    in_specs = [
        pl.BlockSpec((tq, w), lambda b, h, i: (b * nq + i, q[1] + h)),
        pl.BlockSpec((t, w), lambda b, h, i: (b, k[1] + h)),
        pl.BlockSpec((t, w), lambda b, h, i: (b, v[1] + h)),
    ]
    args = [q[0], k[0], v[0]]
    if cache is not None:
        in_specs += [pl.BlockSpec((PAST_LEN, w), lambda b, h, i: (b, h))] * 2
        args += list(cache)
    in_specs += [
        pl.BlockSpec((tq, w), lambda b, h, i: (b * nq + i, g[1] + h)),
        pl.BlockSpec((4, C_QK_DIM), lambda b, h, i: (0, 0)),
        pl.BlockSpec((1, LANES), lambda b, h, i: (0, 0)),
    ]
    args += [g[0], lam_p, sub_g.reshape(1, LANES)]
    return pl.pallas_call(
        functools.partial(_diff_kernel, has_cache=cache is not None, hb=hb, q_scale=q_scale),
        grid=(nb, C_HEADS // hb, nq),
        in_specs=in_specs,
        out_specs=pl.BlockSpec((tq, w), lambda b, h, i: (b * nq + i, h)),
        out_shape=jax.ShapeDtypeStruct((nb * t, C_WIDTH), BF16),
        scratch_shapes=_attn_scratch(hb, s_tot, 2 * TQ),
        compiler_params=_params("parallel", "parallel", "arbitrary"),
        name="diff_attn",
    )(*args)


def _out_kernel(*refs, n_in, final):
    y_refs = refs[:n_in]
    w_ref, x_ref, gate_ref = refs[n_in:n_in + 3]
    refs = refs[n_in + 3:]
    half = x_ref.shape[0] // 2
    for rows in (slice(0, half), slice(half, 2 * half)):
        acc = None
        k0 = 0
        for y_ref in y_refs:
            kk = y_ref.shape[1]
            a = jnp.dot(y_ref[rows, :], w_ref[k0:k0 + kk, :], preferred_element_type=F32)
            acc = a if acc is None else acc + a
            k0 += kk
        x = x_ref[rows, :] + gate_ref[...] * acc
        xn = x * lax.rsqrt(jnp.mean(x * x, axis=-1, keepdims=True) + EPS)
        if final:
            fg_ref, o_ref = refs
            o_ref[rows, :] = xn * fg_ref[...]
        else:
            g_ref, shift_ref, scale_ref, o_ref, h_ref = refs
            o_ref[rows, :] = x
            h_ref[rows, :] = (xn * g_ref[...] * (1.0 + scale_ref[...]) + shift_ref[...]).astype(h_ref.dtype)


def _out_proj(ys, w, x, mod3, row_of_tile, final_g=None, next_norm=None, tm=512):
    t = x.shape[0]
    final = final_g is not None
    vec = pl.BlockSpec((1, D_MODEL), lambda i: (0, 0))
    mod_row = lambda part: pl.BlockSpec((None, 1, D_MODEL), lambda i: (row_of_tile(i), 0, part))
    in_specs = [pl.BlockSpec((tm, y.shape[1]), lambda i: (i, 0)) for y in ys]
    in_specs += [
        pl.BlockSpec(w.shape, lambda i: (0, 0)),
        pl.BlockSpec((tm, D_MODEL), lambda i: (i, 0)),
        mod_row(2),
    ]
    args = list(ys) + [w, x, mod3]
    row_blk = pl.BlockSpec((tm, D_MODEL), lambda i: (i, 0))
    if final:
        in_specs.append(vec)
        args.append(final_g.reshape(1, D_MODEL))
        out_specs, out_shape = row_blk, jax.ShapeDtypeStruct((t, D_MODEL), F32)
    else:
        next_g, next_mod3 = next_norm
        in_specs += [vec, mod_row(0), mod_row(1)]
        args += [next_g.reshape(1, D_MODEL), next_mod3, next_mod3]
        out_specs = [row_blk, row_blk]
        out_shape = [jax.ShapeDtypeStruct((t, D_MODEL), F32), jax.ShapeDtypeStruct((t, D_MODEL), BF16)]
    return pl.pallas_call(
        functools.partial(_out_kernel, n_in=len(ys), final=final),
        grid=(t // tm,),
        in_specs=in_specs,
        out_specs=out_specs,
        out_shape=out_shape,
        compiler_params=_params("parallel"),
        name="out_proj",
    )(*args)


def kernel(x_prompt, x_sample, cache_k0, cache_v0, cache_k1, cache_v1, c, c_ctx, w_ada0, b_ada0, norm_g0, w_in0, w_s0, b_s0, q_norm_g0, k_norm_g0, w_out0, w_ada1, b_ada1, norm_g1, w_in1, lambda_q1, lambda_k1, lambda_q2, lambda_k2, subln_g1, w_out1, final_g):
    n_ctx, n_smp = x_prompt.shape[0], x_sample.shape[0]

    cond = jnp.concatenate([c_ctx[None], c, jnp.zeros((8 - 1 - n_smp, D_MODEL), F32)], axis=0)
    mod0 = _ada(cond, w_ada0, b_ada0).reshape(8, 1, 3 * D_MODEL)
    mod1 = _ada(cond, w_ada1, b_ada1).reshape(8, 1, 3 * D_MODEL)

    w_in0_b, w_out0_b = w_in0.astype(BF16), w_out0.astype(BF16)
    w_in1_b, w_out1_b = w_in1.astype(BF16), w_out1.astype(BF16)
    ws_b = w_s0.astype(BF16)
    bs_b = jnp.broadcast_to(b_s0[:, :, None], (A_GROUPS, CHUNK, LANES))
    lam_p = jnp.stack([lambda_q1, lambda_k1, lambda_q2, lambda_k2]).astype(F32)
    cos0, sin_a0, sin_b0 = _rope_tables(B_HEAD_DIM // 4)
    tab0 = (cos0, sin_a0 - sin_b0, _rope_partner_matrix(B_HEAD_DIM // 4))
    tab1 = _rope_tables(C_QK_DIM // 4)

    def run(x, nb, t, smp):
        def row_fn(tm):
            if not smp:
                return lambda i: 0
            return lambda i: 1 + (i * tm) // t

        h = _norm_mod(x, norm_g0, mod0, row_fn(512), 512)
        y_a = _mm_sgate(h, w_in0_b, ws_b, bs_b)
        if smp:
            qp, kp, vp = _mm_qkv0(h, w_in0_b, q_norm_g0, k_norm_g0, tab0, False)
            cache = (cache_k0.reshape(nb * PAST_LEN, B_KV_WIDTH), cache_v0.reshape(nb * PAST_LEN, B_KV_WIDTH))
            k0 = v0 = None
        else:
            qp, kp, vp, k0, v0 = _mm_qkv0(h, w_in0_b, q_norm_g0, k_norm_g0, None, True)
            cache = None
        gate_b = _mm(h, w_in0_b, BF16, 512, (G0_OFF // 512, 1, B_WIDTH // 512))
        y_b = _gqa(qp, kp, vp, cache, gate_b, nb, t, 512 if smp else SEQ)
        x1, h = _out_proj([y_a, y_b], w_out0_b, x, mod0, row_fn(512), next_norm=(norm_g1, mod1))

        if smp:
            qk = _mm_rope(h, w_in1_b, tab1, (C_QK_DIM ** -0.5) * LOG2E)
            vg = _mm(h, w_in1_b, BF16, 1024, (2 * C_WIDTH // 1024, 1, 2 * C_WIDTH // 1024))
            cache = (cache_k1.reshape(nb * PAST_LEN, C_WIDTH), cache_v1.reshape(nb * PAST_LEN, C_WIDTH))
            hb = 1
            second = C_WIDTH // (hb * LANES)
            y_c = _diff((qk, 0), (qk, second), (vg, 0), (vg, second), cache, lam_p, subln_g1,
                        nb, t, 1024, hb, None)
            k1 = v1 = None
        else:
            qg = _mm(h, w_in1_b, BF16, C_WIDTH, (0, 3, 2))
            kb, k1 = _mm_kt(h, w_in1_b, 1, nb)
            v1 = _mm(h, w_in1_b, F32, C_WIDTH, (2, 1, 1), tm=512)
            hb = 8
            y_c = _diff((qg, 0), (kb, 0), (v1, 0), (qg, C_WIDTH // (hb * LANES)), None, lam_p, subln_g1,
                        nb, t, SEQ, hb, (C_QK_DIM ** -0.5) * LOG2E)
        y = _out_proj([y_c], w_out1_b, x1, mod1, row_fn(512), final_g)
        return y, k0, v0, k1, v1

    y_p, k0, v0, k1, v1 = run(x_prompt.reshape(n_ctx * SEQ, D_MODEL), n_ctx, SEQ, False)
    y_s, _, _, _, _ = run(x_sample.reshape(n_smp * DEC_SEQ, D_MODEL), n_smp, DEC_SEQ, True)

    return (
        y_p.reshape(n_ctx, SEQ, D_MODEL),
        y_s.reshape(n_smp, DEC_SEQ, D_MODEL),
        k0.reshape(n_ctx, SEQ, B_KV_HEADS, B_HEAD_DIM),
        v0.reshape(n_ctx, SEQ, B_KV_HEADS, B_HEAD_DIM),
        k1.reshape(n_ctx, C_HEADS, 2, C_QK_DIM, SEQ).transpose(0, 4, 1, 2, 3),
        v1.reshape(n_ctx, SEQ, C_HEADS, C_V_DIM),
    )
```

```python
import functools
import math

import jax
import jax.numpy as jnp
from jax import lax
from jax.experimental import pallas as pl
from jax.experimental.pallas import tpu as pltpu

F32 = jnp.float32
BF16 = jnp.bfloat16

D_MODEL = 2048
SEQ = 256
DEC_SEQ = 4096
PAST_LEN = 256
GRID_W = 64
CHUNK = 128
ROPE_THETA = 10000.0
EPS = 1e-6

A_GROUPS = 8
A_WIDTH = 1024
B_HEADS = 8
B_KV_HEADS = 2
B_GROUP = B_HEADS // B_KV_HEADS
B_HEAD_DIM = 128
B_WIDTH = 1024
B_KV_WIDTH = 256
IN0_WIDTH = 3 * A_WIDTH + 2 * B_WIDTH + 2 * B_KV_WIDTH
C_HEADS = 16
C_QK_DIM = 64
C_V_DIM = 128
C_WIDTH = 2048
IN1_WIDTH = 4 * C_WIDTH
LAMBDA_INIT_1 = 0.8 - 0.6 * math.exp(-0.3 * 1)

LANES = 128
LOG2E = math.log2(math.e)
VMEM_LIMIT = 56 * 1024 * 1024

Q0_OFF = 3 * A_WIDTH
K0_OFF = Q0_OFF + B_WIDTH
V0_OFF = K0_OFF + B_KV_WIDTH
G0_OFF = V0_OFF + B_KV_WIDTH


def _params(*sem):
    return pltpu.CompilerParams(dimension_semantics=sem, vmem_limit_bytes=VMEM_LIMIT)


def _silu(x):
    return x * jax.nn.sigmoid(x)


def _ada_kernel(c_ref, w_ref, b_ref, o_ref):
    s = _silu(c_ref[...]).astype(BF16)
    o_ref[...] = jnp.dot(s, w_ref[...].astype(BF16), preferred_element_type=F32) + b_ref[...]


def _ada(cond, w_ada, b_ada):
    bn = 512
    n = w_ada.shape[1]
    return pl.pallas_call(
        _ada_kernel,
        grid=(n // bn,),
        in_specs=[
            pl.BlockSpec((8, D_MODEL), lambda j: (0, 0)),
            pl.BlockSpec((D_MODEL, bn), lambda j: (0, j)),
            pl.BlockSpec((1, bn), lambda j: (0, j)),
        ],
        out_specs=pl.BlockSpec((8, bn), lambda j: (0, j)),
        out_shape=jax.ShapeDtypeStruct((8, n), F32),
        compiler_params=_params("parallel"),
        name="ada",
    )(cond, w_ada, b_ada.reshape(1, n))


def _mm_kernel(h_ref, w_ref, o_ref):
    o_ref[...] = jnp.dot(h_ref[...], w_ref[...], preferred_element_type=F32).astype(o_ref.dtype)


def _mm(h, w, out_dtype, tn, col_blocks=None, tm=1024):
    t, k = h.shape
    first, stride, count = col_blocks if col_blocks is not None else (0, 1, w.shape[1] // tn)
    assert t % tm == 0 and w.shape[1] % tn == 0
    return pl.pallas_call(
        _mm_kernel,
        grid=(t // tm, count),
        in_specs=[
            pl.BlockSpec((tm, k), lambda i, j: (i, 0)),
            pl.BlockSpec((k, tn), lambda i, j: (0, first + stride * j)),
        ],
        out_specs=pl.BlockSpec((tm, tn), lambda i, j: (i, j)),
        out_shape=jax.ShapeDtypeStruct((t, count * tn), out_dtype),
        compiler_params=_params("parallel", "parallel"),
        name="in_proj",
    )(h, w)


def _mm_kt_kernel(h_ref, w_ref, kb_ref, kt_ref):
    acc = jnp.dot(h_ref[...], w_ref[...], preferred_element_type=F32)
    kb_ref[...] = acc.astype(kb_ref.dtype)
    for b in range(kt_ref.shape[0]):
        kt_ref[b] = acc[b * SEQ:(b + 1) * SEQ, :].T


def _mm_kt(h, w, col_block, n_batch, tb=2):
    t, k = h.shape
    tm = tb * SEQ
    return pl.pallas_call(
        _mm_kt_kernel,
        grid=(t // tm,),
        in_specs=[
            pl.BlockSpec((tm, k), lambda i: (i, 0)),
            pl.BlockSpec((k, C_WIDTH), lambda i: (0, col_block)),
        ],
        out_specs=[
            pl.BlockSpec((tm, C_WIDTH), lambda i: (i, 0)),
            pl.BlockSpec((tb, C_WIDTH, SEQ), lambda i: (i, 0, 0)),
        ],
        out_shape=[
            jax.ShapeDtypeStruct((t, C_WIDTH), BF16),
            jax.ShapeDtypeStruct((n_batch, C_WIDTH, SEQ), F32),
        ],
        compiler_params=_params("parallel"),
        name="in_proj_kt",
    )(h, w)


def _mm_sgate_kernel(h_ref, w_ref, ws_ref, bs_ref, o_ref):
    z = jnp.dot(h_ref[...], w_ref[...], preferred_element_type=F32)
    v = z[:, A_WIDTH:2 * A_WIDTH]
    mu = jnp.mean(v, axis=-1, keepdims=True)
    vc = v - mu
    var = jnp.mean(vc * vc, axis=-1, keepdims=True)
    vn = (vc * lax.rsqrt(var + EPS)).astype(BF16)
    for ch in range(z.shape[0] // CHUNK):
        rows = slice(ch * CHUNK, (ch + 1) * CHUNK)
        for g in range(A_GROUPS):
            cols = slice(g * LANES, (g + 1) * LANES)
            s = jnp.dot(ws_ref[g], vn[rows, cols], preferred_element_type=F32) + bs_ref[g]
            u = z[rows, g * LANES:(g + 1) * LANES]
            gate = z[rows, 2 * A_WIDTH + g * LANES:2 * A_WIDTH + (g + 1) * LANES]
            o_ref[rows, cols] = (u * s * _silu(gate)).astype(o_ref.dtype)


def _mm_sgate(h, w, ws_b, bs_b, tm=512):
    t, k = h.shape
    full = pl.BlockSpec((A_GROUPS, CHUNK, CHUNK), lambda i: (0, 0, 0))
    return pl.pallas_call(
        _mm_sgate_kernel,
        grid=(t // tm,),
        in_specs=[
            pl.BlockSpec((tm, k), lambda i: (i, 0)),
            pl.BlockSpec((k, 3 * A_WIDTH), lambda i: (0, 0)),
            full, full,
        ],
        out_specs=pl.BlockSpec((tm, A_WIDTH), lambda i: (i, 0)),
        out_shape=jax.ShapeDtypeStruct((t, A_WIDTH), BF16),
        compiler_params=_params("parallel"),
        name="in_proj_sgate",
    )(h, w, ws_b, bs_b)


def _rope_tables(half):
    rows = DEC_SEQ // GRID_W
    row_pos = jnp.repeat(jnp.arange(rows, dtype=F32), GRID_W)
    col_pos = jnp.tile(jnp.arange(GRID_W, dtype=F32), rows)
    freqs = ROPE_THETA ** (-jnp.arange(half, dtype=F32) / half)
    ang_r = row_pos[:, None] * freqs[None, :]
    ang_c = col_pos[:, None] * freqs[None, :]
    cr, sr, cc, sc = jnp.cos(ang_r), jnp.sin(ang_r), jnp.cos(ang_c), jnp.sin(ang_c)
    z = jnp.zeros_like(sr)
    reps = LANES // (4 * half)
    cos_t = jnp.tile(jnp.concatenate([cr, cr, cc, cc], axis=-1), (1, reps))
    sin_a = jnp.tile(jnp.concatenate([z, sr, z, sc], axis=-1), (1, reps))
    sin_b = jnp.tile(jnp.concatenate([-sr, z, -sc, z], axis=-1), (1, reps))
    return cos_t, sin_a, sin_b


def _rope_partner_matrix(half):
    lane = jnp.arange(LANES)
    first = (lane % (2 * half)) < half
    src = jnp.where(first, lane + half, lane - half)
    sign = jnp.where(first, -1.0, 1.0)
    return jnp.zeros((LANES, LANES), F32).at[src, lane].set(sign).astype(BF16)


def _rope(x, cos_t, sin_a, sin_b, half):
    return x * cos_t + pltpu.roll(x, half, 1) * sin_a + pltpu.roll(x, LANES - half, 1) * sin_b


QKV0_SUB_ROWS = 256


def _mm_qkv0_kernel(*refs, rope, emit_f32):
    x_ref, ng_ref, shift_ref, scale_ref, w_ref, qg_ref, kg_ref = refs[:7]
    refs = refs[7:]
    if rope:
        cos_ref, sin_ref, perm_ref = refs[:3]
        refs = refs[3:]
    h_ref, qo_ref, ko_ref, vo_ref = refs[:4]
    if emit_f32:
        kf_ref, vf_ref = refs[4:6]

    def norm(x, g):
        ms = jnp.mean(x * x, axis=-1, keepdims=True)
        return x * lax.rsqrt(ms + EPS) * g

    def rot(x, rows):
        if not rope:
            return x
        partner = jnp.dot(x.astype(BF16), perm_ref[...], preferred_element_type=F32)
        return x * cos_ref[rows, :] + partner * sin_ref[rows, :]

    qscale = (B_HEAD_DIM ** -0.5) * LOG2E
    for r0 in range(0, x_ref.shape[0], QKV0_SUB_ROWS):
        rows = slice(r0, r0 + QKV0_SUB_ROWS)
        hs = (norm(x_ref[rows, :], ng_ref[...]) * (1.0 + scale_ref[...]) + shift_ref[...]).astype(BF16)
        h_ref[rows, :] = hs
        z = jnp.dot(hs, w_ref[...], preferred_element_type=F32)
        for h in range(B_HEADS):
            cols = slice(h * LANES, (h + 1) * LANES)
            qn = norm(z[:, cols], qg_ref[...])
            qo_ref[rows, cols] = (rot(qn, rows) * qscale).astype(qo_ref.dtype)
        for h in range(B_KV_HEADS):
            cols = slice(h * LANES, (h + 1) * LANES)
            kn = norm(z[:, B_WIDTH + h * LANES:B_WIDTH + (h + 1) * LANES], kg_ref[...])
            ko_ref[rows, cols] = rot(kn, rows).astype(ko_ref.dtype)
            if emit_f32:
                kf_ref[rows, cols] = kn
        v = z[:, B_WIDTH + B_KV_WIDTH:]
        vo_ref[rows, :] = v.astype(vo_ref.dtype)
        if emit_f32:
            vf_ref[rows, :] = v


def _norm_mm_qkv0(x, norm_g, mod3, row_of_tile, w, q_g, k_g, tables, emit_f32, tm=512):
    t, kdim = x.shape
    rope = tables is not None
    qkv_w = B_WIDTH + 2 * B_KV_WIDTH
    in_specs = [
        pl.BlockSpec((tm, kdim), lambda i: (i, 0)),
        pl.BlockSpec((1, kdim), lambda i: (0, 0)),
        pl.BlockSpec((None, 1, kdim), lambda i: (row_of_tile(i), 0, 0)),
        pl.BlockSpec((None, 1, kdim), lambda i: (row_of_tile(i), 0, 1)),
        pl.BlockSpec((kdim, qkv_w), lambda i: (0, Q0_OFF // qkv_w)),
        pl.BlockSpec((1, LANES), lambda i: (0, 0)),
        pl.BlockSpec((1, LANES), lambda i: (0, 0)),
    ]
    args = [x, norm_g.reshape(1, kdim), mod3, mod3, w, q_g.reshape(1, LANES), k_g.reshape(1, LANES)]
    if rope:
        nt = DEC_SEQ // tm
        in_specs += [pl.BlockSpec((tm, LANES), lambda i: (i % nt, 0))] * 2
        in_specs.append(pl.BlockSpec((LANES, LANES), lambda i: (0, 0)))
        args += list(tables)
    out_specs = [
        pl.BlockSpec((tm, kdim), lambda i: (i, 0)),
        pl.BlockSpec((tm, B_WIDTH), lambda i: (i, 0)),
        pl.BlockSpec((tm, B_KV_WIDTH), lambda i: (i, 0)),
        pl.BlockSpec((tm, B_KV_WIDTH), lambda i: (i, 0)),
    ]
    out_shape = [
        jax.ShapeDtypeStruct((t, kdim), BF16),
        jax.ShapeDtypeStruct((t, B_WIDTH), BF16),
        jax.ShapeDtypeStruct((t, B_KV_WIDTH), BF16),
        jax.ShapeDtypeStruct((t, B_KV_WIDTH), BF16),
    ]
    if emit_f32:
        out_specs += [pl.BlockSpec((tm, B_KV_WIDTH), lambda i: (i, 0))] * 2
        out_shape += [jax.ShapeDtypeStruct((t, B_KV_WIDTH), F32)] * 2
    assert Q0_OFF % qkv_w == 0
    return pl.pallas_call(
        functools.partial(_mm_qkv0_kernel, rope=rope, emit_f32=emit_f32),
        grid=(t // tm,),
        in_specs=in_specs,
        out_specs=out_specs,
        out_shape=out_shape,
        compiler_params=_params("parallel"),
        name="in_proj_qkv",
    )(*args)


ROPE_SUB_COLS = 1024


def _mm_rope_kernel(h_ref, w_ref, cos_ref, sa_ref, sb_ref, o_ref, *, q_tiles, q_scale):
    scale = jnp.where(pl.program_id(1) < q_tiles, q_scale, 1.0)
    cos_t, sa, sb = cos_ref[...], sa_ref[...], sb_ref[...]
    h = h_ref[...]
    for c0 in range(0, w_ref.shape[1], ROPE_SUB_COLS):
        acc = jnp.dot(h, w_ref[:, c0:c0 + ROPE_SUB_COLS], preferred_element_type=F32)
        for c in range(ROPE_SUB_COLS // LANES):
            x = acc[:, c * LANES:(c + 1) * LANES]
            cols = slice(c0 + c * LANES, c0 + (c + 1) * LANES)
            o_ref[:, cols] = (_rope(x, cos_t, sa, sb, C_QK_DIM // 4) * scale).astype(o_ref.dtype)


def _mm_rope(h, w, tables, q_scale, tm=1024, tn=C_WIDTH):
    t, k = h.shape
    nt = DEC_SEQ // tm
    tab = pl.BlockSpec((tm, LANES), lambda i, j: (i % nt, 0))
    return pl.pallas_call(
        functools.partial(_mm_rope_kernel, q_tiles=C_WIDTH // tn, q_scale=q_scale),
        grid=(t // tm, 2 * C_WIDTH // tn),
        in_specs=[
            pl.BlockSpec((tm, k), lambda i, j: (i, 0)),
            pl.BlockSpec((k, tn), lambda i, j: (0, j)),
            tab, tab, tab,
        ],
        out_specs=pl.BlockSpec((tm, tn), lambda i, j: (i, j)),
        out_shape=jax.ShapeDtypeStruct((t, 2 * C_WIDTH), BF16),
        compiler_params=_params("parallel", "parallel"),
        name="in_proj_rope",
    )(h, w, *tables)


KEY_CHUNK = 256
SUBLANES = 8


MAX_KEY_BLOCK = 1024


def _key_blocks(s_new, has_cache):
    size = min(s_new, MAX_KEY_BLOCK)
    blocks = [(r, size) for r in range(0, s_new, size)]
    if has_cache:
        blocks.append((s_new, PAST_LEN))
    return blocks


def _attn_pipeline(units, blocks, s_scr, p_scr):
    n = s_scr.shape[2]
    st = [dict() for _ in units]

    def stage1(u, b):
        r0, size = blocks[b]
        if b == 0:
            st[u]["qq"] = units[u][0]()
        s = lax.dot_general(units[u][1](b), st[u]["qq"], (((1,), (1,)), ((), ())), preferred_element_type=F32)
        s_scr[u % 2, r0:r0 + size, :] = s
        m8 = s.reshape(size // SUBLANES, SUBLANES, n).max(axis=0)
        st[u]["m8"] = m8 if b == 0 else jnp.maximum(st[u]["m8"], m8)
        if b == len(blocks) - 1:
            st[u]["m"] = st[u]["m8"].max(axis=0, keepdims=True)

    def stage2(u, b):
        r0, size = blocks[b]
        for r in range(r0, r0 + size, KEY_CHUNK):
            p = jnp.exp2(s_scr[u % 2, r:r + KEY_CHUNK, :] - st[u]["m"])
            l8 = p.reshape(KEY_CHUNK // SUBLANES, SUBLANES, n).sum(axis=0)
            st[u]["l8"] = l8 if r == 0 else st[u]["l8"] + l8
            p_scr[u % 2, r:r + KEY_CHUNK, :] = p.astype(BF16)

    def stage3(u, b):
        r0, size = blocks[b]
        a = jnp.dot(units[u][2][:, r0:r0 + size], p_scr[u % 2, r0:r0 + size, :], preferred_element_type=F32)
        st[u]["acc"] = a if b == 0 else st[u]["acc"] + a
        if b == len(blocks) - 1:
            l = st[u]["l8"].sum(axis=0, keepdims=True)
            units[u][3](st[u]["acc"] * (1.0 / l))

    for slot in range(len(units) + 2):
        for b in range(len(blocks)):
            if 0 <= slot - 2 < len(units):
                stage3(slot - 2, b)
            if slot < len(units):
                stage1(slot, b)
            if 0 <= slot - 1 < len(units):
                stage2(slot - 1, b)


def _fill_vt(vt_scr, v_ref, vc_ref, n_kv):
    s_new = v_ref.shape[0]
    for j in range(n_kv):
        cols = slice(j * LANES, (j + 1) * LANES)
        vt_scr[j, :, 0:s_new] = v_ref[:, cols].astype(F32).T.astype(BF16)
        if vc_ref is not None:
            vt_scr[j, :, s_new:] = vc_ref[:, cols].astype(F32).T.astype(BF16)


def _attn_scratch(n_kv, s_tot, n):
    return [
        pltpu.VMEM((n_kv, LANES, s_tot), BF16),
        pltpu.VMEM((2, s_tot, n), F32),
        pltpu.VMEM((2, s_tot, n), BF16),
    ]


TQ = 256


def _gqa_kernel(*refs, has_cache):
    q_ref, k_ref, v_ref = refs[:3]
    refs = refs[3:]
    kc_ref = vc_ref = None
    if has_cache:
        (kc_ref, vc_ref), refs = refs[:2], refs[2:]
    g_ref, o_ref, vt_scr, s_scr, p_scr = refs
    blocks = _key_blocks(k_ref.shape[0], has_cache)

    @pl.when(pl.program_id(2) == 0)
    def _():
        _fill_vt(vt_scr, v_ref, vc_ref, 1)

    def key_block(b):
        r0, size = blocks[b]
        if r0 < k_ref.shape[0]:
            return k_ref[r0:r0 + size, :].astype(BF16)
        return kc_ref[...].astype(BF16)

    def make_unit(rows, heads):
        def make_qq():
            return jnp.concatenate([q_ref[rows, h * LANES:(h + 1) * LANES] for h in heads], axis=0)

        def finish(ot):
            for i, h in enumerate(heads):
                cols = slice(h * LANES, (h + 1) * LANES)
                gate = g_ref[rows, cols].astype(F32)
                o_ref[rows, cols] = (ot[:, i * TQ:(i + 1) * TQ].T * _silu(gate)).astype(o_ref.dtype)

        return make_qq, key_block, vt_scr.at[0], finish

    units = [make_unit(slice(r, r + TQ), (h, h + 1))
             for r in range(0, q_ref.shape[0], TQ) for h in range(0, B_GROUP, 2)]
    _attn_pipeline(units, blocks, s_scr, p_scr)


def _gqa(qp, kp, vp, cache, gate, nb, t, tqs):
    nq = t // tqs
    qw = B_GROUP * LANES
    s_tot = t + (PAST_LEN if cache is not None else 0)
    in_specs = [
        pl.BlockSpec((tqs, qw), lambda b, h, i: (b * nq + i, h)),
        pl.BlockSpec((t, LANES), lambda b, h, i: (b, h)),
        pl.BlockSpec((t, LANES), lambda b, h, i: (b, h)),
    ]
    args = [qp, kp, vp]
    if cache is not None:
        in_specs += [pl.BlockSpec((PAST_LEN, LANES), lambda b, h, i: (b, h))] * 2
        args += list(cache)
    in_specs.append(pl.BlockSpec((tqs, qw), lambda b, h, i: (b * nq + i, h)))
    args.append(gate)
    return pl.pallas_call(
        functools.partial(_gqa_kernel, has_cache=cache is not None),
        grid=(nb, B_KV_HEADS, nq),
        in_specs=in_specs,
        out_specs=pl.BlockSpec((tqs, qw), lambda b, h, i: (b * nq + i, h)),
        out_shape=jax.ShapeDtypeStruct((nb * t, B_WIDTH), BF16),
        scratch_shapes=_attn_scratch(1, s_tot, 2 * TQ),
        compiler_params=_params("parallel", "parallel", "arbitrary"),
        name="gqa_attn",
    )(*args)


def _diff_kernel(*refs, has_cache, hb, q_scale):
    q_ref, k_ref, v_ref = refs[:3]
    refs = refs[3:]
    kc_ref = vc_ref = None
    if has_cache:
        (kc_ref, vc_ref), refs = refs[:2], refs[2:]
    g_ref, lam_ref, sg_ref, o_ref, vt_scr, s_scr, p_scr = refs
    blocks = _key_blocks(k_ref.shape[0], has_cache)

    @pl.when(pl.program_id(2) == 0)
    def _():
        _fill_vt(vt_scr, v_ref, vc_ref, hb)

    lp = lam_ref[...]
    lam = (jnp.exp(jnp.sum(lp[0:1] * lp[1:2], axis=-1, keepdims=True))
           - jnp.exp(jnp.sum(lp[2:3] * lp[3:4], axis=-1, keepdims=True)) + LAMBDA_INIT_1)
    lane = lax.broadcasted_iota(jnp.int32, (TQ, LANES), 1)

    def make_unit(rows, j):
        cols = slice(j * LANES, (j + 1) * LANES)

        def make_qq():
            q = q_ref[rows, cols]
            if q_scale is not None:
                q = q.astype(F32) * q_scale
            zero = jnp.zeros_like(q)
            maps = [jnp.where(lane < C_QK_DIM, q, zero), jnp.where(lane >= C_QK_DIM, q, zero)]
            return jnp.concatenate(maps, axis=0).astype(BF16)

        def key_block(b):
            r0, size = blocks[b]
            if r0 < k_ref.shape[0]:
                return k_ref[r0:r0 + size, cols].astype(BF16)
            return kc_ref[:, cols].astype(BF16)

        def finish(ot):
            ot = ot[:, :TQ] - lam * ot[:, TQ:]
            ms = jnp.mean(ot * ot, axis=0, keepdims=True)
            o = (ot * lax.rsqrt(ms + EPS)).T * sg_ref[...] * (1.0 - LAMBDA_INIT_1)
            gate = g_ref[rows, cols].astype(F32)
            o_ref[rows, cols] = (o * _silu(gate)).astype(o_ref.dtype)

        return make_qq, key_block, vt_scr.at[j], finish

    units = [make_unit(slice(r, r + TQ), j) for j in range(hb) for r in range(0, q_ref.shape[0], TQ)]
    _attn_pipeline(units, blocks, s_scr, p_scr)


def _diff(q, k, v, g, cache, lam_p, sub_g, nb, t, tq, hb, q_scale):
    nq = t // tq
    w = hb * LANES
    s_tot = t + (PAST_LEN if cache is not None else 0)
    in_specs = [
        pl.BlockSpec((tq, w), lambda b, h, i: (b * nq + i, q[1] + h)),
        pl.BlockSpec((t, w), lambda b, h, i: (b, k[1] + h)),
        pl.BlockSpec((t, w), lambda b, h, i: (b, v[1] + h)),
    ]
    args = [q[0], k[0], v[0]]
    if cache is not None:
        in_specs += [pl.BlockSpec((PAST_LEN, w), lambda b, h, i: (b, h))] * 2
        args += list(cache)
    in_specs += [
        pl.BlockSpec((tq, w), lambda b, h, i: (b * nq + i, g[1] + h)),
        pl.BlockSpec((4, C_QK_DIM), lambda b, h, i: (0, 0)),
        pl.BlockSpec((1, LANES), lambda b, h, i: (0, 0)),
    ]
    args += [g[0], lam_p, sub_g.reshape(1, LANES)]
    return pl.pallas_call(
        functools.partial(_diff_kernel, has_cache=cache is not None, hb=hb, q_scale=q_scale),
        grid=(nb, C_HEADS // hb, nq),
        in_specs=in_specs,
        out_specs=pl.BlockSpec((tq, w), lambda b, h, i: (b * nq + i, h)),
        out_shape=jax.ShapeDtypeStruct((nb * t, C_WIDTH), BF16),
        scratch_shapes=_attn_scratch(hb, s_tot, 2 * TQ),
        compiler_params=_params("parallel", "parallel", "arbitrary"),
        name="diff_attn",
    )(*args)


def _out_kernel(*refs, n_in, final):
    y_refs = refs[:n_in]
    w_ref, x_ref, gate_ref = refs[n_in:n_in + 3]
    refs = refs[n_in + 3:]
    half = x_ref.shape[0] // 2
    for rows in (slice(0, half), slice(half, 2 * half)):
        acc = None
        k0 = 0
        for y_ref in y_refs:
            kk = y_ref.shape[1]
            a = jnp.dot(y_ref[rows, :], w_ref[k0:k0 + kk, :], preferred_element_type=F32)
            acc = a if acc is None else acc + a
            k0 += kk
        x = x_ref[rows, :] + gate_ref[...] * acc
        xn = x * lax.rsqrt(jnp.mean(x * x, axis=-1, keepdims=True) + EPS)
        if final:
            fg_ref, o_ref = refs
            o_ref[rows, :] = xn * fg_ref[...]
        else:
            g_ref, shift_ref, scale_ref, o_ref, h_ref = refs
            o_ref[rows, :] = x
            h_ref[rows, :] = (xn * g_ref[...] * (1.0 + scale_ref[...]) + shift_ref[...]).astype(h_ref.dtype)


def _out_proj(ys, w, x, mod3, row_of_tile, final_g=None, next_norm=None, tm=512):
    t = x.shape[0]
    final = final_g is not None
    vec = pl.BlockSpec((1, D_MODEL), lambda i: (0, 0))
    mod_row = lambda part: pl.BlockSpec((None, 1, D_MODEL), lambda i: (row_of_tile(i), 0, part))
    in_specs = [pl.BlockSpec((tm, y.shape[1]), lambda i: (i, 0)) for y in ys]
    in_specs += [
        pl.BlockSpec(w.shape, lambda i: (0, 0)),
        pl.BlockSpec((tm, D_MODEL), lambda i: (i, 0)),
        mod_row(2),
    ]
    args = list(ys) + [w, x, mod3]
    row_blk = pl.BlockSpec((tm, D_MODEL), lambda i: (i, 0))
    if final:
        in_specs.append(vec)
        args.append(final_g.reshape(1, D_MODEL))
        out_specs, out_shape = row_blk, jax.ShapeDtypeStruct((t, D_MODEL), F32)
    else:
        next_g, next_mod3 = next_norm
        in_specs += [vec, mod_row(0), mod_row(1)]
        args += [next_g.reshape(1, D_MODEL), next_mod3, next_mod3]
        out_specs = [row_blk, row_blk]
        out_shape = [jax.ShapeDtypeStruct((t, D_MODEL), F32), jax.ShapeDtypeStruct((t, D_MODEL), BF16)]
    return pl.pallas_call(
        functools.partial(_out_kernel, n_in=len(ys), final=final),
        grid=(t // tm,),
        in_specs=in_specs,
        out_specs=out_specs,
        out_shape=out_shape,
        compiler_params=_params("parallel"),
        name="out_proj",
    )(*args)


def kernel(x_prompt, x_sample, cache_k0, cache_v0, cache_k1, cache_v1, c, c_ctx, w_ada0, b_ada0, norm_g0, w_in0, w_s0, b_s0, q_norm_g0, k_norm_g0, w_out0, w_ada1, b_ada1, norm_g1, w_in1, lambda_q1, lambda_k1, lambda_q2, lambda_k2, subln_g1, w_out1, final_g):
    n_ctx, n_smp = x_prompt.shape[0], x_sample.shape[0]

    cond = jnp.concatenate([c_ctx[None], c, jnp.zeros((8 - 1 - n_smp, D_MODEL), F32)], axis=0)
    mod0 = _ada(cond, w_ada0, b_ada0).reshape(8, 1, 3 * D_MODEL)
    mod1 = _ada(cond, w_ada1, b_ada1).reshape(8, 1, 3 * D_MODEL)

    w_in0_b, w_out0_b = w_in0.astype(BF16), w_out0.astype(BF16)
    w_in1_b, w_out1_b = w_in1.astype(BF16), w_out1.astype(BF16)
    ws_b = w_s0.astype(BF16)
    bs_b = jnp.broadcast_to(b_s0[:, :, None], (A_GROUPS, CHUNK, LANES))
    lam_p = jnp.stack([lambda_q1, lambda_k1, lambda_q2, lambda_k2]).astype(F32)
    cos0, sin_a0, sin_b0 = _rope_tables(B_HEAD_DIM // 4)
    tab0 = (cos0, sin_a0 - sin_b0, _rope_partner_matrix(B_HEAD_DIM // 4))
    tab1 = _rope_tables(C_QK_DIM // 4)

    def run(x, nb, t, smp):
        def row_fn(tm):
            if not smp:
                return lambda i: 0
            return lambda i: 1 + (i * tm) // t

        front = (x, norm_g0, mod0, row_fn(512), w_in0_b, q_norm_g0, k_norm_g0)
        if smp:
            h, qp, kp, vp = _norm_mm_qkv0(*front, tab0, False)
            cache = (cache_k0.reshape(nb * PAST_LEN, B_KV_WIDTH), cache_v0.reshape(nb * PAST_LEN, B_KV_WIDTH))
            k0 = v0 = None
        else:
            h, qp, kp, vp, k0, v0 = _norm_mm_qkv0(*front, None, True)
            cache = None
        y_a = _mm_sgate(h, w_in0_b, ws_b, bs_b)
        gate_b = _mm(h, w_in0_b, BF16, 512, (G0_OFF // 512, 1, B_WIDTH // 512))
        y_b = _gqa(qp, kp, vp, cache, gate_b, nb, t, 512 if smp else SEQ)
        x1, h = _out_proj([y_a, y_b], w_out0_b, x, mod0, row_fn(512), next_norm=(norm_g1, mod1))

        if smp:
            qk = _mm_rope(h, w_in1_b, tab1, (C_QK_DIM ** -0.5) * LOG2E)
            vg = _mm(h, w_in1_b, BF16, 1024, (2 * C_WIDTH // 1024, 1, 2 * C_WIDTH // 1024))
            cache = (cache_k1.reshape(nb * PAST_LEN, C_WIDTH), cache_v1.reshape(nb * PAST_LEN, C_WIDTH))
            hb = 1
            second = C_WIDTH // (hb * LANES)
            y_c = _diff((qk, 0), (qk, second), (vg, 0), (vg, second), cache, lam_p, subln_g1,
                        nb, t, 1024, hb, None)
            k1 = v1 = None
        else:
            qg = _mm(h, w_in1_b, BF16, C_WIDTH, (0, 3, 2))
            kb, k1 = _mm_kt(h, w_in1_b, 1, nb)
            v1 = _mm(h, w_in1_b, F32, C_WIDTH, (2, 1, 1), tm=512)
            hb = 8
            y_c = _diff((qg, 0), (kb, 0), (v1, 0), (qg, C_WIDTH // (hb * LANES)), None, lam_p, subln_g1,
                        nb, t, SEQ, hb, (C_QK_DIM ** -0.5) * LOG2E)
        y = _out_proj([y_c], w_out1_b, x1, mod1, row_fn(512), final_g)
        return y, k0, v0, k1, v1

    y_p, k0, v0, k1, v1 = run(x_prompt.reshape(n_ctx * SEQ, D_MODEL), n_ctx, SEQ, False)
    y_s, _, _, _, _ = run(x_sample.reshape(n_smp * DEC_SEQ, D_MODEL), n_smp, DEC_SEQ, True)

    return (
        y_p.reshape(n_ctx, SEQ, D_MODEL),
        y_s.reshape(n_smp, DEC_SEQ, D_MODEL),
        k0.reshape(n_ctx, SEQ, B_KV_HEADS, B_HEAD_DIM),
        v0.reshape(n_ctx, SEQ, B_KV_HEADS, B_HEAD_DIM),
        k1.reshape(n_ctx, C_HEADS, 2, C_QK_DIM, SEQ).transpose(0, 4, 1, 2, 3),
        v1.reshape(n_ctx, SEQ, C_HEADS, C_V_DIM),
    )
```

```python
import functools
import math

import jax
import jax.numpy as jnp
from jax import lax
from jax.experimental import pallas as pl
from jax.experimental.pallas import tpu as pltpu

F32 = jnp.float32
BF16 = jnp.bfloat16

D_MODEL = 2048
SEQ = 256
DEC_SEQ = 4096
PAST_LEN = 256
GRID_W = 64
CHUNK = 128
ROPE_THETA = 10000.0
EPS = 1e-6

A_GROUPS = 8
A_WIDTH = 1024
B_HEADS = 8
B_KV_HEADS = 2
B_GROUP = B_HEADS // B_KV_HEADS
B_HEAD_DIM = 128
B_WIDTH = 1024
B_KV_WIDTH = 256
IN0_WIDTH = 3 * A_WIDTH + 2 * B_WIDTH + 2 * B_KV_WIDTH
C_HEADS = 16
C_QK_DIM = 64
C_V_DIM = 128
C_WIDTH = 2048
IN1_WIDTH = 4 * C_WIDTH
LAMBDA_INIT_1 = 0.8 - 0.6 * math.exp(-0.3 * 1)

LANES = 128
LOG2E = math.log2(math.e)
VMEM_LIMIT = 56 * 1024 * 1024

Q0_OFF = 3 * A_WIDTH
K0_OFF = Q0_OFF + B_WIDTH
V0_OFF = K0_OFF + B_KV_WIDTH
G0_OFF = V0_OFF + B_KV_WIDTH


def _params(*sem):
    return pltpu.CompilerParams(dimension_semantics=sem, vmem_limit_bytes=VMEM_LIMIT)


def _silu(x):
    return x * jax.nn.sigmoid(x)


def _ada_kernel(c_ref, w_ref, b_ref, o_ref):
    s = _silu(c_ref[...]).astype(BF16)
    o_ref[...] = jnp.dot(s, w_ref[...].astype(BF16), preferred_element_type=F32) + b_ref[...]


def _ada(cond, w_ada, b_ada):
    bn = 512
    n = w_ada.shape[1]
    return pl.pallas_call(
        _ada_kernel,
        grid=(n // bn,),
        in_specs=[
            pl.BlockSpec((8, D_MODEL), lambda j: (0, 0)),
            pl.BlockSpec((D_MODEL, bn), lambda j: (0, j)),
            pl.BlockSpec((1, bn), lambda j: (0, j)),
        ],
        out_specs=pl.BlockSpec((8, bn), lambda j: (0, j)),
        out_shape=jax.ShapeDtypeStruct((8, n), F32),
        compiler_params=_params("parallel"),
        name="ada",
    )(cond, w_ada, b_ada.reshape(1, n))


def _mm_kernel(h_ref, w_ref, o_ref):
    o_ref[...] = jnp.dot(h_ref[...], w_ref[...], preferred_element_type=F32).astype(o_ref.dtype)


def _mm(h, w, out_dtype, tn, col_blocks=None, tm=1024):
    t, k = h.shape
    first, stride, count = col_blocks if col_blocks is not None else (0, 1, w.shape[1] // tn)
    assert t % tm == 0 and w.shape[1] % tn == 0
    return pl.pallas_call(
        _mm_kernel,
        grid=(t // tm, count),
        in_specs=[
            pl.BlockSpec((tm, k), lambda i, j: (i, 0)),
            pl.BlockSpec((k, tn), lambda i, j: (0, first + stride * j)),
        ],
        out_specs=pl.BlockSpec((tm, tn), lambda i, j: (i, j)),
        out_shape=jax.ShapeDtypeStruct((t, count * tn), out_dtype),
        compiler_params=_params("parallel", "parallel"),
        name="in_proj",
    )(h, w)


def _mm_kt_kernel(h_ref, w_ref, kb_ref, kt_ref):
    acc = jnp.dot(h_ref[...], w_ref[...], preferred_element_type=F32)
    kb_ref[...] = acc.astype(kb_ref.dtype)
    for b in range(kt_ref.shape[0]):
        kt_ref[b] = acc[b * SEQ:(b + 1) * SEQ, :].T


def _mm_kt(h, w, col_block, n_batch, tb=2):
    t, k = h.shape
    tm = tb * SEQ
    return pl.pallas_call(
        _mm_kt_kernel,
        grid=(t // tm,),
        in_specs=[
            pl.BlockSpec((tm, k), lambda i: (i, 0)),
            pl.BlockSpec((k, C_WIDTH), lambda i: (0, col_block)),
        ],
        out_specs=[
            pl.BlockSpec((tm, C_WIDTH), lambda i: (i, 0)),
            pl.BlockSpec((tb, C_WIDTH, SEQ), lambda i: (i, 0, 0)),
        ],
        out_shape=[
            jax.ShapeDtypeStruct((t, C_WIDTH), BF16),
            jax.ShapeDtypeStruct((n_batch, C_WIDTH, SEQ), F32),
        ],
        compiler_params=_params("parallel"),
        name="in_proj_kt",
    )(h, w)


def _mm_sgate_kernel(h_ref, w_ref, ws_ref, bs_ref, o_ref):
    z = jnp.dot(h_ref[...], w_ref[...], preferred_element_type=F32)
    v = z[:, A_WIDTH:2 * A_WIDTH]
    mu = jnp.mean(v, axis=-1, keepdims=True)
    vc = v - mu
    var = jnp.mean(vc * vc, axis=-1, keepdims=True)
    vn = (vc * lax.rsqrt(var + EPS)).astype(BF16)
    for ch in range(z.shape[0] // CHUNK):
        rows = slice(ch * CHUNK, (ch + 1) * CHUNK)
        for g in range(A_GROUPS):
            cols = slice(g * LANES, (g + 1) * LANES)
            s = jnp.dot(ws_ref[g], vn[rows, cols], preferred_element_type=F32) + bs_ref[g]
            u = z[rows, g * LANES:(g + 1) * LANES]
            gate = z[rows, 2 * A_WIDTH + g * LANES:2 * A_WIDTH + (g + 1) * LANES]
            o_ref[rows, cols] = (u * s * _silu(gate)).astype(o_ref.dtype)


def _mm_sgate(h, w, ws_b, bs_b, tm=512):
    t, k = h.shape
    full = pl.BlockSpec((A_GROUPS, CHUNK, CHUNK), lambda i: (0, 0, 0))
    return pl.pallas_call(
        _mm_sgate_kernel,
        grid=(t // tm,),
        in_specs=[
            pl.BlockSpec((tm, k), lambda i: (i, 0)),
            pl.BlockSpec((k, 3 * A_WIDTH), lambda i: (0, 0)),
            full, full,
        ],
        out_specs=pl.BlockSpec((tm, A_WIDTH), lambda i: (i, 0)),
        out_shape=jax.ShapeDtypeStruct((t, A_WIDTH), BF16),
        compiler_params=_params("parallel"),
        name="in_proj_sgate",
    )(h, w, ws_b, bs_b)


def _rope_tables(half):
    rows = DEC_SEQ // GRID_W
    row_pos = jnp.repeat(jnp.arange(rows, dtype=F32), GRID_W)
    col_pos = jnp.tile(jnp.arange(GRID_W, dtype=F32), rows)
    freqs = ROPE_THETA ** (-jnp.arange(half, dtype=F32) / half)
    ang_r = row_pos[:, None] * freqs[None, :]
    ang_c = col_pos[:, None] * freqs[None, :]
    cr, sr, cc, sc = jnp.cos(ang_r), jnp.sin(ang_r), jnp.cos(ang_c), jnp.sin(ang_c)
    z = jnp.zeros_like(sr)
    reps = LANES // (4 * half)
    cos_t = jnp.tile(jnp.concatenate([cr, cr, cc, cc], axis=-1), (1, reps))
    sin_a = jnp.tile(jnp.concatenate([z, sr, z, sc], axis=-1), (1, reps))
    sin_b = jnp.tile(jnp.concatenate([-sr, z, -sc, z], axis=-1), (1, reps))
    return cos_t, sin_a, sin_b


def _rope_partner_matrix(half):
    lane = jnp.arange(LANES)
    first = (lane % (2 * half)) < half
    src = jnp.where(first, lane + half, lane - half)
    sign = jnp.where(first, -1.0, 1.0)
    return jnp.zeros((LANES, LANES), F32).at[src, lane].set(sign).astype(BF16)


def _rope(x, cos_t, sin_a, sin_b, half):
    return x * cos_t + pltpu.roll(x, half, 1) * sin_a + pltpu.roll(x, LANES - half, 1) * sin_b


QKV0_SUB_ROWS = 256


def _mm_qkv0_kernel(*refs, rope, emit_f32):
    x_ref, ng_ref, shift_ref, scale_ref, w_ref, qg_ref, kg_ref = refs[:7]
    refs = refs[7:]
    if rope:
        cos_ref, sin_ref, perm_ref = refs[:3]
        refs = refs[3:]
    h_ref, qo_ref, ko_ref, vo_ref = refs[:4]
    if emit_f32:
        kf_ref, vf_ref = refs[4:6]

    def norm(x, g):
        ms = jnp.mean(x * x, axis=-1, keepdims=True)
        return x * lax.rsqrt(ms + EPS) * g

    def rot(x, rows):
        if not rope:
            return x
        partner = jnp.dot(x.astype(BF16), perm_ref[...], preferred_element_type=F32)
        return x * cos_ref[rows, :] + partner * sin_ref[rows, :]

    qscale = (B_HEAD_DIM ** -0.5) * LOG2E
    for r0 in range(0, x_ref.shape[0], QKV0_SUB_ROWS):
        rows = slice(r0, r0 + QKV0_SUB_ROWS)
        hs = (norm(x_ref[rows, :], ng_ref[...]) * (1.0 + scale_ref[...]) + shift_ref[...]).astype(BF16)
        h_ref[rows, :] = hs
        z = jnp.dot(hs, w_ref[...], preferred_element_type=F32)
        for h in range(B_HEADS):
            cols = slice(h * LANES, (h + 1) * LANES)
            qn = norm(z[:, cols], qg_ref[...])
            qo_ref[rows, cols] = (rot(qn, rows) * qscale).astype(qo_ref.dtype)
        for h in range(B_KV_HEADS):
            cols = slice(h * LANES, (h + 1) * LANES)
            kn = norm(z[:, B_WIDTH + h * LANES:B_WIDTH + (h + 1) * LANES], kg_ref[...])
            ko_ref[rows, cols] = rot(kn, rows).astype(ko_ref.dtype)
            if emit_f32:
                kf_ref[rows, cols] = kn
        v = z[:, B_WIDTH + B_KV_WIDTH:]
        vo_ref[rows, :] = v.astype(vo_ref.dtype)
        if emit_f32:
            vf_ref[rows, :] = v


def _norm_mm_qkv0(x, norm_g, mod3, row_of_tile, w, q_g, k_g, tables, emit_f32, tm=512):
    t, kdim = x.shape
    rope = tables is not None
    qkv_w = B_WIDTH + 2 * B_KV_WIDTH
    in_specs = [
        pl.BlockSpec((tm, kdim), lambda i: (i, 0)),
        pl.BlockSpec((1, kdim), lambda i: (0, 0)),
        pl.BlockSpec((None, 1, kdim), lambda i: (row_of_tile(i), 0, 0)),
        pl.BlockSpec((None, 1, kdim), lambda i: (row_of_tile(i), 0, 1)),
        pl.BlockSpec((kdim, qkv_w), lambda i: (0, Q0_OFF // qkv_w)),
        pl.BlockSpec((1, LANES), lambda i: (0, 0)),
        pl.BlockSpec((1, LANES), lambda i: (0, 0)),
    ]
    args = [x, norm_g.reshape(1, kdim), mod3, mod3, w, q_g.reshape(1, LANES), k_g.reshape(1, LANES)]
    if rope:
        nt = DEC_SEQ // tm
        in_specs += [pl.BlockSpec((tm, LANES), lambda i: (i % nt, 0))] * 2
        in_specs.append(pl.BlockSpec((LANES, LANES), lambda i: (0, 0)))
        args += list(tables)
    out_specs = [
        pl.BlockSpec((tm, kdim), lambda i: (i, 0)),
        pl.BlockSpec((tm, B_WIDTH), lambda i: (i, 0)),
        pl.BlockSpec((tm, B_KV_WIDTH), lambda i: (i, 0)),
        pl.BlockSpec((tm, B_KV_WIDTH), lambda i: (i, 0)),
    ]
    out_shape = [
        jax.ShapeDtypeStruct((t, kdim), BF16),
        jax.ShapeDtypeStruct((t, B_WIDTH), BF16),
        jax.ShapeDtypeStruct((t, B_KV_WIDTH), BF16),
        jax.ShapeDtypeStruct((t, B_KV_WIDTH), BF16),
    ]
    if emit_f32:
        out_specs += [pl.BlockSpec((tm, B_KV_WIDTH), lambda i: (i, 0))] * 2
        out_shape += [jax.ShapeDtypeStruct((t, B_KV_WIDTH), F32)] * 2
    assert Q0_OFF % qkv_w == 0
    return pl.pallas_call(
        functools.partial(_mm_qkv0_kernel, rope=rope, emit_f32=emit_f32),
        grid=(t // tm,),
        in_specs=in_specs,
        out_specs=out_specs,
        out_shape=out_shape,
        compiler_params=_params("parallel"),
        name="in_proj_qkv",
    )(*args)


ROPE_SUB_COLS = 1024


def _mm_rope_kernel(h_ref, w_ref, cos_ref, sa_ref, sb_ref, o_ref, *, q_tiles, q_scale):
    scale = jnp.where(pl.program_id(1) < q_tiles, q_scale, 1.0)
    cos_t, sa, sb = cos_ref[...], sa_ref[...], sb_ref[...]
    h = h_ref[...]
    for c0 in range(0, w_ref.shape[1], ROPE_SUB_COLS):
        acc = jnp.dot(h, w_ref[:, c0:c0 + ROPE_SUB_COLS], preferred_element_type=F32)
        for c in range(ROPE_SUB_COLS // LANES):
            x = acc[:, c * LANES:(c + 1) * LANES]
            cols = slice(c0 + c * LANES, c0 + (c + 1) * LANES)
            o_ref[:, cols] = (_rope(x, cos_t, sa, sb, C_QK_DIM // 4) * scale).astype(o_ref.dtype)


def _mm_rope(h, w, tables, q_scale, tm=1024, tn=C_WIDTH):
    t, k = h.shape
    nt = DEC_SEQ // tm
    tab = pl.BlockSpec((tm, LANES), lambda i, j: (i % nt, 0))
    return pl.pallas_call(
        functools.partial(_mm_rope_kernel, q_tiles=C_WIDTH // tn, q_scale=q_scale),
        grid=(t // tm, 2 * C_WIDTH // tn),
        in_specs=[
            pl.BlockSpec((tm, k), lambda i, j: (i, 0)),
            pl.BlockSpec((k, tn), lambda i, j: (0, j)),
            tab, tab, tab,
        ],
        out_specs=pl.BlockSpec((tm, tn), lambda i, j: (i, j)),
        out_shape=jax.ShapeDtypeStruct((t, 2 * C_WIDTH), BF16),
        compiler_params=_params("parallel", "parallel"),
        name="in_proj_rope",
    )(h, w, *tables)


KEY_CHUNK = 256
SUBLANES = 8


MAX_KEY_BLOCK = 1024


def _key_blocks(s_new, has_cache):
    size = min(s_new, MAX_KEY_BLOCK)
    blocks = [(r, size) for r in range(0, s_new, size)]
    if has_cache:
        blocks.append((s_new, PAST_LEN))
    return blocks


def _attn_pipeline(units, blocks, s_scr, p_scr):
    n = s_scr.shape[2]
    st = [dict() for _ in units]

    def stage1(u, b):
        r0, size = blocks[b]
        if b == 0:
            st[u]["qq"] = units[u][0]()
        s = lax.dot_general(units[u][1](b), st[u]["qq"], (((1,), (1,)), ((), ())), preferred_element_type=F32)
        s_scr[u % 2, r0:r0 + size, :] = s
        m8 = s.reshape(size // SUBLANES, SUBLANES, n).max(axis=0)
        st[u]["m8"] = m8 if b == 0 else jnp.maximum(st[u]["m8"], m8)
        if b == len(blocks) - 1:
            st[u]["m"] = st[u]["m8"].max(axis=0, keepdims=True)

    def stage2(u, b):
        r0, size = blocks[b]
        for r in range(r0, r0 + size, KEY_CHUNK):
            p = jnp.exp2(s_scr[u % 2, r:r + KEY_CHUNK, :] - st[u]["m"])
            l8 = p.reshape(KEY_CHUNK // SUBLANES, SUBLANES, n).sum(axis=0)
            st[u]["l8"] = l8 if r == 0 else st[u]["l8"] + l8
            p_scr[u % 2, r:r + KEY_CHUNK, :] = p.astype(BF16)

    def stage3(u, b):
        r0, size = blocks[b]
        a = jnp.dot(units[u][2][:, r0:r0 + size], p_scr[u % 2, r0:r0 + size, :], preferred_element_type=F32)
        st[u]["acc"] = a if b == 0 else st[u]["acc"] + a
        if b == len(blocks) - 1:
            l = st[u]["l8"].sum(axis=0, keepdims=True)
            units[u][3](st[u]["acc"] * (1.0 / l))

    for slot in range(len(units) + 2):
        for b in range(len(blocks)):
            if 0 <= slot - 2 < len(units):
                stage3(slot - 2, b)
            if slot < len(units):
                stage1(slot, b)
            if 0 <= slot - 1 < len(units):
                stage2(slot - 1, b)


def _fill_vt(vt_scr, v_ref, vc_ref, n_kv):
    s_new = v_ref.shape[0]
    for j in range(n_kv):
        cols = slice(j * LANES, (j + 1) * LANES)
        vt_scr[j, :, 0:s_new] = v_ref[:, cols].astype(F32).T.astype(BF16)
        if vc_ref is not None:
            vt_scr[j, :, s_new:] = vc_ref[:, cols].astype(F32).T.astype(BF16)


def _attn_scratch(n_kv, s_tot, n):
    return [
        pltpu.VMEM((n_kv, LANES, s_tot), BF16),
        pltpu.VMEM((2, s_tot, n), F32),
        pltpu.VMEM((2, s_tot, n), BF16),
    ]


TQ = 256


def _gqa_kernel(*refs, has_cache):
    q_ref, k_ref, v_ref = refs[:3]
    refs = refs[3:]
    kc_ref = vc_ref = None
    if has_cache:
        (kc_ref, vc_ref), refs = refs[:2], refs[2:]
    g_ref, o_ref, vt_scr, s_scr, p_scr = refs
    blocks = _key_blocks(k_ref.shape[0], has_cache)

    @pl.when(pl.program_id(2) == 0)
    def _():
        _fill_vt(vt_scr, v_ref, vc_ref, 1)

    def key_block(b):
        r0, size = blocks[b]
        if r0 < k_ref.shape[0]:
            return k_ref[r0:r0 + size, :].astype(BF16)
        return kc_ref[...].astype(BF16)

    def make_unit(rows, heads):
        def make_qq():
            return jnp.concatenate([q_ref[rows, h * LANES:(h + 1) * LANES] for h in heads], axis=0)

        def finish(ot):
            for i, h in enumerate(heads):
                cols = slice(h * LANES, (h + 1) * LANES)
                gate = g_ref[rows, cols].astype(F32)
                o_ref[rows, cols] = (ot[:, i * TQ:(i + 1) * TQ].T * _silu(gate)).astype(o_ref.dtype)

        return make_qq, key_block, vt_scr.at[0], finish

    units = [make_unit(slice(r, r + TQ), (h, h + 1))
             for r in range(0, q_ref.shape[0], TQ) for h in range(0, B_GROUP, 2)]
    _attn_pipeline(units, blocks, s_scr, p_scr)


def _gqa(qp, kp, vp, cache, gate, nb, t, tqs):
    nq = t // tqs
    qw = B_GROUP * LANES
    s_tot = t + (PAST_LEN if cache is not None else 0)
    in_specs = [
        pl.BlockSpec((tqs, qw), lambda b, h, i: (b * nq + i, h)),
        pl.BlockSpec((t, LANES), lambda b, h, i: (b, h)),
        pl.BlockSpec((t, LANES), lambda b, h, i: (b, h)),
    ]
    args = [qp, kp, vp]
    if cache is not None:
        in_specs += [pl.BlockSpec((PAST_LEN, LANES), lambda b, h, i: (b, h))] * 2
        args += list(cache)
    in_specs.append(pl.BlockSpec((tqs, qw), lambda b, h, i: (b * nq + i, h)))
    args.append(gate)
    return pl.pallas_call(
        functools.partial(_gqa_kernel, has_cache=cache is not None),
        grid=(nb, B_KV_HEADS, nq),
        in_specs=in_specs,
        out_specs=pl.BlockSpec((tqs, qw), lambda b, h, i: (b * nq + i, h)),
        out_shape=jax.ShapeDtypeStruct((nb * t, B_WIDTH), BF16),
        scratch_shapes=_attn_scratch(1, s_tot, 2 * TQ),
        compiler_params=_params("parallel", "parallel", "arbitrary"),
        name="gqa_attn",
    )(*args)


def _diff_kernel(*refs, has_cache, hb, q_scale):
    q_ref, k_ref, v_ref = refs[:3]
    refs = refs[3:]
    kc_ref = vc_ref = None
    if has_cache:
        (kc_ref, vc_ref), refs = refs[:2], refs[2:]
    g_ref, lam_ref, sg_ref, o_ref, vt_scr, s_scr, p_scr = refs
    blocks = _key_blocks(k_ref.shape[0], has_cache)

    @pl.when(pl.program_id(2) == 0)
    def _():
        _fill_vt(vt_scr, v_ref, vc_ref, hb)

    lp = lam_ref[...]
    lam = (jnp.exp(jnp.sum(lp[0:1] * lp[1:2], axis=-1, keepdims=True))
           - jnp.exp(jnp.sum(lp[2:3] * lp[3:4], axis=-1, keepdims=True)) + LAMBDA_INIT_1)
    lane = lax.broadcasted_iota(jnp.int32, (TQ, LANES), 1)

    def make_unit(rows, j):
        cols = slice(j * LANES, (j + 1) * LANES)

        def make_qq():
            q = q_ref[rows, cols]
            if q_scale is not None:
                q = q.astype(F32) * q_scale
            zero = jnp.zeros_like(q)
            maps = [jnp.where(lane < C_QK_DIM, q, zero), jnp.where(lane >= C_QK_DIM, q, zero)]
            return jnp.concatenate(maps, axis=0).astype(BF16)

        def key_block(b):
            r0, size = blocks[b]
            if r0 < k_ref.shape[0]:
                return k_ref[r0:r0 + size, cols].astype(BF16)
            return kc_ref[:, cols].astype(BF16)

        def finish(ot):
            ot = ot[:, :TQ] - lam * ot[:, TQ:]
            ms = jnp.mean(ot * ot, axis=0, keepdims=True)
            o = (ot * lax.rsqrt(ms + EPS)).T * sg_ref[...] * (1.0 - LAMBDA_INIT_1)
            gate = g_ref[rows, cols].astype(F32)
            o_ref[rows, cols] = (o * _silu(gate)).astype(o_ref.dtype)

        return make_qq, key_block, vt_scr.at[j], finish

    units = [make_unit(slice(r, r + TQ), j) for j in range(hb) for r in range(0, q_ref.shape[0], TQ)]
    _attn_pipeline(units, blocks, s_scr, p_scr)


def _diff(q, k, v, g, cache, lam_p, sub_g, nb, t, tq, hb, q_scale):
    nq = t // tq
    w = hb * LANES
    s_tot = t + (PAST_LEN if cache is not None else 0)
    in_specs = [
        pl.BlockSpec((tq, w), lambda b, h, i: (b * nq + i, q[1] + h)),
        pl.BlockSpec((t, w), lambda b, h, i: (b, k[1] + h)),
        pl.BlockSpec((t, w), lambda b, h, i: (b, v[1] + h)),
    ]
    args = [q[0], k[0], v[0]]
    if cache is not None:
        in_specs += [pl.BlockSpec((PAST_LEN, w), lambda b, h, i: (b, h))] * 2
        args += list(cache)
    in_specs += [
        pl.BlockSpec((tq, w), lambda b, h, i: (b * nq + i, g[1] + h)),
        pl.BlockSpec((4, C_QK_DIM), lambda b, h, i: (0, 0)),
        pl.BlockSpec((1, LANES), lambda b, h, i: (0, 0)),
    ]
    args += [g[0], lam_p, sub_g.reshape(1, LANES)]
    return pl.pallas_call(
        functools.partial(_diff_kernel, has_cache=cache is not None, hb=hb, q_scale=q_scale),
        grid=(nb, C_HEADS // hb, nq),
        in_specs=in_specs,
        out_specs=pl.BlockSpec((tq, w), lambda b, h, i: (b * nq + i, h)),
        out_shape=jax.ShapeDtypeStruct((nb * t, C_WIDTH), BF16),
        scratch_shapes=_attn_scratch(hb, s_tot, 2 * TQ),
        compiler_params=_params("parallel", "parallel", "arbitrary"),
        name="diff_attn",
    )(*args)


def _out_kernel(*refs, n_in, final):
    y_refs = refs[:n_in]
    w_ref, x_ref, gate_ref = refs[n_in:n_in + 3]
    refs = refs[n_in + 3:]
    half = x_ref.shape[0] // 2
    for rows in (slice(0, half), slice(half, 2 * half)):
        acc = None
        k0 = 0
        for y_ref in y_refs:
            kk = y_ref.shape[1]
            a = jnp.dot(y_ref[rows, :], w_ref[k0:k0 + kk, :], preferred_element_type=F32)
            acc = a if acc is None else acc + a
            k0 += kk
        x = x_ref[rows, :] + gate_ref[...] * acc
        xn = x * lax.rsqrt(jnp.mean(x * x, axis=-1, keepdims=True) + EPS)
        if final:
            fg_ref, o_ref = refs
            o_ref[rows, :] = xn * fg_ref[...]
        else:
            g_ref, shift_ref, scale_ref, o_ref, h_ref = refs
            o_ref[rows, :] = x
            h_ref[rows, :] = (xn * g_ref[...] * (1.0 + scale_ref[...]) + shift_ref[...]).astype(h_ref.dtype)


def _out_proj(ys, w, x, mod3, row_of_tile, final_g=None, next_norm=None, tm=512):
    t = x.shape[0]
    final = final_g is not None
    vec = pl.BlockSpec((1, D_MODEL), lambda i: (0, 0))
    mod_row = lambda part: pl.BlockSpec((None, 1, D_MODEL), lambda i: (row_of_tile(i), 0, part))
    in_specs = [pl.BlockSpec((tm, y.shape[1]), lambda i: (i, 0)) for y in ys]
    in_specs += [
        pl.BlockSpec(w.shape, lambda i: (0, 0)),
        pl.BlockSpec((tm, D_MODEL), lambda i: (i, 0)),
        mod_row(2),
    ]
    args = list(ys) + [w, x, mod3]
    row_blk = pl.BlockSpec((tm, D_MODEL), lambda i: (i, 0))
    if final:
        in_specs.append(vec)
        args.append(final_g.reshape(1, D_MODEL))
        out_specs, out_shape = row_blk, jax.ShapeDtypeStruct((t, D_MODEL), F32)
    else:
        next_g, next_mod3 = next_norm
        in_specs += [vec, mod_row(0), mod_row(1)]
        args += [next_g.reshape(1, D_MODEL), next_mod3, next_mod3]
        out_specs = [row_blk, row_blk]
        out_shape = [jax.ShapeDtypeStruct((t, D_MODEL), F32), jax.ShapeDtypeStruct((t, D_MODEL), BF16)]
    return pl.pallas_call(
        functools.partial(_out_kernel, n_in=len(ys), final=final),
        grid=(t // tm,),
        in_specs=in_specs,
        out_specs=out_specs,
        out_shape=out_shape,
        compiler_params=_params("parallel"),
        name="out_proj",
    )(*args)


def kernel(x_prompt, x_sample, cache_k0, cache_v0, cache_k1, cache_v1, c, c_ctx, w_ada0, b_ada0, norm_g0, w_in0, w_s0, b_s0, q_norm_g0, k_norm_g0, w_out0, w_ada1, b_ada1, norm_g1, w_in1, lambda_q1, lambda_k1, lambda_q2, lambda_k2, subln_g1, w_out1, final_g):
    n_ctx, n_smp = x_prompt.shape[0], x_sample.shape[0]

    cond = jnp.concatenate([c_ctx[None], c, jnp.zeros((8 - 1 - n_smp, D_MODEL), F32)], axis=0)
    mod0 = _ada(cond, w_ada0, b_ada0).reshape(8, 1, 3 * D_MODEL)
    mod1 = _ada(cond, w_ada1, b_ada1).reshape(8, 1, 3 * D_MODEL)

    w_in0_b, w_out0_b = w_in0.astype(BF16), w_out0.astype(BF16)
    w_in1_b, w_out1_b = w_in1.astype(BF16), w_out1.astype(BF16)
    ws_b = w_s0.astype(BF16)
    bs_b = jnp.broadcast_to(b_s0[:, :, None], (A_GROUPS, CHUNK, LANES))
    lam_p = jnp.stack([lambda_q1, lambda_k1, lambda_q2, lambda_k2]).astype(F32)
    cos0, sin_a0, sin_b0 = _rope_tables(B_HEAD_DIM // 4)
    tab0 = (cos0, sin_a0 - sin_b0, _rope_partner_matrix(B_HEAD_DIM // 4))
    tab1 = _rope_tables(C_QK_DIM // 4)

    def run(x, nb, t, smp):
        def row_fn(tm):
            if not smp:
                return lambda i: 0
            return lambda i: 1 + (i * tm) // t

        front = (x, norm_g0, mod0, row_fn(512), w_in0_b, q_norm_g0, k_norm_g0)
        if smp:
            h, qp, kp, vp = _norm_mm_qkv0(*front, tab0, False)
            cache = (cache_k0.reshape(nb * PAST_LEN, B_KV_WIDTH), cache_v0.reshape(nb * PAST_LEN, B_KV_WIDTH))
            k0 = v0 = None
        else:
            h, qp, kp, vp, k0, v0 = _norm_mm_qkv0(*front, None, True)
            cache = None
        y_a = _mm_sgate(h, w_in0_b, ws_b, bs_b)
        gate_b = _mm(h, w_in0_b, BF16, 512, (G0_OFF // 512, 1, B_WIDTH // 512))
        y_b = _gqa(qp, kp, vp, cache, gate_b, nb, t, 1024 if smp else SEQ)
        x1, h = _out_proj([y_a, y_b], w_out0_b, x, mod0, row_fn(512), next_norm=(norm_g1, mod1))

        if smp:
            qk = _mm_rope(h, w_in1_b, tab1, (C_QK_DIM ** -0.5) * LOG2E)
            vg = _mm(h, w_in1_b, BF16, 1024, (2 * C_WIDTH // 1024, 1, 2 * C_WIDTH // 1024))
            cache = (cache_k1.reshape(nb * PAST_LEN, C_WIDTH), cache_v1.reshape(nb * PAST_LEN, C_WIDTH))
            hb = 1
            second = C_WIDTH // (hb * LANES)
            y_c = _diff((qk, 0), (qk, second), (vg, 0), (vg, second), cache, lam_p, subln_g1,
                        nb, t, 2048, hb, None)
            k1 = v1 = None
        else:
            qg = _mm(h, w_in1_b, BF16, C_WIDTH, (0, 3, 2))
            kb, k1 = _mm_kt(h, w_in1_b, 1, nb)
            v1 = _mm(h, w_in1_b, F32, C_WIDTH, (2, 1, 1), tm=512)
            hb = 8
            y_c = _diff((qg, 0), (kb, 0), (v1, 0), (qg, C_WIDTH // (hb * LANES)), None, lam_p, subln_g1,
                        nb, t, SEQ, hb, (C_QK_DIM ** -0.5) * LOG2E)
        y = _out_proj([y_c], w_out1_b, x1, mod1, row_fn(512), final_g)
        return y, k0, v0, k1, v1

    y_p, k0, v0, k1, v1 = run(x_prompt.reshape(n_ctx * SEQ, D_MODEL), n_ctx, SEQ, False)
    y_s, _, _, _, _ = run(x_sample.reshape(n_smp * DEC_SEQ, D_MODEL), n_smp, DEC_SEQ, True)

    return (
        y_p.reshape(n_ctx, SEQ, D_MODEL),
        y_s.reshape(n_smp, DEC_SEQ, D_MODEL),
        k0.reshape(n_ctx, SEQ, B_KV_HEADS, B_HEAD_DIM),
        v0.reshape(n_ctx, SEQ, B_KV_HEADS, B_HEAD_DIM),
        k1.reshape(n_ctx, C_HEADS, 2, C_QK_DIM, SEQ).transpose(0, 4, 1, 2, 3),
        v1.reshape(n_ctx, SEQ, C_HEADS, C_V_DIM),
    )
```

```python
import functools
import math

import jax
import jax.numpy as jnp
from jax import lax
from jax.experimental import pallas as pl
from jax.experimental.pallas import tpu as pltpu

F32 = jnp.float32
BF16 = jnp.bfloat16

D_MODEL = 2048
SEQ = 256
DEC_SEQ = 4096
PAST_LEN = 256
GRID_W = 64
CHUNK = 128
ROPE_THETA = 10000.0
EPS = 1e-6

A_GROUPS = 8
A_WIDTH = 1024
B_HEADS = 8
B_KV_HEADS = 2
B_GROUP = B_HEADS // B_KV_HEADS
B_HEAD_DIM = 128
B_WIDTH = 1024
B_KV_WIDTH = 256
IN0_WIDTH = 3 * A_WIDTH + 2 * B_WIDTH + 2 * B_KV_WIDTH
C_HEADS = 16
C_QK_DIM = 64
C_V_DIM = 128
C_WIDTH = 2048
IN1_WIDTH = 4 * C_WIDTH
LAMBDA_INIT_1 = 0.8 - 0.6 * math.exp(-0.3 * 1)

LANES = 128
LOG2E = math.log2(math.e)
VMEM_LIMIT = 56 * 1024 * 1024

Q0_OFF = 3 * A_WIDTH
K0_OFF = Q0_OFF + B_WIDTH
V0_OFF = K0_OFF + B_KV_WIDTH
G0_OFF = V0_OFF + B_KV_WIDTH


def _params(*sem):
    return pltpu.CompilerParams(dimension_semantics=sem, vmem_limit_bytes=VMEM_LIMIT)


def _silu(x):
    return x * jax.nn.sigmoid(x)


def _ada_kernel(c_ref, w_ref, b_ref, o_ref):
    s = _silu(c_ref[...]).astype(BF16)
    o_ref[...] = jnp.dot(s, w_ref[...].astype(BF16), preferred_element_type=F32) + b_ref[...]


def _ada(cond, w_ada, b_ada):
    bn = 512
    n = w_ada.shape[1]
    return pl.pallas_call(
        _ada_kernel,
        grid=(n // bn,),
        in_specs=[
            pl.BlockSpec((8, D_MODEL), lambda j: (0, 0)),
            pl.BlockSpec((D_MODEL, bn), lambda j: (0, j)),
            pl.BlockSpec((1, bn), lambda j: (0, j)),
        ],
        out_specs=pl.BlockSpec((8, bn), lambda j: (0, j)),
        out_shape=jax.ShapeDtypeStruct((8, n), F32),
        compiler_params=_params("parallel"),
        name="ada",
    )(cond, w_ada, b_ada.reshape(1, n))


def _mm_kernel(h_ref, w_ref, o_ref):
    o_ref[...] = jnp.dot(h_ref[...], w_ref[...], preferred_element_type=F32).astype(o_ref.dtype)


def _mm(h, w, out_dtype, tn, col_blocks=None, tm=1024):
    t, k = h.shape
    first, stride, count = col_blocks if col_blocks is not None else (0, 1, w.shape[1] // tn)
    assert t % tm == 0 and w.shape[1] % tn == 0
    return pl.pallas_call(
        _mm_kernel,
        grid=(t // tm, count),
        in_specs=[
            pl.BlockSpec((tm, k), lambda i, j: (i, 0)),
            pl.BlockSpec((k, tn), lambda i, j: (0, first + stride * j)),
        ],
        out_specs=pl.BlockSpec((tm, tn), lambda i, j: (i, j)),
        out_shape=jax.ShapeDtypeStruct((t, count * tn), out_dtype),
        compiler_params=_params("parallel", "parallel"),
        name="in_proj",
    )(h, w)


def _mm_kt_kernel(h_ref, w_ref, kb_ref, kt_ref):
    acc = jnp.dot(h_ref[...], w_ref[...], preferred_element_type=F32)
    kb_ref[...] = acc.astype(kb_ref.dtype)
    for b in range(kt_ref.shape[0]):
        kt_ref[b] = acc[b * SEQ:(b + 1) * SEQ, :].T


def _mm_kt(h, w, col_block, n_batch, tb=2):
    t, k = h.shape
    tm = tb * SEQ
    return pl.pallas_call(
        _mm_kt_kernel,
        grid=(t // tm,),
        in_specs=[
            pl.BlockSpec((tm, k), lambda i: (i, 0)),
            pl.BlockSpec((k, C_WIDTH), lambda i: (0, col_block)),
        ],
        out_specs=[
            pl.BlockSpec((tm, C_WIDTH), lambda i: (i, 0)),
            pl.BlockSpec((tb, C_WIDTH, SEQ), lambda i: (i, 0, 0)),
        ],
        out_shape=[
            jax.ShapeDtypeStruct((t, C_WIDTH), BF16),
            jax.ShapeDtypeStruct((n_batch, C_WIDTH, SEQ), F32),
        ],
        compiler_params=_params("parallel"),
        name="in_proj_kt",
    )(h, w)


def _mm_sgate_kernel(h_ref, w_ref, ws_ref, bs_ref, o_ref):
    z = jnp.dot(h_ref[...], w_ref[...], preferred_element_type=F32)
    v = z[:, A_WIDTH:2 * A_WIDTH]
    mu = jnp.mean(v, axis=-1, keepdims=True)
    vc = v - mu
    var = jnp.mean(vc * vc, axis=-1, keepdims=True)
    vn = (vc * lax.rsqrt(var + EPS)).astype(BF16)
    for ch in range(z.shape[0] // CHUNK):
        rows = slice(ch * CHUNK, (ch + 1) * CHUNK)
        for g in range(A_GROUPS):
            cols = slice(g * LANES, (g + 1) * LANES)
            s = jnp.dot(ws_ref[g], vn[rows, cols], preferred_element_type=F32) + bs_ref[g]
            u = z[rows, g * LANES:(g + 1) * LANES]
            gate = z[rows, 2 * A_WIDTH + g * LANES:2 * A_WIDTH + (g + 1) * LANES]
            o_ref[rows, cols] = (u * s * _silu(gate)).astype(o_ref.dtype)


def _mm_sgate(h, w, ws_b, bs_b, tm=512):
    t, k = h.shape
    full = pl.BlockSpec((A_GROUPS, CHUNK, CHUNK), lambda i: (0, 0, 0))
    return pl.pallas_call(
        _mm_sgate_kernel,
        grid=(t // tm,),
        in_specs=[
            pl.BlockSpec((tm, k), lambda i: (i, 0)),
            pl.BlockSpec((k, 3 * A_WIDTH), lambda i: (0, 0)),
            full, full,
        ],
        out_specs=pl.BlockSpec((tm, A_WIDTH), lambda i: (i, 0)),
        out_shape=jax.ShapeDtypeStruct((t, A_WIDTH), BF16),
        compiler_params=_params("parallel"),
        name="in_proj_sgate",
    )(h, w, ws_b, bs_b)


def _rope_tables(half):
    rows = DEC_SEQ // GRID_W
    row_pos = jnp.repeat(jnp.arange(rows, dtype=F32), GRID_W)
    col_pos = jnp.tile(jnp.arange(GRID_W, dtype=F32), rows)
    freqs = ROPE_THETA ** (-jnp.arange(half, dtype=F32) / half)
    ang_r = row_pos[:, None] * freqs[None, :]
    ang_c = col_pos[:, None] * freqs[None, :]
    cr, sr, cc, sc = jnp.cos(ang_r), jnp.sin(ang_r), jnp.cos(ang_c), jnp.sin(ang_c)
    z = jnp.zeros_like(sr)
    reps = LANES // (4 * half)
    cos_t = jnp.tile(jnp.concatenate([cr, cr, cc, cc], axis=-1), (1, reps))
    sin_a = jnp.tile(jnp.concatenate([z, sr, z, sc], axis=-1), (1, reps))
    sin_b = jnp.tile(jnp.concatenate([-sr, z, -sc, z], axis=-1), (1, reps))
    return cos_t, sin_a, sin_b


def _rope_partner_matrix(half):
    lane = jnp.arange(LANES)
    first = (lane % (2 * half)) < half
    src = jnp.where(first, lane + half, lane - half)
    sign = jnp.where(first, -1.0, 1.0)
    return jnp.zeros((LANES, LANES), F32).at[src, lane].set(sign).astype(BF16)


def _rope(x, cos_t, sin_a, sin_b, half):
    return x * cos_t + pltpu.roll(x, half, 1) * sin_a + pltpu.roll(x, LANES - half, 1) * sin_b


QKV0_SUB_ROWS = 256


def _mm_qkv0_kernel(*refs, rope, emit_f32):
    x_ref, ng_ref, shift_ref, scale_ref, w_ref, qg_ref, kg_ref = refs[:7]
    refs = refs[7:]
    if rope:
        cos_ref, sin_ref, perm_ref = refs[:3]
        refs = refs[3:]
    h_ref, qo_ref, ko_ref, vo_ref = refs[:4]
    if emit_f32:
        kf_ref, vf_ref = refs[4:6]

    def norm(x, g):
        ms = jnp.mean(x * x, axis=-1, keepdims=True)
        return x * lax.rsqrt(ms + EPS) * g

    def rot(x, rows):
        if not rope:
            return x
        partner = jnp.dot(x.astype(BF16), perm_ref[...], preferred_element_type=F32)
        return x * cos_ref[rows, :] + partner * sin_ref[rows, :]

    qscale = (B_HEAD_DIM ** -0.5) * LOG2E
    for r0 in range(0, x_ref.shape[0], QKV0_SUB_ROWS):
        rows = slice(r0, r0 + QKV0_SUB_ROWS)
        hs = (norm(x_ref[rows, :], ng_ref[...]) * (1.0 + scale_ref[...]) + shift_ref[...]).astype(BF16)
        h_ref[rows, :] = hs
        z = jnp.dot(hs, w_ref[...], preferred_element_type=F32)
        for h in range(B_HEADS):
            cols = slice(h * LANES, (h + 1) * LANES)
            qn = norm(z[:, cols], qg_ref[...])
            qo_ref[rows, cols] = (rot(qn, rows) * qscale).astype(qo_ref.dtype)
        for h in range(B_KV_HEADS):
            cols = slice(h * LANES, (h + 1) * LANES)
            kn = norm(z[:, B_WIDTH + h * LANES:B_WIDTH + (h + 1) * LANES], kg_ref[...])
            ko_ref[rows, cols] = rot(kn, rows).astype(ko_ref.dtype)
            if emit_f32:
                kf_ref[rows, cols] = kn
        v = z[:, B_WIDTH + B_KV_WIDTH:]
        vo_ref[rows, :] = v.astype(vo_ref.dtype)
        if emit_f32:
            vf_ref[rows, :] = v


def _norm_mm_qkv0(x, norm_g, mod3, row_of_tile, w, q_g, k_g, tables, emit_f32, tm=512):
    t, kdim = x.shape
    rope = tables is not None
    qkv_w = B_WIDTH + 2 * B_KV_WIDTH
    in_specs = [
        pl.BlockSpec((tm, kdim), lambda i: (i, 0)),
        pl.BlockSpec((1, kdim), lambda i: (0, 0)),
        pl.BlockSpec((None, 1, kdim), lambda i: (row_of_tile(i), 0, 0)),
        pl.BlockSpec((None, 1, kdim), lambda i: (row_of_tile(i), 0, 1)),
        pl.BlockSpec((kdim, qkv_w), lambda i: (0, Q0_OFF // qkv_w)),
        pl.BlockSpec((1, LANES), lambda i: (0, 0)),
        pl.BlockSpec((1, LANES), lambda i: (0, 0)),
    ]
    args = [x, norm_g.reshape(1, kdim), mod3, mod3, w, q_g.reshape(1, LANES), k_g.reshape(1, LANES)]
    if rope:
        nt = DEC_SEQ // tm
        in_specs += [pl.BlockSpec((tm, LANES), lambda i: (i % nt, 0))] * 2
        in_specs.append(pl.BlockSpec((LANES, LANES), lambda i: (0, 0)))
        args += list(tables)
    out_specs = [
        pl.BlockSpec((tm, kdim), lambda i: (i, 0)),
        pl.BlockSpec((tm, B_WIDTH), lambda i: (i, 0)),
        pl.BlockSpec((tm, B_KV_WIDTH), lambda i: (i, 0)),
        pl.BlockSpec((tm, B_KV_WIDTH), lambda i: (i, 0)),
    ]
    out_shape = [
        jax.ShapeDtypeStruct((t, kdim), BF16),
        jax.ShapeDtypeStruct((t, B_WIDTH), BF16),
        jax.ShapeDtypeStruct((t, B_KV_WIDTH), BF16),
        jax.ShapeDtypeStruct((t, B_KV_WIDTH), BF16),
    ]
    if emit_f32:
        out_specs += [pl.BlockSpec((tm, B_KV_WIDTH), lambda i: (i, 0))] * 2
        out_shape += [jax.ShapeDtypeStruct((t, B_KV_WIDTH), F32)] * 2
    assert Q0_OFF % qkv_w == 0
    return pl.pallas_call(
        functools.partial(_mm_qkv0_kernel, rope=rope, emit_f32=emit_f32),
        grid=(t // tm,),
        in_specs=in_specs,
        out_specs=out_specs,
        out_shape=out_shape,
        compiler_params=_params("parallel"),
        name="in_proj_qkv",
    )(*args)


ROPE_SUB_COLS = 1024


def _mm_rope_kernel(h_ref, w_ref, cos_ref, sa_ref, sb_ref, o_ref, *, q_tiles, q_scale):
    scale = jnp.where(pl.program_id(1) < q_tiles, q_scale, 1.0)
    cos_t, sa, sb = cos_ref[...], sa_ref[...], sb_ref[...]
    h = h_ref[...]
    for c0 in range(0, w_ref.shape[1], ROPE_SUB_COLS):
        acc = jnp.dot(h, w_ref[:, c0:c0 + ROPE_SUB_COLS], preferred_element_type=F32)
        for c in range(ROPE_SUB_COLS // LANES):
            x = acc[:, c * LANES:(c + 1) * LANES]
            cols = slice(c0 + c * LANES, c0 + (c + 1) * LANES)
            o_ref[:, cols] = (_rope(x, cos_t, sa, sb, C_QK_DIM // 4) * scale).astype(o_ref.dtype)


def _mm_rope(h, w, tables, q_scale, tm=1024, tn=C_WIDTH):
    t, k = h.shape
    nt = DEC_SEQ // tm
    tab = pl.BlockSpec((tm, LANES), lambda i, j: (i % nt, 0))
    return pl.pallas_call(
        functools.partial(_mm_rope_kernel, q_tiles=C_WIDTH // tn, q_scale=q_scale),
        grid=(t // tm, 2 * C_WIDTH // tn),
        in_specs=[
            pl.BlockSpec((tm, k), lambda i, j: (i, 0)),
            pl.BlockSpec((k, tn), lambda i, j: (0, j)),
            tab, tab, tab,
        ],
        out_specs=pl.BlockSpec((tm, tn), lambda i, j: (i, j)),
        out_shape=jax.ShapeDtypeStruct((t, 2 * C_WIDTH), BF16),
        compiler_params=_params("parallel", "parallel"),
        name="in_proj_rope",
    )(h, w, *tables)


KEY_CHUNK = 256
SUBLANES = 8


MAX_KEY_BLOCK = 1024


def _key_blocks(s_new, has_cache):
    size = min(s_new, MAX_KEY_BLOCK)
    blocks = [(r, size) for r in range(0, s_new, size)]
    if has_cache:
        blocks.append((s_new, PAST_LEN))
    return blocks


def _attn_pipeline(units, blocks, s_scr, p_scr):
    n = s_scr.shape[2]
    st = [dict() for _ in units]

    def stage1(u, b):
        r0, size = blocks[b]
        if b == 0:
            st[u]["qq"] = units[u][0]()
        s = lax.dot_general(units[u][1](b), st[u]["qq"], (((1,), (1,)), ((), ())), preferred_element_type=F32)
        s_scr[u % 2, r0:r0 + size, :] = s
        m8 = s.reshape(size // SUBLANES, SUBLANES, n).max(axis=0)
        st[u]["m8"] = m8 if b == 0 else jnp.maximum(st[u]["m8"], m8)
        if b == len(blocks) - 1:
            st[u]["m"] = st[u]["m8"].max(axis=0, keepdims=True)

    def stage2(u, b):
        r0, size = blocks[b]
        for r in range(r0, r0 + size, KEY_CHUNK):
            p = jnp.exp2(s_scr[u % 2, r:r + KEY_CHUNK, :] - st[u]["m"])
            l8 = p.reshape(KEY_CHUNK // SUBLANES, SUBLANES, n).sum(axis=0)
            st[u]["l8"] = l8 if r == 0 else st[u]["l8"] + l8
            p_scr[u % 2, r:r + KEY_CHUNK, :] = p.astype(BF16)

    def stage3(u, b):
        r0, size = blocks[b]
        a = jnp.dot(units[u][2][:, r0:r0 + size], p_scr[u % 2, r0:r0 + size, :], preferred_element_type=F32)
        st[u]["acc"] = a if b == 0 else st[u]["acc"] + a
        if b == len(blocks) - 1:
            l = st[u]["l8"].sum(axis=0, keepdims=True)
            units[u][3](st[u]["acc"] * (1.0 / l))

    for slot in range(len(units) + 2):
        for b in range(len(blocks)):
            if 0 <= slot - 2 < len(units):
                stage3(slot - 2, b)
            if slot < len(units):
                stage1(slot, b)
            if 0 <= slot - 1 < len(units):
                stage2(slot - 1, b)


def _fill_vt(vt_scr, v_ref, vc_ref, n_kv):
    s_new = v_ref.shape[0]
    for j in range(n_kv):
        cols = slice(j * LANES, (j + 1) * LANES)
        vt_scr[j, :, 0:s_new] = v_ref[:, cols].astype(F32).T.astype(BF16)
        if vc_ref is not None:
            vt_scr[j, :, s_new:] = vc_ref[:, cols].astype(F32).T.astype(BF16)


def _attn_scratch(n_kv, s_tot, n):
    return [
        pltpu.VMEM((n_kv, LANES, s_tot), BF16),
        pltpu.VMEM((2, s_tot, n), F32),
        pltpu.VMEM((2, s_tot, n), BF16),
    ]


TQ = 256


def _gqa_kernel(*refs, has_cache):
    q_ref, k_ref, v_ref = refs[:3]
    refs = refs[3:]
    kc_ref = vc_ref = None
    if has_cache:
        (kc_ref, vc_ref), refs = refs[:2], refs[2:]
    g_ref, o_ref, vt_scr, s_scr, p_scr = refs
    blocks = _key_blocks(k_ref.shape[0], has_cache)

    @pl.when(pl.program_id(2) == 0)
    def _():
        _fill_vt(vt_scr, v_ref, vc_ref, 1)

    def key_block(b):
        r0, size = blocks[b]
        if r0 < k_ref.shape[0]:
            return k_ref[r0:r0 + size, :].astype(BF16)
        return kc_ref[...].astype(BF16)

    def make_unit(rows, heads):
        def make_qq():
            return jnp.concatenate([q_ref[rows, h * LANES:(h + 1) * LANES] for h in heads], axis=0)

        def finish(ot):
            for i, h in enumerate(heads):
                cols = slice(h * LANES, (h + 1) * LANES)
                gate = g_ref[rows, cols].astype(F32)
                o_ref[rows, cols] = (ot[:, i * TQ:(i + 1) * TQ].T * _silu(gate)).astype(o_ref.dtype)

        return make_qq, key_block, vt_scr.at[0], finish

    units = [make_unit(slice(r, r + TQ), (h, h + 1))
             for r in range(0, q_ref.shape[0], TQ) for h in range(0, B_GROUP, 2)]
    _attn_pipeline(units, blocks, s_scr, p_scr)


def _gqa(qp, kp, vp, cache, gate, nb, t, tqs):
    nq = t // tqs
    qw = B_GROUP * LANES
    s_tot = t + (PAST_LEN if cache is not None else 0)
    in_specs = [
        pl.BlockSpec((tqs, qw), lambda b, h, i: (b * nq + i, h)),
        pl.BlockSpec((t, LANES), lambda b, h, i: (b, h)),
        pl.BlockSpec((t, LANES), lambda b, h, i: (b, h)),
    ]
    args = [qp, kp, vp]
    if cache is not None:
        in_specs += [pl.BlockSpec((PAST_LEN, LANES), lambda b, h, i: (b, h))] * 2
        args += list(cache)
    in_specs.append(pl.BlockSpec((tqs, qw), lambda b, h, i: (b * nq + i, h)))
    args.append(gate)
    return pl.pallas_call(
        functools.partial(_gqa_kernel, has_cache=cache is not None),
        grid=(nb, B_KV_HEADS, nq),
        in_specs=in_specs,
        out_specs=pl.BlockSpec((tqs, qw), lambda b, h, i: (b * nq + i, h)),
        out_shape=jax.ShapeDtypeStruct((nb * t, B_WIDTH), BF16),
        scratch_shapes=_attn_scratch(1, s_tot, 2 * TQ),
        compiler_params=_params("parallel", "parallel", "arbitrary"),
        name="gqa_attn",
    )(*args)


def _diff_kernel(*refs, has_cache, hb, q_scale):
    q_ref, k_ref, v_ref = refs[:3]
    refs = refs[3:]
    kc_ref = vc_ref = None
    if has_cache:
        (kc_ref, vc_ref), refs = refs[:2], refs[2:]
    g_ref, lam_ref, sg_ref, o_ref, vt_scr, s_scr, p_scr = refs
    blocks = _key_blocks(k_ref.shape[0], has_cache)

    @pl.when(pl.program_id(2) == 0)
    def _():
        _fill_vt(vt_scr, v_ref, vc_ref, hb)

    lp = lam_ref[...]
    lam = (jnp.exp(jnp.sum(lp[0:1] * lp[1:2], axis=-1, keepdims=True))
           - jnp.exp(jnp.sum(lp[2:3] * lp[3:4], axis=-1, keepdims=True)) + LAMBDA_INIT_1)
    lane = lax.broadcasted_iota(jnp.int32, (TQ, LANES), 1)

    def make_unit(rows, j):
        cols = slice(j * LANES, (j + 1) * LANES)

        def make_qq():
            q = q_ref[rows, cols]
            if q_scale is not None:
                q = q.astype(F32) * q_scale
            zero = jnp.zeros_like(q)
            maps = [jnp.where(lane < C_QK_DIM, q, zero), jnp.where(lane >= C_QK_DIM, q, zero)]
            return jnp.concatenate(maps, axis=0).astype(BF16)

        def key_block(b):
            r0, size = blocks[b]
            if r0 < k_ref.shape[0]:
                return k_ref[r0:r0 + size, cols].astype(BF16)
            return kc_ref[:, cols].astype(BF16)

        def finish(ot):
            ot = ot[:, :TQ] - lam * ot[:, TQ:]
            ms = jnp.mean(ot * ot, axis=0, keepdims=True)
            o = (ot * lax.rsqrt(ms + EPS)).T * sg_ref[...] * (1.0 - LAMBDA_INIT_1)
            gate = g_ref[rows, cols].astype(F32)
            o_ref[rows, cols] = (o * _silu(gate)).astype(o_ref.dtype)

        return make_qq, key_block, vt_scr.at[j], finish

    units = [make_unit(slice(r, r + TQ), j) for j in range(hb) for r in range(0, q_ref.shape[0], TQ)]
    _attn_pipeline(units, blocks, s_scr, p_scr)


def _diff(q, k, v, g, cache, lam_p, sub_g, nb, t, tq, hb, q_scale):
    nq = t // tq
    w = hb * LANES
    s_tot = t + (PAST_LEN if cache is not None else 0)
    in_specs = [
        pl.BlockSpec((tq, w), lambda b, h, i: (b * nq + i, q[1] + h)),
        pl.BlockSpec((t, w), lambda b, h, i: (b, k[1] + h)),
        pl.BlockSpec((t, w), lambda b, h, i: (b, v[1] + h)),
    ]
    args = [q[0], k[0], v[0]]
    if cache is not None:
        in_specs += [pl.BlockSpec((PAST_LEN, w), lambda b, h, i: (b, h))] * 2
        args += list(cache)
    in_specs += [
        pl.BlockSpec((tq, w), lambda b, h, i: (b * nq + i, g[1] + h)),
        pl.BlockSpec((4, C_QK_DIM), lambda b, h, i: (0, 0)),
        pl.BlockSpec((1, LANES), lambda b, h, i: (0, 0)),
    ]
    args += [g[0], lam_p, sub_g.reshape(1, LANES)]
    return pl.pallas_call(
        functools.partial(_diff_kernel, has_cache=cache is not None, hb=hb, q_scale=q_scale),
        grid=(nb, C_HEADS // hb, nq),
        in_specs=in_specs,
        out_specs=pl.BlockSpec((tq, w), lambda b, h, i: (b * nq + i, h)),
        out_shape=jax.ShapeDtypeStruct((nb * t, C_WIDTH), BF16),
        scratch_shapes=_attn_scratch(hb, s_tot, 2 * TQ),
        compiler_params=_params("parallel", "parallel", "arbitrary"),
        name="diff_attn",
    )(*args)


def _out_kernel(*refs, n_in, final):
    y_refs = refs[:n_in]
    w_ref, x_ref, gate_ref = refs[n_in:n_in + 3]
    refs = refs[n_in + 3:]
    half = x_ref.shape[0] // 2
    for rows in (slice(0, half), slice(half, 2 * half)):
        acc = None
        k0 = 0
        for y_ref in y_refs:
            kk = y_ref.shape[1]
            a = jnp.dot(y_ref[rows, :], w_ref[k0:k0 + kk, :], preferred_element_type=F32)
            acc = a if acc is None else acc + a
            k0 += kk
        x = x_ref[rows, :] + gate_ref[...] * acc
        xn = x * lax.rsqrt(jnp.mean(x * x, axis=-1, keepdims=True) + EPS)
        if final:
            fg_ref, o_ref = refs
            o_ref[rows, :] = xn * fg_ref[...]
        else:
            g_ref, shift_ref, scale_ref, o_ref, h_ref = refs
            o_ref[rows, :] = x
            h_ref[rows, :] = (xn * g_ref[...] * (1.0 + scale_ref[...]) + shift_ref[...]).astype(h_ref.dtype)


def _out_proj(ys, w, x, mod3, row_of_tile, final_g=None, next_norm=None, tm=512):
    t = x.shape[0]
    final = final_g is not None
    vec = pl.BlockSpec((1, D_MODEL), lambda i: (0, 0))
    mod_row = lambda part: pl.BlockSpec((None, 1, D_MODEL), lambda i: (row_of_tile(i), 0, part))
    in_specs = [pl.BlockSpec((tm, y.shape[1]), lambda i: (i, 0)) for y in ys]
    in_specs += [
        pl.BlockSpec(w.shape, lambda i: (0, 0)),
        pl.BlockSpec((tm, D_MODEL), lambda i: (i, 0)),
        mod_row(2),
    ]
    args = list(ys) + [w, x, mod3]
    row_blk = pl.BlockSpec((tm, D_MODEL), lambda i: (i, 0))
    if final:
        in_specs.append(vec)
        args.append(final_g.reshape(1, D_MODEL))
        out_specs, out_shape = row_blk, jax.ShapeDtypeStruct((t, D_MODEL), F32)
    else:
        next_g, next_mod3 = next_norm
        in_specs += [vec, mod_row(0), mod_row(1)]
        args += [next_g.reshape(1, D_MODEL), next_mod3, next_mod3]
        out_specs = [row_blk, row_blk]
        out_shape = [jax.ShapeDtypeStruct((t, D_MODEL), F32), jax.ShapeDtypeStruct((t, D_MODEL), BF16)]
    return pl.pallas_call(
        functools.partial(_out_kernel, n_in=len(ys), final=final),
        grid=(t // tm,),
        in_specs=in_specs,
        out_specs=out_specs,
        out_shape=out_shape,
        compiler_params=_params("parallel"),
        name="out_proj",
    )(*args)


def kernel(x_prompt, x_sample, cache_k0, cache_v0, cache_k1, cache_v1, c, c_ctx, w_ada0, b_ada0, norm_g0, w_in0, w_s0, b_s0, q_norm_g0, k_norm_g0, w_out0, w_ada1, b_ada1, norm_g1, w_in1, lambda_q1, lambda_k1, lambda_q2, lambda_k2, subln_g1, w_out1, final_g):
    n_ctx, n_smp = x_prompt.shape[0], x_sample.shape[0]

    cond = jnp.concatenate([c_ctx[None], c, jnp.zeros((8 - 1 - n_smp, D_MODEL), F32)], axis=0)
    mod0 = _ada(cond, w_ada0, b_ada0).reshape(8, 1, 3 * D_MODEL)
    mod1 = _ada(cond, w_ada1, b_ada1).reshape(8, 1, 3 * D_MODEL)

    w_in0_b, w_out0_b = w_in0.astype(BF16), w_out0.astype(BF16)
    w_in1_b, w_out1_b = w_in1.astype(BF16), w_out1.astype(BF16)
    ws_b = w_s0.astype(BF16)
    bs_b = jnp.broadcast_to(b_s0[:, :, None], (A_GROUPS, CHUNK, LANES))
    lam_p = jnp.stack([lambda_q1, lambda_k1, lambda_q2, lambda_k2]).astype(F32)
    cos0, sin_a0, sin_b0 = _rope_tables(B_HEAD_DIM // 4)
    tab0 = (cos0, sin_a0 - sin_b0, _rope_partner_matrix(B_HEAD_DIM // 4))
    tab1 = _rope_tables(C_QK_DIM // 4)

    def run(x, nb, t, smp):
        def row_fn(tm):
            if not smp:
                return lambda i: 0
            return lambda i: 1 + (i * tm) // t

        front = (x, norm_g0, mod0, row_fn(512), w_in0_b, q_norm_g0, k_norm_g0)
        if smp:
            h, qp, kp, vp = _norm_mm_qkv0(*front, tab0, False)
            cache = (cache_k0.reshape(nb * PAST_LEN, B_KV_WIDTH), cache_v0.reshape(nb * PAST_LEN, B_KV_WIDTH))
            k0 = v0 = None
        else:
            h, qp, kp, vp, k0, v0 = _norm_mm_qkv0(*front, None, True)
            cache = None
        y_a = _mm_sgate(h, w_in0_b, ws_b, bs_b)
        gate_b = _mm(h, w_in0_b, BF16, 512, (G0_OFF // 512, 1, B_WIDTH // 512))
        y_b = _gqa(qp, kp, vp, cache, gate_b, nb, t, 2048 if smp else SEQ)
        x1, h = _out_proj([y_a, y_b], w_out0_b, x, mod0, row_fn(512), next_norm=(norm_g1, mod1))

        if smp:
            qk = _mm_rope(h, w_in1_b, tab1, (C_QK_DIM ** -0.5) * LOG2E)
            vg = _mm(h, w_in1_b, BF16, 1024, (2 * C_WIDTH // 1024, 1, 2 * C_WIDTH // 1024))
            cache = (cache_k1.reshape(nb * PAST_LEN, C_WIDTH), cache_v1.reshape(nb * PAST_LEN, C_WIDTH))
            hb = 1
            second = C_WIDTH // (hb * LANES)
            y_c = _diff((qk, 0), (qk, second), (vg, 0), (vg, second), cache, lam_p, subln_g1,
                        nb, t, 4096, hb, None)
            k1 = v1 = None
        else:
            qg = _mm(h, w_in1_b, BF16, C_WIDTH, (0, 3, 2))
            kb, k1 = _mm_kt(h, w_in1_b, 1, nb)
            v1 = _mm(h, w_in1_b, F32, C_WIDTH, (2, 1, 1), tm=512)
            hb = 8
            y_c = _diff((qg, 0), (kb, 0), (v1, 0), (qg, C_WIDTH // (hb * LANES)), None, lam_p, subln_g1,
                        nb, t, SEQ, hb, (C_QK_DIM ** -0.5) * LOG2E)
        y = _out_proj([y_c], w_out1_b, x1, mod1, row_fn(512), final_g)
        return y, k0, v0, k1, v1

    y_p, k0, v0, k1, v1 = run(x_prompt.reshape(n_ctx * SEQ, D_MODEL), n_ctx, SEQ, False)
    y_s, _, _, _, _ = run(x_sample.reshape(n_smp * DEC_SEQ, D_MODEL), n_smp, DEC_SEQ, True)

    return (
        y_p.reshape(n_ctx, SEQ, D_MODEL),
        y_s.reshape(n_smp, DEC_SEQ, D_MODEL),
        k0.reshape(n_ctx, SEQ, B_KV_HEADS, B_HEAD_DIM),
        v0.reshape(n_ctx, SEQ, B_KV_HEADS, B_HEAD_DIM),
        k1.reshape(n_ctx, C_HEADS, 2, C_QK_DIM, SEQ).transpose(0, 4, 1, 2, 3),
        v1.reshape(n_ctx, SEQ, C_HEADS, C_V_DIM),
    )
```

```python
import functools
import math

import jax
import jax.numpy as jnp
from jax import lax
from jax.experimental import pallas as pl
from jax.experimental.pallas import tpu as pltpu

F32 = jnp.float32
BF16 = jnp.bfloat16

D_MODEL = 2048
SEQ = 256
DEC_SEQ = 4096
PAST_LEN = 256
GRID_W = 64
CHUNK = 128
ROPE_THETA = 10000.0
EPS = 1e-6

A_GROUPS = 8
A_WIDTH = 1024
B_HEADS = 8
B_KV_HEADS = 2
B_GROUP = B_HEADS // B_KV_HEADS
B_HEAD_DIM = 128
B_WIDTH = 1024
B_KV_WIDTH = 256
IN0_WIDTH = 3 * A_WIDTH + 2 * B_WIDTH + 2 * B_KV_WIDTH
C_HEADS = 16
C_QK_DIM = 64
C_V_DIM = 128
C_WIDTH = 2048
IN1_WIDTH = 4 * C_WIDTH
LAMBDA_INIT_1 = 0.8 - 0.6 * math.exp(-0.3 * 1)

LANES = 128
LOG2E = math.log2(math.e)
VMEM_LIMIT = 56 * 1024 * 1024

Q0_OFF = 3 * A_WIDTH
K0_OFF = Q0_OFF + B_WIDTH
V0_OFF = K0_OFF + B_KV_WIDTH
G0_OFF = V0_OFF + B_KV_WIDTH


def _params(*sem):
    return pltpu.CompilerParams(dimension_semantics=sem, vmem_limit_bytes=VMEM_LIMIT)


def _silu(x):
    return x * jax.nn.sigmoid(x)


def _ada_kernel(c_ref, w_ref, b_ref, o_ref):
    s = _silu(c_ref[...]).astype(BF16)
    o_ref[...] = jnp.dot(s, w_ref[...].astype(BF16), preferred_element_type=F32) + b_ref[...]


def _ada(cond, w_ada, b_ada):
    bn = 512
    n = w_ada.shape[1]
    return pl.pallas_call(
        _ada_kernel,
        grid=(n // bn,),
        in_specs=[
            pl.BlockSpec((8, D_MODEL), lambda j: (0, 0)),
            pl.BlockSpec((D_MODEL, bn), lambda j: (0, j)),
            pl.BlockSpec((1, bn), lambda j: (0, j)),
        ],
        out_specs=pl.BlockSpec((8, bn), lambda j: (0, j)),
        out_shape=jax.ShapeDtypeStruct((8, n), F32),
        compiler_params=_params("parallel"),
        name="ada",
    )(cond, w_ada, b_ada.reshape(1, n))


def _mm_kernel(h_ref, w_ref, o_ref):
    o_ref[...] = jnp.dot(h_ref[...], w_ref[...], preferred_element_type=F32).astype(o_ref.dtype)


def _mm(h, w, out_dtype, tn, col_blocks=None, tm=1024):
    t, k = h.shape
    first, stride, count = col_blocks if col_blocks is not None else (0, 1, w.shape[1] // tn)
    assert t % tm == 0 and w.shape[1] % tn == 0
    return pl.pallas_call(
        _mm_kernel,
        grid=(t // tm, count),
        in_specs=[
            pl.BlockSpec((tm, k), lambda i, j: (i, 0)),
            pl.BlockSpec((k, tn), lambda i, j: (0, first + stride * j)),
        ],
        out_specs=pl.BlockSpec((tm, tn), lambda i, j: (i, j)),
        out_shape=jax.ShapeDtypeStruct((t, count * tn), out_dtype),
        compiler_params=_params("parallel", "parallel"),
        name="in_proj",
    )(h, w)


def _mm_kt_kernel(h_ref, w_ref, kb_ref, kt_ref):
    acc = jnp.dot(h_ref[...], w_ref[...], preferred_element_type=F32)
    kb_ref[...] = acc.astype(kb_ref.dtype)
    for b in range(kt_ref.shape[0]):
        kt_ref[b] = acc[b * SEQ:(b + 1) * SEQ, :].T


def _mm_kt(h, w, col_block, n_batch, tb=2):
    t, k = h.shape
    tm = tb * SEQ
    return pl.pallas_call(
        _mm_kt_kernel,
        grid=(t // tm,),
        in_specs=[
            pl.BlockSpec((tm, k), lambda i: (i, 0)),
            pl.BlockSpec((k, C_WIDTH), lambda i: (0, col_block)),
        ],
        out_specs=[
            pl.BlockSpec((tm, C_WIDTH), lambda i: (i, 0)),
            pl.BlockSpec((tb, C_WIDTH, SEQ), lambda i: (i, 0, 0)),
        ],
        out_shape=[
            jax.ShapeDtypeStruct((t, C_WIDTH), BF16),
            jax.ShapeDtypeStruct((n_batch, C_WIDTH, SEQ), F32),
        ],
        compiler_params=_params("parallel"),
        name="in_proj_kt",
    )(h, w)


def _mm_sgate_kernel(h_ref, w_ref, ws_ref, bs_ref, o_ref):
    z = jnp.dot(h_ref[...], w_ref[...], preferred_element_type=F32)
    v = z[:, A_WIDTH:2 * A_WIDTH]
    mu = jnp.mean(v, axis=-1, keepdims=True)
    vc = v - mu
    var = jnp.mean(vc * vc, axis=-1, keepdims=True)
    vn = (vc * lax.rsqrt(var + EPS)).astype(BF16)
    for ch in range(z.shape[0] // CHUNK):
        rows = slice(ch * CHUNK, (ch + 1) * CHUNK)
        for g in range(A_GROUPS):
            cols = slice(g * LANES, (g + 1) * LANES)
            s = jnp.dot(ws_ref[g], vn[rows, cols], preferred_element_type=F32) + bs_ref[g]
            u = z[rows, g * LANES:(g + 1) * LANES]
            gate = z[rows, 2 * A_WIDTH + g * LANES:2 * A_WIDTH + (g + 1) * LANES]
            o_ref[rows, cols] = (u * s * _silu(gate)).astype(o_ref.dtype)


def _mm_sgate(h, w, ws_b, bs_b, tm=512):
    t, k = h.shape
    full = pl.BlockSpec((A_GROUPS, CHUNK, CHUNK), lambda i: (0, 0, 0))
    return pl.pallas_call(
        _mm_sgate_kernel,
        grid=(t // tm,),
        in_specs=[
            pl.BlockSpec((tm, k), lambda i: (i, 0)),
            pl.BlockSpec((k, 3 * A_WIDTH), lambda i: (0, 0)),
            full, full,
        ],
        out_specs=pl.BlockSpec((tm, A_WIDTH), lambda i: (i, 0)),
        out_shape=jax.ShapeDtypeStruct((t, A_WIDTH), BF16),
        compiler_params=_params("parallel"),
        name="in_proj_sgate",
    )(h, w, ws_b, bs_b)


def _rope_tables(half):
    rows = DEC_SEQ // GRID_W
    row_pos = jnp.repeat(jnp.arange(rows, dtype=F32), GRID_W)
    col_pos = jnp.tile(jnp.arange(GRID_W, dtype=F32), rows)
    freqs = ROPE_THETA ** (-jnp.arange(half, dtype=F32) / half)
    ang_r = row_pos[:, None] * freqs[None, :]
    ang_c = col_pos[:, None] * freqs[None, :]
    cr, sr, cc, sc = jnp.cos(ang_r), jnp.sin(ang_r), jnp.cos(ang_c), jnp.sin(ang_c)
    z = jnp.zeros_like(sr)
    reps = LANES // (4 * half)
    cos_t = jnp.tile(jnp.concatenate([cr, cr, cc, cc], axis=-1), (1, reps))
    sin_a = jnp.tile(jnp.concatenate([z, sr, z, sc], axis=-1), (1, reps))
    sin_b = jnp.tile(jnp.concatenate([-sr, z, -sc, z], axis=-1), (1, reps))
    return cos_t, sin_a, sin_b


def _rope_partner_matrix(half):
    lane = jnp.arange(LANES)
    first = (lane % (2 * half)) < half
    src = jnp.where(first, lane + half, lane - half)
    sign = jnp.where(first, -1.0, 1.0)
    return jnp.zeros((LANES, LANES), F32).at[src, lane].set(sign).astype(BF16)


def _rope(x, cos_t, sin_a, sin_b, half):
    return x * cos_t + pltpu.roll(x, half, 1) * sin_a + pltpu.roll(x, LANES - half, 1) * sin_b


QKV0_SUB_ROWS = 256


def _mm_qkv0_kernel(*refs, rope, emit_f32):
    x_ref, ng_ref, shift_ref, scale_ref, w_ref, qg_ref, kg_ref = refs[:7]
    refs = refs[7:]
    if rope:
        cos_ref, sin_ref, perm_ref = refs[:3]
        refs = refs[3:]
    h_ref, qo_ref, ko_ref, vo_ref = refs[:4]
    if emit_f32:
        kf_ref, vf_ref = refs[4:6]

    def norm(x, g):
        ms = jnp.mean(x * x, axis=-1, keepdims=True)
        return x * lax.rsqrt(ms + EPS) * g

    def rot(x, rows):
        if not rope:
            return x
        partner = jnp.dot(x.astype(BF16), perm_ref[...], preferred_element_type=F32)
        return x * cos_ref[rows, :] + partner * sin_ref[rows, :]

    qscale = (B_HEAD_DIM ** -0.5) * LOG2E
    for r0 in range(0, x_ref.shape[0], QKV0_SUB_ROWS):
        rows = slice(r0, r0 + QKV0_SUB_ROWS)
        hs = (norm(x_ref[rows, :], ng_ref[...]) * (1.0 + scale_ref[...]) + shift_ref[...]).astype(BF16)
        h_ref[rows, :] = hs
        z = jnp.dot(hs, w_ref[...], preferred_element_type=F32)
        for h in range(B_HEADS):
            cols = slice(h * LANES, (h + 1) * LANES)
            qn = norm(z[:, cols], qg_ref[...])
            qo_ref[rows, cols] = (rot(qn, rows) * qscale).astype(qo_ref.dtype)
        for h in range(B_KV_HEADS):
            cols = slice(h * LANES, (h + 1) * LANES)
            kn = norm(z[:, B_WIDTH + h * LANES:B_WIDTH + (h + 1) * LANES], kg_ref[...])
            ko_ref[rows, cols] = rot(kn, rows).astype(ko_ref.dtype)
            if emit_f32:
                kf_ref[rows, cols] = kn
        v = z[:, B_WIDTH + B_KV_WIDTH:]
        vo_ref[rows, :] = v.astype(vo_ref.dtype)
        if emit_f32:
            vf_ref[rows, :] = v


def _norm_mm_qkv0(x, norm_g, mod3, row_of_tile, w, q_g, k_g, tables, emit_f32, tm=512):
    t, kdim = x.shape
    rope = tables is not None
    qkv_w = B_WIDTH + 2 * B_KV_WIDTH
    in_specs = [
        pl.BlockSpec((tm, kdim), lambda i: (i, 0)),
        pl.BlockSpec((1, kdim), lambda i: (0, 0)),
        pl.BlockSpec((None, 1, kdim), lambda i: (row_of_tile(i), 0, 0)),
        pl.BlockSpec((None, 1, kdim), lambda i: (row_of_tile(i), 0, 1)),
        pl.BlockSpec((kdim, qkv_w), lambda i: (0, Q0_OFF // qkv_w)),
        pl.BlockSpec((1, LANES), lambda i: (0, 0)),
        pl.BlockSpec((1, LANES), lambda i: (0, 0)),
    ]
    args = [x, norm_g.reshape(1, kdim), mod3, mod3, w, q_g.reshape(1, LANES), k_g.reshape(1, LANES)]
    if rope:
        nt = DEC_SEQ // tm
        in_specs += [pl.BlockSpec((tm, LANES), lambda i: (i % nt, 0))] * 2
        in_specs.append(pl.BlockSpec((LANES, LANES), lambda i: (0, 0)))
        args += list(tables)
    out_specs = [
        pl.BlockSpec((tm, kdim), lambda i: (i, 0)),
        pl.BlockSpec((tm, B_WIDTH), lambda i: (i, 0)),
        pl.BlockSpec((tm, B_KV_WIDTH), lambda i: (i, 0)),
        pl.BlockSpec((tm, B_KV_WIDTH), lambda i: (i, 0)),
    ]
    out_shape = [
        jax.ShapeDtypeStruct((t, kdim), BF16),
        jax.ShapeDtypeStruct((t, B_WIDTH), BF16),
        jax.ShapeDtypeStruct((t, B_KV_WIDTH), BF16),
        jax.ShapeDtypeStruct((t, B_KV_WIDTH), BF16),
    ]
    if emit_f32:
        out_specs += [pl.BlockSpec((tm, B_KV_WIDTH), lambda i: (i, 0))] * 2
        out_shape += [jax.ShapeDtypeStruct((t, B_KV_WIDTH), F32)] * 2
    assert Q0_OFF % qkv_w == 0
    return pl.pallas_call(
        functools.partial(_mm_qkv0_kernel, rope=rope, emit_f32=emit_f32),
        grid=(t // tm,),
        in_specs=in_specs,
        out_specs=out_specs,
        out_shape=out_shape,
        compiler_params=_params("parallel"),
        name="in_proj_qkv",
    )(*args)


ROPE_SUB_COLS = 1024


def _mm_rope_kernel(h_ref, w_ref, cos_ref, sa_ref, sb_ref, o_ref, *, q_tiles, q_scale):
    scale = jnp.where(pl.program_id(1) < q_tiles, q_scale, 1.0)
    cos_t, sa, sb = cos_ref[...], sa_ref[...], sb_ref[...]
    h = h_ref[...]
    for c0 in range(0, w_ref.shape[1], ROPE_SUB_COLS):
        acc = jnp.dot(h, w_ref[:, c0:c0 + ROPE_SUB_COLS], preferred_element_type=F32)
        for c in range(ROPE_SUB_COLS // LANES):
            x = acc[:, c * LANES:(c + 1) * LANES]
            cols = slice(c0 + c * LANES, c0 + (c + 1) * LANES)
            o_ref[:, cols] = (_rope(x, cos_t, sa, sb, C_QK_DIM // 4) * scale).astype(o_ref.dtype)


def _mm_rope(h, w, tables, q_scale, tm=1024, tn=C_WIDTH):
    t, k = h.shape
    nt = DEC_SEQ // tm
    tab = pl.BlockSpec((tm, LANES), lambda i, j: (i % nt, 0))
    return pl.pallas_call(
        functools.partial(_mm_rope_kernel, q_tiles=C_WIDTH // tn, q_scale=q_scale),
        grid=(t // tm, 2 * C_WIDTH // tn),
        in_specs=[
            pl.BlockSpec((tm, k), lambda i, j: (i, 0)),
            pl.BlockSpec((k, tn), lambda i, j: (0, j)),
            tab, tab, tab,
        ],
        out_specs=pl.BlockSpec((tm, tn), lambda i, j: (i, j)),
        out_shape=jax.ShapeDtypeStruct((t, 2 * C_WIDTH), BF16),
        compiler_params=_params("parallel", "parallel"),
        name="in_proj_rope",
    )(h, w, *tables)


KEY_CHUNK = 256
SUBLANES = 8


MAX_KEY_BLOCK = 1024


def _key_blocks(s_new, has_cache):
    size = min(s_new, MAX_KEY_BLOCK)
    blocks = [(r, size) for r in range(0, s_new, size)]
    if has_cache:
        blocks.append((s_new, PAST_LEN))
    return blocks


def _attn_pipeline(units, blocks, s_scr, p_scr):
    n = s_scr.shape[2]
    st = [dict() for _ in units]

    def stage1(u, b):
        r0, size = blocks[b]
        if b == 0:
            st[u]["qq"] = units[u][0]()
        s = lax.dot_general(units[u][1](b), st[u]["qq"], (((1,), (1,)), ((), ())), preferred_element_type=F32)
        s_scr[u % 2, r0:r0 + size, :] = s
        m8 = s.reshape(size // SUBLANES, SUBLANES, n).max(axis=0)
        st[u]["m8"] = m8 if b == 0 else jnp.maximum(st[u]["m8"], m8)
        if b == len(blocks) - 1:
            st[u]["m"] = st[u]["m8"].max(axis=0, keepdims=True)

    def stage2(u, b):
        r0, size = blocks[b]
        for r in range(r0, r0 + size, KEY_CHUNK):
            p = jnp.exp2(s_scr[u % 2, r:r + KEY_CHUNK, :] - st[u]["m"])
            l8 = p.reshape(KEY_CHUNK // SUBLANES, SUBLANES, n).sum(axis=0)
            st[u]["l8"] = l8 if r == 0 else st[u]["l8"] + l8
            p_scr[u % 2, r:r + KEY_CHUNK, :] = p.astype(BF16)

    def stage3(u, b):
        r0, size = blocks[b]
        a = jnp.dot(units[u][2][:, r0:r0 + size], p_scr[u % 2, r0:r0 + size, :], preferred_element_type=F32)
        st[u]["acc"] = a if b == 0 else st[u]["acc"] + a
        if b == len(blocks) - 1:
            l = st[u]["l8"].sum(axis=0, keepdims=True)
            units[u][3](st[u]["acc"] * (1.0 / l))

    for slot in range(len(units) + 2):
        for b in range(len(blocks)):
            if 0 <= slot - 2 < len(units):
                stage3(slot - 2, b)
            if slot < len(units):
                stage1(slot, b)
            if 0 <= slot - 1 < len(units):
                stage2(slot - 1, b)


def _fill_vt(vt_scr, v_ref, vc_ref, n_kv):
    s_new = v_ref.shape[0]
    for j in range(n_kv):
        cols = slice(j * LANES, (j + 1) * LANES)
        vt_scr[j, :, 0:s_new] = v_ref[:, cols].astype(F32).T.astype(BF16)
        if vc_ref is not None:
            vt_scr[j, :, s_new:] = vc_ref[:, cols].astype(F32).T.astype(BF16)


def _attn_scratch(n_kv, s_tot, n):
    return [
        pltpu.VMEM((n_kv, LANES, s_tot), BF16),
        pltpu.VMEM((2, s_tot, n), F32),
        pltpu.VMEM((2, s_tot, n), BF16),
    ]


TQ = 256


def _gqa_kernel(*refs, has_cache):
    q_ref, k_ref, v_ref = refs[:3]
    refs = refs[3:]
    kc_ref = vc_ref = None
    if has_cache:
        (kc_ref, vc_ref), refs = refs[:2], refs[2:]
    g_ref, o_ref, vt_scr, s_scr, p_scr = refs
    blocks = _key_blocks(k_ref.shape[0], has_cache)

    @pl.when(pl.program_id(2) == 0)
    def _():
        _fill_vt(vt_scr, v_ref, vc_ref, 1)

    def key_block(b):
        r0, size = blocks[b]
        if r0 < k_ref.shape[0]:
            return k_ref[r0:r0 + size, :].astype(BF16)
        return kc_ref[...].astype(BF16)

    def make_unit(rows, heads):
        def make_qq():
            return jnp.concatenate([q_ref[rows, h * LANES:(h + 1) * LANES] for h in heads], axis=0)

        def finish(ot):
            for i, h in enumerate(heads):
                cols = slice(h * LANES, (h + 1) * LANES)
                gate = g_ref[rows, cols].astype(F32)
                o_ref[rows, cols] = (ot[:, i * TQ:(i + 1) * TQ].T * _silu(gate)).astype(o_ref.dtype)

        return make_qq, key_block, vt_scr.at[0], finish

    units = [make_unit(slice(r, r + TQ), (h, h + 1))
             for r in range(0, q_ref.shape[0], TQ) for h in range(0, B_GROUP, 2)]
    _attn_pipeline(units, blocks, s_scr, p_scr)


def _gqa(qp, kp, vp, cache, gate, nb, t, tqs):
    nq = t // tqs
    qw = B_GROUP * LANES
    s_tot = t + (PAST_LEN if cache is not None else 0)
    in_specs = [
        pl.BlockSpec((tqs, qw), lambda b, h, i: (b * nq + i, h)),
        pl.BlockSpec((t, LANES), lambda b, h, i: (b, h)),
        pl.BlockSpec((t, LANES), lambda b, h, i: (b, h)),
    ]
    args = [qp, kp, vp]
    if cache is not None:
        in_specs += [pl.BlockSpec((PAST_LEN, LANES), lambda b, h, i: (b, h))] * 2
        args += list(cache)
    in_specs.append(pl.BlockSpec((tqs, qw), lambda b, h, i: (b * nq + i, h)))
    args.append(gate)
    return pl.pallas_call(
        functools.partial(_gqa_kernel, has_cache=cache is not None),
        grid=(nb, B_KV_HEADS, nq),
        in_specs=in_specs,
        out_specs=pl.BlockSpec((tqs, qw), lambda b, h, i: (b * nq + i, h)),
        out_shape=jax.ShapeDtypeStruct((nb * t, B_WIDTH), BF16),
        scratch_shapes=_attn_scratch(1, s_tot, 2 * TQ),
        compiler_params=_params("parallel", "parallel", "arbitrary"),
        name="gqa_attn",
    )(*args)


def _diff_kernel(*refs, has_cache, hb, q_scale):
    q_ref, k_ref, v_ref = refs[:3]
    refs = refs[3:]
    kc_ref = vc_ref = None
    if has_cache:
        (kc_ref, vc_ref), refs = refs[:2], refs[2:]
    g_ref, lam_ref, sg_ref, o_ref, vt_scr, s_scr, p_scr = refs
    blocks = _key_blocks(k_ref.shape[0], has_cache)

    @pl.when(pl.program_id(2) == 0)
    def _():
        _fill_vt(vt_scr, v_ref, vc_ref, hb)

    lp = lam_ref[...]
    lam = (jnp.exp(jnp.sum(lp[0:1] * lp[1:2], axis=-1, keepdims=True))
           - jnp.exp(jnp.sum(lp[2:3] * lp[3:4], axis=-1, keepdims=True)) + LAMBDA_INIT_1)
    lane = lax.broadcasted_iota(jnp.int32, (TQ, LANES), 1)

    def make_unit(rows, j):
        cols = slice(j * LANES, (j + 1) * LANES)

        def make_qq():
            q = q_ref[rows, cols]
            if q_scale is not None:
                q = q.astype(F32) * q_scale
            zero = jnp.zeros_like(q)
            maps = [jnp.where(lane < C_QK_DIM, q, zero), jnp.where(lane >= C_QK_DIM, q, zero)]
            return jnp.concatenate(maps, axis=0).astype(BF16)

        def key_block(b):
            r0, size = blocks[b]
            if r0 < k_ref.shape[0]:
                return k_ref[r0:r0 + size, cols].astype(BF16)
            return kc_ref[:, cols].astype(BF16)

        def finish(ot):
            ot = ot[:, :TQ] - lam * ot[:, TQ:]
            ms = jnp.mean(ot * ot, axis=0, keepdims=True)
            o = (ot * lax.rsqrt(ms + EPS)).T * sg_ref[...] * (1.0 - LAMBDA_INIT_1)
            gate = g_ref[rows, cols].astype(F32)
            o_ref[rows, cols] = (o * _silu(gate)).astype(o_ref.dtype)

        return make_qq, key_block, vt_scr.at[j], finish

    units = [make_unit(slice(r, r + TQ), j) for j in range(hb) for r in range(0, q_ref.shape[0], TQ)]
    _attn_pipeline(units, blocks, s_scr, p_scr)


def _diff(q, k, v, g, cache, lam_p, sub_g, nb, t, tq, hb, q_scale):
    nq = t // tq
    w = hb * LANES
    s_tot = t + (PAST_LEN if cache is not None else 0)
    in_specs = [
        pl.BlockSpec((tq, w), lambda b, h, i: (b * nq + i, q[1] + h)),
        pl.BlockSpec((t, w), lambda b, h, i: (b, k[1] + h)),
        pl.BlockSpec((t, w), lambda b, h, i: (b, v[1] + h)),
    ]
    args = [q[0], k[0], v[0]]
    if cache is not None:
        in_specs += [pl.BlockSpec((PAST_LEN, w), lambda b, h, i: (b, h))] * 2
        args += list(cache)
    in_specs += [
        pl.BlockSpec((tq, w), lambda b, h, i: (b * nq + i, g[1] + h)),
        pl.BlockSpec((4, C_QK_DIM), lambda b, h, i: (0, 0)),
        pl.BlockSpec((1, LANES), lambda b, h, i: (0, 0)),
    ]
    args += [g[0], lam_p, sub_g.reshape(1, LANES)]
    return pl.pallas_call(
        functools.partial(_diff_kernel, has_cache=cache is not None, hb=hb, q_scale=q_scale),
        grid=(nb, C_HEADS // hb, nq),
        in_specs=in_specs,
        out_specs=pl.BlockSpec((tq, w), lambda b, h, i: (b * nq + i, h)),
        out_shape=jax.ShapeDtypeStruct((nb * t, C_WIDTH), BF16),
        scratch_shapes=_attn_scratch(hb, s_tot, 2 * TQ),
        compiler_params=_params("parallel", "parallel", "arbitrary"),
        name="diff_attn",
    )(*args)


def _out_kernel(*refs, n_in, final):
    y_refs = refs[:n_in]
    w_ref, x_ref, gate_ref = refs[n_in:n_in + 3]
    refs = refs[n_in + 3:]
    half = x_ref.shape[0] // 2
    for rows in (slice(0, half), slice(half, 2 * half)):
        acc = None
        k0 = 0
        for y_ref in y_refs:
            kk = y_ref.shape[1]
            a = jnp.dot(y_ref[rows, :], w_ref[k0:k0 + kk, :], preferred_element_type=F32)
            acc = a if acc is None else acc + a
            k0 += kk
        x = x_ref[rows, :] + gate_ref[...] * acc
        xn = x * lax.rsqrt(jnp.mean(x * x, axis=-1, keepdims=True) + EPS)
        if final:
            fg_ref, o_ref = refs
            o_ref[rows, :] = xn * fg_ref[...]
        else:
            g_ref, shift_ref, scale_ref, o_ref, h_ref = refs
            o_ref[rows, :] = x
            h_ref[rows, :] = (xn * g_ref[...] * (1.0 + scale_ref[...]) + shift_ref[...]).astype(h_ref.dtype)


def _out_proj(ys, w, x, mod3, row_of_tile, final_g=None, next_norm=None, tm=512):
    t = x.shape[0]
    final = final_g is not None
    vec = pl.BlockSpec((1, D_MODEL), lambda i: (0, 0))
    mod_row = lambda part: pl.BlockSpec((None, 1, D_MODEL), lambda i: (row_of_tile(i), 0, part))
    in_specs = [pl.BlockSpec((tm, y.shape[1]), lambda i: (i, 0)) for y in ys]
    in_specs += [
        pl.BlockSpec(w.shape, lambda i: (0, 0)),
        pl.BlockSpec((tm, D_MODEL), lambda i: (i, 0)),
        mod_row(2),
    ]
    args = list(ys) + [w, x, mod3]
    row_blk = pl.BlockSpec((tm, D_MODEL), lambda i: (i, 0))
    if final:
        in_specs.append(vec)
        args.append(final_g.reshape(1, D_MODEL))
        out_specs, out_shape = row_blk, jax.ShapeDtypeStruct((t, D_MODEL), F32)
    else:
        next_g, next_mod3 = next_norm
        in_specs += [vec, mod_row(0), mod_row(1)]
        args += [next_g.reshape(1, D_MODEL), next_mod3, next_mod3]
        out_specs = [row_blk, row_blk]
        out_shape = [jax.ShapeDtypeStruct((t, D_MODEL), F32), jax.ShapeDtypeStruct((t, D_MODEL), BF16)]
    return pl.pallas_call(
        functools.partial(_out_kernel, n_in=len(ys), final=final),
        grid=(t // tm,),
        in_specs=in_specs,
        out_specs=out_specs,
        out_shape=out_shape,
        compiler_params=_params("parallel"),
        name="out_proj",
    )(*args)


def kernel(x_prompt, x_sample, cache_k0, cache_v0, cache_k1, cache_v1, c, c_ctx, w_ada0, b_ada0, norm_g0, w_in0, w_s0, b_s0, q_norm_g0, k_norm_g0, w_out0, w_ada1, b_ada1, norm_g1, w_in1, lambda_q1, lambda_k1, lambda_q2, lambda_k2, subln_g1, w_out1, final_g):
    n_ctx, n_smp = x_prompt.shape[0], x_sample.shape[0]

    cond = jnp.concatenate([c_ctx[None], c, jnp.zeros((8 - 1 - n_smp, D_MODEL), F32)], axis=0)
    mod0 = _ada(cond, w_ada0, b_ada0).reshape(8, 1, 3 * D_MODEL)
    mod1 = _ada(cond, w_ada1, b_ada1).reshape(8, 1, 3 * D_MODEL)

    w_in0_b, w_out0_b = w_in0.astype(BF16), w_out0.astype(BF16)
    w_in1_b, w_out1_b = w_in1.astype(BF16), w_out1.astype(BF16)
    ws_b = w_s0.astype(BF16)
    bs_b = jnp.broadcast_to(b_s0[:, :, None], (A_GROUPS, CHUNK, LANES))
    lam_p = jnp.stack([lambda_q1, lambda_k1, lambda_q2, lambda_k2]).astype(F32)
    cos0, sin_a0, sin_b0 = _rope_tables(B_HEAD_DIM // 4)
    tab0 = (cos0, sin_a0 - sin_b0, _rope_partner_matrix(B_HEAD_DIM // 4))
    tab1 = _rope_tables(C_QK_DIM // 4)

    def run(x, nb, t, smp):
        def row_fn(tm):
            if not smp:
                return lambda i: 0
            return lambda i: 1 + (i * tm) // t

        front = (x, norm_g0, mod0, row_fn(512), w_in0_b, q_norm_g0, k_norm_g0)
        if smp:
            h, qp, kp, vp = _norm_mm_qkv0(*front, tab0, False)
            cache = (cache_k0.reshape(nb * PAST_LEN, B_KV_WIDTH), cache_v0.reshape(nb * PAST_LEN, B_KV_WIDTH))
            k0 = v0 = None
        else:
            h, qp, kp, vp, k0, v0 = _norm_mm_qkv0(*front, None, True)
            cache = None
        y_a = _mm_sgate(h, w_in0_b, ws_b, bs_b)
        gate_b = _mm(h, w_in0_b, BF16, 512, (G0_OFF // 512, 1, B_WIDTH // 512))
        y_b = _gqa(qp, kp, vp, cache, gate_b, nb, t, t)
        x1, h = _out_proj([y_a, y_b], w_out0_b, x, mod0, row_fn(512), next_norm=(norm_g1, mod1))

        if smp:
            qk = _mm_rope(h, w_in1_b, tab1, (C_QK_DIM ** -0.5) * LOG2E)
            vg = _mm(h, w_in1_b, BF16, 1024, (2 * C_WIDTH // 1024, 1, 2 * C_WIDTH // 1024))
            cache = (cache_k1.reshape(nb * PAST_LEN, C_WIDTH), cache_v1.reshape(nb * PAST_LEN, C_WIDTH))
            hb = 2
            second = C_WIDTH // (hb * LANES)
            y_c = _diff((qk, 0), (qk, second), (vg, 0), (vg, second), cache, lam_p, subln_g1,
                        nb, t, 4096, hb, None)
            k1 = v1 = None
        else:
            qg = _mm(h, w_in1_b, BF16, C_WIDTH, (0, 3, 2))
            kb, k1 = _mm_kt(h, w_in1_b, 1, nb)
            v1 = _mm(h, w_in1_b, F32, C_WIDTH, (2, 1, 1), tm=512)
            hb = 16
            y_c = _diff((qg, 0), (kb, 0), (v1, 0), (qg, C_WIDTH // (hb * LANES)), None, lam_p, subln_g1,
                        nb, t, SEQ, hb, (C_QK_DIM ** -0.5) * LOG2E)
        y = _out_proj([y_c], w_out1_b, x1, mod1, row_fn(512), final_g)
        return y, k0, v0, k1, v1

    y_p, k0, v0, k1, v1 = run(x_prompt.reshape(n_ctx * SEQ, D_MODEL), n_ctx, SEQ, False)
    y_s, _, _, _, _ = run(x_sample.reshape(n_smp * DEC_SEQ, D_MODEL), n_smp, DEC_SEQ, True)

    return (
        y_p.reshape(n_ctx, SEQ, D_MODEL),
        y_s.reshape(n_smp, DEC_SEQ, D_MODEL),
        k0.reshape(n_ctx, SEQ, B_KV_HEADS, B_HEAD_DIM),
        v0.reshape(n_ctx, SEQ, B_KV_HEADS, B_HEAD_DIM),
        k1.reshape(n_ctx, C_HEADS, 2, C_QK_DIM, SEQ).transpose(0, 4, 1, 2, 3),
        v1.reshape(n_ctx, SEQ, C_HEADS, C_V_DIM),
    )
```

```python
import functools
import math

import jax
import jax.numpy as jnp
from jax import lax
from jax.experimental import pallas as pl
from jax.experimental.pallas import tpu as pltpu

F32 = jnp.float32
BF16 = jnp.bfloat16

D_MODEL = 2048
SEQ = 256
DEC_SEQ = 4096
PAST_LEN = 256
GRID_W = 64
CHUNK = 128
ROPE_THETA = 10000.0
EPS = 1e-6

A_GROUPS = 8
A_WIDTH = 1024
B_HEADS = 8
B_KV_HEADS = 2
B_GROUP = B_HEADS // B_KV_HEADS
B_HEAD_DIM = 128
B_WIDTH = 1024
B_KV_WIDTH = 256
IN0_WIDTH = 3 * A_WIDTH + 2 * B_WIDTH + 2 * B_KV_WIDTH
C_HEADS = 16
C_QK_DIM = 64
C_V_DIM = 128
C_WIDTH = 2048
IN1_WIDTH = 4 * C_WIDTH
LAMBDA_INIT_1 = 0.8 - 0.6 * math.exp(-0.3 * 1)

LANES = 128
LOG2E = math.log2(math.e)
VMEM_LIMIT = 56 * 1024 * 1024

ROW_TILE = 512
GQA_ROWS_SMP = 1024
DIFF_HEADS_CTX = 16

Q0_OFF = 3 * A_WIDTH
K0_OFF = Q0_OFF + B_WIDTH
V0_OFF = K0_OFF + B_KV_WIDTH
G0_OFF = V0_OFF + B_KV_WIDTH


def _params(*sem):
    return pltpu.CompilerParams(dimension_semantics=sem, vmem_limit_bytes=VMEM_LIMIT)


def _silu(x):
    return x * jax.nn.sigmoid(x)


def _ada_kernel(c_ref, w_ref, b_ref, o_ref):
    s = _silu(c_ref[...]).astype(BF16)
    o_ref[...] = jnp.dot(s, w_ref[...].astype(BF16), preferred_element_type=F32) + b_ref[...]


def _ada(cond, w_ada, b_ada):
    bn = 512
    n = w_ada.shape[1]
    return pl.pallas_call(
        _ada_kernel,
        grid=(n // bn,),
        in_specs=[
            pl.BlockSpec((8, D_MODEL), lambda j: (0, 0)),
            pl.BlockSpec((D_MODEL, bn), lambda j: (0, j)),
            pl.BlockSpec((1, bn), lambda j: (0, j)),
        ],
        out_specs=pl.BlockSpec((8, bn), lambda j: (0, j)),
        out_shape=jax.ShapeDtypeStruct((8, n), F32),
        compiler_params=_params("parallel"),
        name="ada",
    )(cond, w_ada, b_ada.reshape(1, n))


def _mm_kernel(h_ref, w_ref, o_ref):
    o_ref[...] = jnp.dot(h_ref[...], w_ref[...], preferred_element_type=F32).astype(o_ref.dtype)


def _mm(h, w, out_dtype, tn, col_blocks=None, tm=1024):
    t, k = h.shape
    first, stride, count = col_blocks if col_blocks is not None else (0, 1, w.shape[1] // tn)
    assert t % tm == 0 and w.shape[1] % tn == 0
    return pl.pallas_call(
        _mm_kernel,
        grid=(t // tm, count),
        in_specs=[
            pl.BlockSpec((tm, k), lambda i, j: (i, 0)),
            pl.BlockSpec((k, tn), lambda i, j: (0, first + stride * j)),
        ],
        out_specs=pl.BlockSpec((tm, tn), lambda i, j: (i, j)),
        out_shape=jax.ShapeDtypeStruct((t, count * tn), out_dtype),
        compiler_params=_params("parallel", "parallel"),
        name="in_proj",
    )(h, w)


def _mm_kt_kernel(h_ref, w_ref, kb_ref, kt_ref):
    acc = jnp.dot(h_ref[...], w_ref[...], preferred_element_type=F32)
    kb_ref[...] = acc.astype(kb_ref.dtype)
    for b in range(kt_ref.shape[0]):
        kt_ref[b] = acc[b * SEQ:(b + 1) * SEQ, :].T


def _mm_kt(h, w, col_block, n_batch, tb=2):
    t, k = h.shape
    tm = tb * SEQ
    return pl.pallas_call(
        _mm_kt_kernel,
        grid=(t // tm,),
        in_specs=[
            pl.BlockSpec((tm, k), lambda i: (i, 0)),
            pl.BlockSpec((k, C_WIDTH), lambda i: (0, col_block)),
        ],
        out_specs=[
            pl.BlockSpec((tm, C_WIDTH), lambda i: (i, 0)),
            pl.BlockSpec((tb, C_WIDTH, SEQ), lambda i: (i, 0, 0)),
        ],
        out_shape=[
            jax.ShapeDtypeStruct((t, C_WIDTH), BF16),
            jax.ShapeDtypeStruct((n_batch, C_WIDTH, SEQ), F32),
        ],
        compiler_params=_params("parallel"),
        name="in_proj_kt",
    )(h, w)


def _mm_sgate_kernel(h_ref, w_ref, ws_ref, bs_ref, o_ref):
    z = jnp.dot(h_ref[...], w_ref[...], preferred_element_type=F32)
    v = z[:, A_WIDTH:2 * A_WIDTH]
    mu = jnp.mean(v, axis=-1, keepdims=True)
    vc = v - mu
    var = jnp.mean(vc * vc, axis=-1, keepdims=True)
    vn = (vc * lax.rsqrt(var + EPS)).astype(BF16)
    for ch in range(z.shape[0] // CHUNK):
        rows = slice(ch * CHUNK, (ch + 1) * CHUNK)
        for g in range(A_GROUPS):
            cols = slice(g * LANES, (g + 1) * LANES)
            s = jnp.dot(ws_ref[g], vn[rows, cols], preferred_element_type=F32) + bs_ref[g]
            u = z[rows, g * LANES:(g + 1) * LANES]
            gate = z[rows, 2 * A_WIDTH + g * LANES:2 * A_WIDTH + (g + 1) * LANES]
            o_ref[rows, cols] = (u * s * _silu(gate)).astype(o_ref.dtype)


def _mm_sgate(h, w, ws_b, bs_b, tm=512):
    t, k = h.shape
    full = pl.BlockSpec((A_GROUPS, CHUNK, CHUNK), lambda i: (0, 0, 0))
    return pl.pallas_call(
        _mm_sgate_kernel,
        grid=(t // tm,),
        in_specs=[
            pl.BlockSpec((tm, k), lambda i: (i, 0)),
            pl.BlockSpec((k, 3 * A_WIDTH), lambda i: (0, 0)),
            full, full,
        ],
        out_specs=pl.BlockSpec((tm, A_WIDTH), lambda i: (i, 0)),
        out_shape=jax.ShapeDtypeStruct((t, A_WIDTH), BF16),
        compiler_params=_params("parallel"),
        name="in_proj_sgate",
    )(h, w, ws_b, bs_b)


def _rope_tables(half):
    rows = DEC_SEQ // GRID_W
    row_pos = jnp.repeat(jnp.arange(rows, dtype=F32), GRID_W)
    col_pos = jnp.tile(jnp.arange(GRID_W, dtype=F32), rows)
    freqs = ROPE_THETA ** (-jnp.arange(half, dtype=F32) / half)
    ang_r = row_pos[:, None] * freqs[None, :]
    ang_c = col_pos[:, None] * freqs[None, :]
    cr, sr, cc, sc = jnp.cos(ang_r), jnp.sin(ang_r), jnp.cos(ang_c), jnp.sin(ang_c)
    z = jnp.zeros_like(sr)
    reps = LANES // (4 * half)
    cos_t = jnp.tile(jnp.concatenate([cr, cr, cc, cc], axis=-1), (1, reps))
    sin_a = jnp.tile(jnp.concatenate([z, sr, z, sc], axis=-1), (1, reps))
    sin_b = jnp.tile(jnp.concatenate([-sr, z, -sc, z], axis=-1), (1, reps))
    return cos_t, sin_a, sin_b


def _rope_partner_matrix(half):
    lane = jnp.arange(LANES)
    first = (lane % (2 * half)) < half
    src = jnp.where(first, lane + half, lane - half)
    sign = jnp.where(first, -1.0, 1.0)
    return jnp.zeros((LANES, LANES), F32).at[src, lane].set(sign).astype(BF16)


def _rope(x, cos_t, sin_a, sin_b, half):
    return x * cos_t + pltpu.roll(x, half, 1) * sin_a + pltpu.roll(x, LANES - half, 1) * sin_b


QKV0_SUB_ROWS = 256


def _mm_qkv0_kernel(*refs, rope, emit_f32):
    x_ref, ng_ref, shift_ref, scale_ref, w_ref, qg_ref, kg_ref = refs[:7]
    refs = refs[7:]
    if rope:
        cos_ref, sin_ref, perm_ref = refs[:3]
        refs = refs[3:]
    h_ref, qo_ref, ko_ref, vo_ref = refs[:4]
    if emit_f32:
        kf_ref, vf_ref = refs[4:6]

    def norm(x, g):
        ms = jnp.mean(x * x, axis=-1, keepdims=True)
        return x * lax.rsqrt(ms + EPS) * g

    def rot(x, rows):
        if not rope:
            return x
        partner = jnp.dot(x.astype(BF16), perm_ref[...], preferred_element_type=F32)
        return x * cos_ref[rows, :] + partner * sin_ref[rows, :]

    qscale = (B_HEAD_DIM ** -0.5) * LOG2E
    for r0 in range(0, x_ref.shape[0], QKV0_SUB_ROWS):
        rows = slice(r0, r0 + QKV0_SUB_ROWS)
        hs = (norm(x_ref[rows, :], ng_ref[...]) * (1.0 + scale_ref[...]) + shift_ref[...]).astype(BF16)
        h_ref[rows, :] = hs
        z = jnp.dot(hs, w_ref[...], preferred_element_type=F32)
        for h in range(B_HEADS):
            cols = slice(h * LANES, (h + 1) * LANES)
            qn = norm(z[:, cols], qg_ref[...])
            qo_ref[rows, cols] = (rot(qn, rows) * qscale).astype(qo_ref.dtype)
        for h in range(B_KV_HEADS):
            cols = slice(h * LANES, (h + 1) * LANES)
            kn = norm(z[:, B_WIDTH + h * LANES:B_WIDTH + (h + 1) * LANES], kg_ref[...])
            ko_ref[rows, cols] = rot(kn, rows).astype(ko_ref.dtype)
            if emit_f32:
                kf_ref[rows, cols] = kn
        v = z[:, B_WIDTH + B_KV_WIDTH:]
        vo_ref[rows, :] = v.astype(vo_ref.dtype)
        if emit_f32:
            vf_ref[rows, :] = v


def _norm_mm_qkv0(x, norm_g, mod3, row_of_tile, w, q_g, k_g, tables, emit_f32, tm=ROW_TILE):
    t, kdim = x.shape
    rope = tables is not None
    qkv_w = B_WIDTH + 2 * B_KV_WIDTH
    in_specs = [
        pl.BlockSpec((tm, kdim), lambda i: (i, 0)),
        pl.BlockSpec((1, kdim), lambda i: (0, 0)),
        pl.BlockSpec((None, 1, kdim), lambda i: (row_of_tile(i), 0, 0)),
        pl.BlockSpec((None, 1, kdim), lambda i: (row_of_tile(i), 0, 1)),
        pl.BlockSpec((kdim, qkv_w), lambda i: (0, Q0_OFF // qkv_w)),
        pl.BlockSpec((1, LANES), lambda i: (0, 0)),
        pl.BlockSpec((1, LANES), lambda i: (0, 0)),
    ]
    args = [x, norm_g.reshape(1, kdim), mod3, mod3, w, q_g.reshape(1, LANES), k_g.reshape(1, LANES)]
    if rope:
        nt = DEC_SEQ // tm
        in_specs += [pl.BlockSpec((tm, LANES), lambda i: (i % nt, 0))] * 2
        in_specs.append(pl.BlockSpec((LANES, LANES), lambda i: (0, 0)))
        args += list(tables)
    out_specs = [
        pl.BlockSpec((tm, kdim), lambda i: (i, 0)),
        pl.BlockSpec((tm, B_WIDTH), lambda i: (i, 0)),
        pl.BlockSpec((tm, B_KV_WIDTH), lambda i: (i, 0)),
        pl.BlockSpec((tm, B_KV_WIDTH), lambda i: (i, 0)),
    ]
    out_shape = [
        jax.ShapeDtypeStruct((t, kdim), BF16),
        jax.ShapeDtypeStruct((t, B_WIDTH), BF16),
        jax.ShapeDtypeStruct((t, B_KV_WIDTH), BF16),
        jax.ShapeDtypeStruct((t, B_KV_WIDTH), BF16),
    ]
    if emit_f32:
        out_specs += [pl.BlockSpec((tm, B_KV_WIDTH), lambda i: (i, 0))] * 2
        out_shape += [jax.ShapeDtypeStruct((t, B_KV_WIDTH), F32)] * 2
    assert Q0_OFF % qkv_w == 0
    return pl.pallas_call(
        functools.partial(_mm_qkv0_kernel, rope=rope, emit_f32=emit_f32),
        grid=(t // tm,),
        in_specs=in_specs,
        out_specs=out_specs,
        out_shape=out_shape,
        compiler_params=_params("parallel"),
        name="in_proj_qkv",
    )(*args)


ROPE_SUB_COLS = 1024


def _mm_rope_kernel(h_ref, w_ref, cos_ref, sa_ref, sb_ref, o_ref, *, q_tiles, q_scale):
    scale = jnp.where(pl.program_id(1) < q_tiles, q_scale, 1.0)
    cos_t, sa, sb = cos_ref[...], sa_ref[...], sb_ref[...]
    h = h_ref[...]
    for c0 in range(0, w_ref.shape[1], ROPE_SUB_COLS):
        acc = jnp.dot(h, w_ref[:, c0:c0 + ROPE_SUB_COLS], preferred_element_type=F32)
        for c in range(ROPE_SUB_COLS // LANES):
            x = acc[:, c * LANES:(c + 1) * LANES]
            cols = slice(c0 + c * LANES, c0 + (c + 1) * LANES)
            o_ref[:, cols] = (_rope(x, cos_t, sa, sb, C_QK_DIM // 4) * scale).astype(o_ref.dtype)


def _mm_rope(h, w, tables, q_scale, tm=1024, tn=C_WIDTH):
    t, k = h.shape
    nt = DEC_SEQ // tm
    tab = pl.BlockSpec((tm, LANES), lambda i, j: (i % nt, 0))
    return pl.pallas_call(
        functools.partial(_mm_rope_kernel, q_tiles=C_WIDTH // tn, q_scale=q_scale),
        grid=(t // tm, 2 * C_WIDTH // tn),
        in_specs=[
            pl.BlockSpec((tm, k), lambda i, j: (i, 0)),
            pl.BlockSpec((k, tn), lambda i, j: (0, j)),
            tab, tab, tab,
        ],
        out_specs=pl.BlockSpec((tm, tn), lambda i, j: (i, j)),
        out_shape=jax.ShapeDtypeStruct((t, 2 * C_WIDTH), BF16),
        compiler_params=_params("parallel", "parallel"),
        name="in_proj_rope",
    )(h, w, *tables)


KEY_CHUNK = 256
SUBLANES = 8


MAX_KEY_BLOCK = 1024


def _key_blocks(s_new, has_cache):
    size = min(s_new, MAX_KEY_BLOCK)
    blocks = [(r, size) for r in range(0, s_new, size)]
    if has_cache:
        blocks.append((s_new, PAST_LEN))
    return blocks


def _attn_pipeline(units, blocks, s_scr, p_scr):
    n = s_scr.shape[2]
    st = [dict() for _ in units]

    def stage1(u, b):
        r0, size = blocks[b]
        if b == 0:
            st[u]["qq"] = units[u][0]()
        s = lax.dot_general(units[u][1](b), st[u]["qq"], (((1,), (1,)), ((), ())), preferred_element_type=F32)
        s_scr[u % 2, r0:r0 + size, :] = s
        m8 = s.reshape(size // SUBLANES, SUBLANES, n).max(axis=0)
        st[u]["m8"] = m8 if b == 0 else jnp.maximum(st[u]["m8"], m8)
        if b == len(blocks) - 1:
            st[u]["m"] = st[u]["m8"].max(axis=0, keepdims=True)

    def stage2(u, b):
        r0, size = blocks[b]
        for r in range(r0, r0 + size, KEY_CHUNK):
            p = jnp.exp2(s_scr[u % 2, r:r + KEY_CHUNK, :] - st[u]["m"])
            l8 = p.reshape(KEY_CHUNK // SUBLANES, SUBLANES, n).sum(axis=0)
            st[u]["l8"] = l8 if r == 0 else st[u]["l8"] + l8
            p_scr[u % 2, r:r + KEY_CHUNK, :] = p.astype(BF16)

    def stage3(u, b):
        r0, size = blocks[b]
        a = jnp.dot(units[u][2][:, r0:r0 + size], p_scr[u % 2, r0:r0 + size, :], preferred_element_type=F32)
        st[u]["acc"] = a if b == 0 else st[u]["acc"] + a
        if b == len(blocks) - 1:
            l = st[u]["l8"].sum(axis=0, keepdims=True)
            units[u][3](st[u]["acc"] * (1.0 / l))

    for slot in range(len(units) + 2):
        for b in range(len(blocks)):
            if 0 <= slot - 2 < len(units):
                stage3(slot - 2, b)
            if slot < len(units):
                stage1(slot, b)
            if 0 <= slot - 1 < len(units):
                stage2(slot - 1, b)


def _fill_vt(vt_scr, v_ref, vc_ref, n_kv):
    s_new = v_ref.shape[0]
    for j in range(n_kv):
        cols = slice(j * LANES, (j + 1) * LANES)
        vt_scr[j, :, 0:s_new] = v_ref[:, cols].astype(F32).T.astype(BF16)
        if vc_ref is not None:
            vt_scr[j, :, s_new:] = vc_ref[:, cols].astype(F32).T.astype(BF16)


def _attn_scratch(n_kv, s_tot, n):
    return [
        pltpu.VMEM((n_kv, LANES, s_tot), BF16),
        pltpu.VMEM((2, s_tot, n), F32),
        pltpu.VMEM((2, s_tot, n), BF16),
    ]


TQ = 256


def _gqa_kernel(*refs, has_cache):
    q_ref, k_ref, v_ref = refs[:3]
    refs = refs[3:]
    kc_ref = vc_ref = None
    if has_cache:
        (kc_ref, vc_ref), refs = refs[:2], refs[2:]
    g_ref, o_ref, vt_scr, s_scr, p_scr = refs
    blocks = _key_blocks(k_ref.shape[0], has_cache)

    @pl.when(pl.program_id(2) == 0)
    def _():
        _fill_vt(vt_scr, v_ref, vc_ref, 1)

    def key_block(b):
        r0, size = blocks[b]
        if r0 < k_ref.shape[0]:
            return k_ref[r0:r0 + size, :].astype(BF16)
        return kc_ref[...].astype(BF16)

    def make_unit(rows, heads):
        def make_qq():
            return jnp.concatenate([q_ref[rows, h * LANES:(h + 1) * LANES] for h in heads], axis=0)

        def finish(ot):
            for i, h in enumerate(heads):
                cols = slice(h * LANES, (h + 1) * LANES)
                gate = g_ref[rows, cols].astype(F32)
                o_ref[rows, cols] = (ot[:, i * TQ:(i + 1) * TQ].T * _silu(gate)).astype(o_ref.dtype)

        return make_qq, key_block, vt_scr.at[0], finish

    units = [make_unit(slice(r, r + TQ), (h, h + 1))
             for r in range(0, q_ref.shape[0], TQ) for h in range(0, B_GROUP, 2)]
    _attn_pipeline(units, blocks, s_scr, p_scr)


def _gqa(qp, kp, vp, cache, gate, nb, t, tqs):
    nq = t // tqs
    qw = B_GROUP * LANES
    s_tot = t + (PAST_LEN if cache is not None else 0)
    in_specs = [
        pl.BlockSpec((tqs, qw), lambda b, h, i: (b * nq + i, h)),
        pl.BlockSpec((t, LANES), lambda b, h, i: (b, h)),
        pl.BlockSpec((t, LANES), lambda b, h, i: (b, h)),
    ]
    args = [qp, kp, vp]
    if cache is not None:
        in_specs += [pl.BlockSpec((PAST_LEN, LANES), lambda b, h, i: (b, h))] * 2
        args += list(cache)
    in_specs.append(pl.BlockSpec((tqs, qw), lambda b, h, i: (b * nq + i, h)))
    args.append(gate)
    return pl.pallas_call(
        functools.partial(_gqa_kernel, has_cache=cache is not None),
        grid=(nb, B_KV_HEADS, nq),
        in_specs=in_specs,
        out_specs=pl.BlockSpec((tqs, qw), lambda b, h, i: (b * nq + i, h)),
        out_shape=jax.ShapeDtypeStruct((nb * t, B_WIDTH), BF16),
        scratch_shapes=_attn_scratch(1, s_tot, 2 * TQ),
        compiler_params=_params("parallel", "parallel", "arbitrary"),
        name="gqa_attn",
    )(*args)


def _diff_kernel(*refs, has_cache, hb, q_scale):
    q_ref, k_ref, v_ref = refs[:3]
    refs = refs[3:]
    kc_ref = vc_ref = None
    if has_cache:
        (kc_ref, vc_ref), refs = refs[:2], refs[2:]
    g_ref, lam_ref, sg_ref, o_ref, vt_scr, s_scr, p_scr = refs
    blocks = _key_blocks(k_ref.shape[0], has_cache)

    @pl.when(pl.program_id(2) == 0)
    def _():
        _fill_vt(vt_scr, v_ref, vc_ref, hb)

    lp = lam_ref[...]
    lam = (jnp.exp(jnp.sum(lp[0:1] * lp[1:2], axis=-1, keepdims=True))
           - jnp.exp(jnp.sum(lp[2:3] * lp[3:4], axis=-1, keepdims=True)) + LAMBDA_INIT_1)
    lane = lax.broadcasted_iota(jnp.int32, (TQ, LANES), 1)

    def make_unit(rows, j):
        cols = slice(j * LANES, (j + 1) * LANES)

        def make_qq():
            q = q_ref[rows, cols]
            if q_scale is not None:
                q = q.astype(F32) * q_scale
            zero = jnp.zeros_like(q)
            maps = [jnp.where(lane < C_QK_DIM, q, zero), jnp.where(lane >= C_QK_DIM, q, zero)]
            return jnp.concatenate(maps, axis=0).astype(BF16)

        def key_block(b):
            r0, size = blocks[b]
            if r0 < k_ref.shape[0]:
                return k_ref[r0:r0 + size, cols].astype(BF16)
            return kc_ref[:, cols].astype(BF16)

        def finish(ot):
            ot = ot[:, :TQ] - lam * ot[:, TQ:]
            ms = jnp.mean(ot * ot, axis=0, keepdims=True)
            o = (ot * lax.rsqrt(ms + EPS)).T * sg_ref[...] * (1.0 - LAMBDA_INIT_1)
            gate = g_ref[rows, cols].astype(F32)
            o_ref[rows, cols] = (o * _silu(gate)).astype(o_ref.dtype)

        return make_qq, key_block, vt_scr.at[j], finish

    units = [make_unit(slice(r, r + TQ), j) for j in range(hb) for r in range(0, q_ref.shape[0], TQ)]
    _attn_pipeline(units, blocks, s_scr, p_scr)


def _diff(q, k, v, g, cache, lam_p, sub_g, nb, t, tq, hb, q_scale):
    nq = t // tq
    w = hb * LANES
    s_tot = t + (PAST_LEN if cache is not None else 0)
    in_specs = [
        pl.BlockSpec((tq, w), lambda b, h, i: (b * nq + i, q[1] + h)),
        pl.BlockSpec((t, w), lambda b, h, i: (b, k[1] + h)),
        pl.BlockSpec((t, w), lambda b, h, i: (b, v[1] + h)),
    ]
    args = [q[0], k[0], v[0]]
    if cache is not None:
        in_specs += [pl.BlockSpec((PAST_LEN, w), lambda b, h, i: (b, h))] * 2
        args += list(cache)
    in_specs += [
        pl.BlockSpec((tq, w), lambda b, h, i: (b * nq + i, g[1] + h)),
        pl.BlockSpec((4, C_QK_DIM), lambda b, h, i: (0, 0)),
        pl.BlockSpec((1, LANES), lambda b, h, i: (0, 0)),
    ]
    args += [g[0], lam_p, sub_g.reshape(1, LANES)]
    return pl.pallas_call(
        functools.partial(_diff_kernel, has_cache=cache is not None, hb=hb, q_scale=q_scale),
        grid=(nb, C_HEADS // hb, nq),
        in_specs=in_specs,
        out_specs=pl.BlockSpec((tq, w), lambda b, h, i: (b * nq + i, h)),
        out_shape=jax.ShapeDtypeStruct((nb * t, C_WIDTH), BF16),
        scratch_shapes=_attn_scratch(hb, s_tot, 2 * TQ),
        compiler_params=_params("parallel", "parallel", "arbitrary"),
        name="diff_attn",
    )(*args)


def _out_kernel(*refs, n_in, final):
    y_refs = refs[:n_in]
    w_ref, x_ref, gate_ref = refs[n_in:n_in + 3]
    refs = refs[n_in + 3:]
    half = x_ref.shape[0] // 2
    for rows in (slice(0, half), slice(half, 2 * half)):
        acc = None
        k0 = 0
        for y_ref in y_refs:
            kk = y_ref.shape[1]
            a = jnp.dot(y_ref[rows, :], w_ref[k0:k0 + kk, :], preferred_element_type=F32)
            acc = a if acc is None else acc + a
            k0 += kk
        x = x_ref[rows, :] + gate_ref[...] * acc
        xn = x * lax.rsqrt(jnp.mean(x * x, axis=-1, keepdims=True) + EPS)
        if final:
            fg_ref, o_ref = refs
            o_ref[rows, :] = xn * fg_ref[...]
        else:
            g_ref, shift_ref, scale_ref, o_ref, h_ref = refs
            o_ref[rows, :] = x
            h_ref[rows, :] = (xn * g_ref[...] * (1.0 + scale_ref[...]) + shift_ref[...]).astype(h_ref.dtype)


def _out_proj(ys, w, x, mod3, row_of_tile, final_g=None, next_norm=None, tm=ROW_TILE):
    t = x.shape[0]
    final = final_g is not None
    vec = pl.BlockSpec((1, D_MODEL), lambda i: (0, 0))
    mod_row = lambda part: pl.BlockSpec((None, 1, D_MODEL), lambda i: (row_of_tile(i), 0, part))
    in_specs = [pl.BlockSpec((tm, y.shape[1]), lambda i: (i, 0)) for y in ys]
    in_specs += [
        pl.BlockSpec(w.shape, lambda i: (0, 0)),
        pl.BlockSpec((tm, D_MODEL), lambda i: (i, 0)),
        mod_row(2),
    ]
    args = list(ys) + [w, x, mod3]
    row_blk = pl.BlockSpec((tm, D_MODEL), lambda i: (i, 0))
    if final:
        in_specs.append(vec)
        args.append(final_g.reshape(1, D_MODEL))
        out_specs, out_shape = row_blk, jax.ShapeDtypeStruct((t, D_MODEL), F32)
    else:
        next_g, next_mod3 = next_norm
        in_specs += [vec, mod_row(0), mod_row(1)]
        args += [next_g.reshape(1, D_MODEL), next_mod3, next_mod3]
        out_specs = [row_blk, row_blk]
        out_shape = [jax.ShapeDtypeStruct((t, D_MODEL), F32), jax.ShapeDtypeStruct((t, D_MODEL), BF16)]
    return pl.pallas_call(
        functools.partial(_out_kernel, n_in=len(ys), final=final),
        grid=(t // tm,),
        in_specs=in_specs,
        out_specs=out_specs,
        out_shape=out_shape,
        compiler_params=_params("parallel"),
        name="out_proj",
    )(*args)


def kernel(x_prompt, x_sample, cache_k0, cache_v0, cache_k1, cache_v1, c, c_ctx, w_ada0, b_ada0, norm_g0, w_in0, w_s0, b_s0, q_norm_g0, k_norm_g0, w_out0, w_ada1, b_ada1, norm_g1, w_in1, lambda_q1, lambda_k1, lambda_q2, lambda_k2, subln_g1, w_out1, final_g):
    n_ctx, n_smp = x_prompt.shape[0], x_sample.shape[0]

    cond = jnp.concatenate([c_ctx[None], c, jnp.zeros((8 - 1 - n_smp, D_MODEL), F32)], axis=0)
    mod0 = _ada(cond, w_ada0, b_ada0).reshape(8, 1, 3 * D_MODEL)
    mod1 = _ada(cond, w_ada1, b_ada1).reshape(8, 1, 3 * D_MODEL)

    w_in0_b, w_out0_b = w_in0.astype(BF16), w_out0.astype(BF16)
    w_in1_b, w_out1_b = w_in1.astype(BF16), w_out1.astype(BF16)
    ws_b = w_s0.astype(BF16)
    bs_b = jnp.broadcast_to(b_s0[:, :, None], (A_GROUPS, CHUNK, LANES))
    lam_p = jnp.stack([lambda_q1, lambda_k1, lambda_q2, lambda_k2]).astype(F32)
    cos0, sin_a0, sin_b0 = _rope_tables(B_HEAD_DIM // 4)
    tab0 = (cos0, sin_a0 - sin_b0, _rope_partner_matrix(B_HEAD_DIM // 4))
    tab1 = _rope_tables(C_QK_DIM // 4)

    def run(x, nb, t, smp):
        def row_fn(tm):
            if not smp:
                return lambda i: 0
            return lambda i: 1 + (i * tm) // t

        front = (x, norm_g0, mod0, row_fn(ROW_TILE), w_in0_b, q_norm_g0, k_norm_g0)
        if smp:
            h, qp, kp, vp = _norm_mm_qkv0(*front, tab0, False)
            cache = (cache_k0.reshape(nb * PAST_LEN, B_KV_WIDTH), cache_v0.reshape(nb * PAST_LEN, B_KV_WIDTH))
            k0 = v0 = None
        else:
            h, qp, kp, vp, k0, v0 = _norm_mm_qkv0(*front, None, True)
            cache = None
        y_a = _mm_sgate(h, w_in0_b, ws_b, bs_b)
        gate_b = _mm(h, w_in0_b, BF16, 512, (G0_OFF // 512, 1, B_WIDTH // 512))
        y_b = _gqa(qp, kp, vp, cache, gate_b, nb, t, GQA_ROWS_SMP if smp else SEQ)
        x1, h = _out_proj([y_a, y_b], w_out0_b, x, mod0, row_fn(ROW_TILE), next_norm=(norm_g1, mod1))

        if smp:
            qk = _mm_rope(h, w_in1_b, tab1, (C_QK_DIM ** -0.5) * LOG2E)
            vg = _mm(h, w_in1_b, BF16, 1024, (2 * C_WIDTH // 1024, 1, 2 * C_WIDTH // 1024))
            cache = (cache_k1.reshape(nb * PAST_LEN, C_WIDTH), cache_v1.reshape(nb * PAST_LEN, C_WIDTH))
            hb = 1
            second = C_WIDTH // (hb * LANES)
            y_c = _diff((qk, 0), (qk, second), (vg, 0), (vg, second), cache, lam_p, subln_g1,
                        nb, t, t, hb, None)
            k1 = v1 = None
        else:
            qg = _mm(h, w_in1_b, BF16, C_WIDTH, (0, 3, 2))
            kb, k1 = _mm_kt(h, w_in1_b, 1, nb)
            v1 = _mm(h, w_in1_b, F32, C_WIDTH, (2, 1, 1), tm=512)
            hb = DIFF_HEADS_CTX
            y_c = _diff((qg, 0), (kb, 0), (v1, 0), (qg, C_WIDTH // (hb * LANES)), None, lam_p, subln_g1,
                        nb, t, SEQ, hb, (C_QK_DIM ** -0.5) * LOG2E)
        y = _out_proj([y_c], w_out1_b, x1, mod1, row_fn(ROW_TILE), final_g)
        return y, k0, v0, k1, v1

    y_p, k0, v0, k1, v1 = run(x_prompt.reshape(n_ctx * SEQ, D_MODEL), n_ctx, SEQ, False)
    y_s, _, _, _, _ = run(x_sample.reshape(n_smp * DEC_SEQ, D_MODEL), n_smp, DEC_SEQ, True)

    return (
        y_p.reshape(n_ctx, SEQ, D_MODEL),
        y_s.reshape(n_smp, DEC_SEQ, D_MODEL),
        k0.reshape(n_ctx, SEQ, B_KV_HEADS, B_HEAD_DIM),
        v0.reshape(n_ctx, SEQ, B_KV_HEADS, B_HEAD_DIM),
        k1.reshape(n_ctx, C_HEADS, 2, C_QK_DIM, SEQ).transpose(0, 4, 1, 2, 3),
        v1.reshape(n_ctx, SEQ, C_HEADS, C_V_DIM),
    )
```

```python
import functools
import math

import jax
import jax.numpy as jnp
from jax import lax
from jax.experimental import pallas as pl
from jax.experimental.pallas import tpu as pltpu

F32 = jnp.float32
BF16 = jnp.bfloat16

D_MODEL = 2048
SEQ = 256
DEC_SEQ = 4096
PAST_LEN = 256
GRID_W = 64
CHUNK = 128
ROPE_THETA = 10000.0
EPS = 1e-6

A_GROUPS = 8
A_WIDTH = 1024
B_HEADS = 8
B_KV_HEADS = 2
B_GROUP = B_HEADS // B_KV_HEADS
B_HEAD_DIM = 128
B_WIDTH = 1024
B_KV_WIDTH = 256
IN0_WIDTH = 3 * A_WIDTH + 2 * B_WIDTH + 2 * B_KV_WIDTH
C_HEADS = 16
C_QK_DIM = 64
C_V_DIM = 128
C_WIDTH = 2048
IN1_WIDTH = 4 * C_WIDTH
LAMBDA_INIT_1 = 0.8 - 0.6 * math.exp(-0.3 * 1)

LANES = 128
LOG2E = math.log2(math.e)
VMEM_LIMIT = 56 * 1024 * 1024

ROW_TILE = 512
GQA_ROWS_SMP = 1024
DIFF_HEADS_CTX = 16

Q0_OFF = 3 * A_WIDTH
K0_OFF = Q0_OFF + B_WIDTH
V0_OFF = K0_OFF + B_KV_WIDTH
G0_OFF = V0_OFF + B_KV_WIDTH


def _params(*sem):
    return pltpu.CompilerParams(dimension_semantics=sem, vmem_limit_bytes=VMEM_LIMIT)


def _silu(x):
    return x * jax.nn.sigmoid(x)


def _ada_kernel(c_ref, w_ref, b_ref, o_ref):
    s = _silu(c_ref[...]).astype(BF16)
    o_ref[...] = jnp.dot(s, w_ref[...].astype(BF16), preferred_element_type=F32) + b_ref[...]


def _ada(cond, w_ada, b_ada):
    bn = 512
    n = w_ada.shape[1]
    return pl.pallas_call(
        _ada_kernel,
        grid=(n // bn,),
        in_specs=[
            pl.BlockSpec((8, D_MODEL), lambda j: (0, 0)),
            pl.BlockSpec((D_MODEL, bn), lambda j: (0, j)),
            pl.BlockSpec((1, bn), lambda j: (0, j)),
        ],
        out_specs=pl.BlockSpec((8, bn), lambda j: (0, j)),
        out_shape=jax.ShapeDtypeStruct((8, n), F32),
        compiler_params=_params("parallel"),
        name="ada",
    )(cond, w_ada, b_ada.reshape(1, n))


def _mm_kernel(h_ref, w_ref, o_ref):
    o_ref[...] = jnp.dot(h_ref[...], w_ref[...], preferred_element_type=F32).astype(o_ref.dtype)


def _mm(h, w, out_dtype, tn, col_blocks=None, tm=1024):
    t, k = h.shape
    first, stride, count = col_blocks if col_blocks is not None else (0, 1, w.shape[1] // tn)
    assert t % tm == 0 and w.shape[1] % tn == 0
    return pl.pallas_call(
        _mm_kernel,
        grid=(t // tm, count),
        in_specs=[
            pl.BlockSpec((tm, k), lambda i, j: (i, 0)),
            pl.BlockSpec((k, tn), lambda i, j: (0, first + stride * j)),
        ],
        out_specs=pl.BlockSpec((tm, tn), lambda i, j: (i, j)),
        out_shape=jax.ShapeDtypeStruct((t, count * tn), out_dtype),
        compiler_params=_params("parallel", "parallel"),
        name="in_proj",
    )(h, w)


def _mm_kt_kernel(h_ref, w_ref, kb_ref, kt_ref):
    acc = jnp.dot(h_ref[...], w_ref[...], preferred_element_type=F32)
    kb_ref[...] = acc.astype(kb_ref.dtype)
    for b in range(kt_ref.shape[0]):
        kt_ref[b] = acc[b * SEQ:(b + 1) * SEQ, :].T


def _mm_kt(h, w, col_block, n_batch, tb=2):
    t, k = h.shape
    tm = tb * SEQ
    return pl.pallas_call(
        _mm_kt_kernel,
        grid=(t // tm,),
        in_specs=[
            pl.BlockSpec((tm, k), lambda i: (i, 0)),
            pl.BlockSpec((k, C_WIDTH), lambda i: (0, col_block)),
        ],
        out_specs=[
            pl.BlockSpec((tm, C_WIDTH), lambda i: (i, 0)),
            pl.BlockSpec((tb, C_WIDTH, SEQ), lambda i: (i, 0, 0)),
        ],
        out_shape=[
            jax.ShapeDtypeStruct((t, C_WIDTH), BF16),
            jax.ShapeDtypeStruct((n_batch, C_WIDTH, SEQ), F32),
        ],
        compiler_params=_params("parallel"),
        name="in_proj_kt",
    )(h, w)


def _mm_sgate_kernel(h_ref, w_ref, ws_ref, bs_ref, o_ref):
    z = jnp.dot(h_ref[...], w_ref[...], preferred_element_type=F32)
    v = z[:, A_WIDTH:2 * A_WIDTH]
    mu = jnp.mean(v, axis=-1, keepdims=True)
    vc = v - mu
    var = jnp.mean(vc * vc, axis=-1, keepdims=True)
    vn = (vc * lax.rsqrt(var + EPS)).astype(BF16)
    for ch in range(z.shape[0] // CHUNK):
        rows = slice(ch * CHUNK, (ch + 1) * CHUNK)
        for g in range(A_GROUPS):
            cols = slice(g * LANES, (g + 1) * LANES)
            s = jnp.dot(ws_ref[g], vn[rows, cols], preferred_element_type=F32) + bs_ref[g]
            u = z[rows, g * LANES:(g + 1) * LANES]
            gate = z[rows, 2 * A_WIDTH + g * LANES:2 * A_WIDTH + (g + 1) * LANES]
            o_ref[rows, cols] = (u * s * _silu(gate)).astype(o_ref.dtype)


def _mm_sgate(h, w, ws_b, bs_b, tm=512):
    t, k = h.shape
    full = pl.BlockSpec((A_GROUPS, CHUNK, CHUNK), lambda i: (0, 0, 0))
    return pl.pallas_call(
        _mm_sgate_kernel,
        grid=(t // tm,),
        in_specs=[
            pl.BlockSpec((tm, k), lambda i: (i, 0)),
            pl.BlockSpec((k, 3 * A_WIDTH), lambda i: (0, 0)),
            full, full,
        ],
        out_specs=pl.BlockSpec((tm, A_WIDTH), lambda i: (i, 0)),
        out_shape=jax.ShapeDtypeStruct((t, A_WIDTH), BF16),
        compiler_params=_params("parallel"),
        name="in_proj_sgate",
    )(h, w, ws_b, bs_b)


def _rope_tables(half):
    rows = DEC_SEQ // GRID_W
    row_pos = jnp.repeat(jnp.arange(rows, dtype=F32), GRID_W)
    col_pos = jnp.tile(jnp.arange(GRID_W, dtype=F32), rows)
    freqs = ROPE_THETA ** (-jnp.arange(half, dtype=F32) / half)
    ang_r = row_pos[:, None] * freqs[None, :]
    ang_c = col_pos[:, None] * freqs[None, :]
    cr, sr, cc, sc = jnp.cos(ang_r), jnp.sin(ang_r), jnp.cos(ang_c), jnp.sin(ang_c)
    z = jnp.zeros_like(sr)
    reps = LANES // (4 * half)
    cos_t = jnp.tile(jnp.concatenate([cr, cr, cc, cc], axis=-1), (1, reps))
    sin_a = jnp.tile(jnp.concatenate([z, sr, z, sc], axis=-1), (1, reps))
    sin_b = jnp.tile(jnp.concatenate([-sr, z, -sc, z], axis=-1), (1, reps))
    return cos_t, sin_a, sin_b


def _rope_partner_matrix(half):
    lane = jnp.arange(LANES)
    first = (lane % (2 * half)) < half
    src = jnp.where(first, lane + half, lane - half)
    sign = jnp.where(first, -1.0, 1.0)
    return jnp.zeros((LANES, LANES), F32).at[src, lane].set(sign).astype(BF16)


def _rope(x, cos_t, sin_a, sin_b, half):
    return x * cos_t + pltpu.roll(x, half, 1) * sin_a + pltpu.roll(x, LANES - half, 1) * sin_b


QKV0_SUB_ROWS = 256


def _mm_qkv0_kernel(*refs, rope, emit_f32):
    x_ref, ng_ref, shift_ref, scale_ref, w_ref, qg_ref, kg_ref = refs[:7]
    refs = refs[7:]
    if rope:
        cos_ref, sin_ref, perm_ref = refs[:3]
        refs = refs[3:]
    h_ref, qo_ref, ko_ref, vo_ref = refs[:4]
    if emit_f32:
        kf_ref, vf_ref = refs[4:6]

    def norm(x, g):
        ms = jnp.mean(x * x, axis=-1, keepdims=True)
        return x * lax.rsqrt(ms + EPS) * g

    def rot(x, rows):
        if not rope:
            return x
        partner = jnp.dot(x.astype(BF16), perm_ref[...], preferred_element_type=F32)
        return x * cos_ref[rows, :] + partner * sin_ref[rows, :]

    qscale = (B_HEAD_DIM ** -0.5) * LOG2E
    for r0 in range(0, x_ref.shape[0], QKV0_SUB_ROWS):
        rows = slice(r0, r0 + QKV0_SUB_ROWS)
        hs = (norm(x_ref[rows, :], ng_ref[...]) * (1.0 + scale_ref[...]) + shift_ref[...]).astype(BF16)
        h_ref[rows, :] = hs
        z = jnp.dot(hs, w_ref[...], preferred_element_type=F32)
        for h in range(B_HEADS):
            cols = slice(h * LANES, (h + 1) * LANES)
            qn = norm(z[:, cols], qg_ref[...])
            qo_ref[rows, cols] = (rot(qn, rows) * qscale).astype(qo_ref.dtype)
        for h in range(B_KV_HEADS):
            cols = slice(h * LANES, (h + 1) * LANES)
            kn = norm(z[:, B_WIDTH + h * LANES:B_WIDTH + (h + 1) * LANES], kg_ref[...])
            ko_ref[rows, cols] = rot(kn, rows).astype(ko_ref.dtype)
            if emit_f32:
                kf_ref[rows, cols] = kn
        v = z[:, B_WIDTH + B_KV_WIDTH:]
        vo_ref[rows, :] = v.astype(vo_ref.dtype)
        if emit_f32:
            vf_ref[rows, :] = v


def _norm_mm_qkv0(x, norm_g, mod3, row_of_tile, w, q_g, k_g, tables, emit_f32, tm=ROW_TILE):
    t, kdim = x.shape
    rope = tables is not None
    qkv_w = B_WIDTH + 2 * B_KV_WIDTH
    in_specs = [
        pl.BlockSpec((tm, kdim), lambda i: (i, 0)),
        pl.BlockSpec((1, kdim), lambda i: (0, 0)),
        pl.BlockSpec((None, 1, kdim), lambda i: (row_of_tile(i), 0, 0)),
        pl.BlockSpec((None, 1, kdim), lambda i: (row_of_tile(i), 0, 1)),
        pl.BlockSpec((kdim, qkv_w), lambda i: (0, Q0_OFF // qkv_w)),
        pl.BlockSpec((1, LANES), lambda i: (0, 0)),
        pl.BlockSpec((1, LANES), lambda i: (0, 0)),
    ]
    args = [x, norm_g.reshape(1, kdim), mod3, mod3, w, q_g.reshape(1, LANES), k_g.reshape(1, LANES)]
    if rope:
        nt = DEC_SEQ // tm
        in_specs += [pl.BlockSpec((tm, LANES), lambda i: (i % nt, 0))] * 2
        in_specs.append(pl.BlockSpec((LANES, LANES), lambda i: (0, 0)))
        args += list(tables)
    out_specs = [
        pl.BlockSpec((tm, kdim), lambda i: (i, 0)),
        pl.BlockSpec((tm, B_WIDTH), lambda i: (i, 0)),
        pl.BlockSpec((tm, B_KV_WIDTH), lambda i: (i, 0)),
        pl.BlockSpec((tm, B_KV_WIDTH), lambda i: (i, 0)),
    ]
    out_shape = [
        jax.ShapeDtypeStruct((t, kdim), BF16),
        jax.ShapeDtypeStruct((t, B_WIDTH), BF16),
        jax.ShapeDtypeStruct((t, B_KV_WIDTH), BF16),
        jax.ShapeDtypeStruct((t, B_KV_WIDTH), BF16),
    ]
    if emit_f32:
        out_specs += [pl.BlockSpec((tm, B_KV_WIDTH), lambda i: (i, 0))] * 2
        out_shape += [jax.ShapeDtypeStruct((t, B_KV_WIDTH), F32)] * 2
    assert Q0_OFF % qkv_w == 0
    return pl.pallas_call(
        functools.partial(_mm_qkv0_kernel, rope=rope, emit_f32=emit_f32),
        grid=(t // tm,),
        in_specs=in_specs,
        out_specs=out_specs,
        out_shape=out_shape,
        compiler_params=_params("parallel"),
        name="in_proj_qkv",
    )(*args)


ROPE_SUB_COLS = 1024


def _mm_rope_kernel(h_ref, w_ref, cos_ref, sa_ref, sb_ref, o_ref, *, q_tiles, q_scale):
    scale = jnp.where(pl.program_id(1) < q_tiles, q_scale, 1.0)
    cos_t, sa, sb = cos_ref[...], sa_ref[...], sb_ref[...]
    h = h_ref[...]
    for c0 in range(0, w_ref.shape[1], ROPE_SUB_COLS):
        acc = jnp.dot(h, w_ref[:, c0:c0 + ROPE_SUB_COLS], preferred_element_type=F32)
        for c in range(ROPE_SUB_COLS // LANES):
            x = acc[:, c * LANES:(c + 1) * LANES]
            cols = slice(c0 + c * LANES, c0 + (c + 1) * LANES)
            o_ref[:, cols] = (_rope(x, cos_t, sa, sb, C_QK_DIM // 4) * scale).astype(o_ref.dtype)


def _mm_rope(h, w, tables, q_scale, tm=1024, tn=C_WIDTH):
    t, k = h.shape
    nt = DEC_SEQ // tm
    tab = pl.BlockSpec((tm, LANES), lambda i, j: (i % nt, 0))
    return pl.pallas_call(
        functools.partial(_mm_rope_kernel, q_tiles=C_WIDTH // tn, q_scale=q_scale),
        grid=(t // tm, 2 * C_WIDTH // tn),
        in_specs=[
            pl.BlockSpec((tm, k), lambda i, j: (i, 0)),
            pl.BlockSpec((k, tn), lambda i, j: (0, j)),
            tab, tab, tab,
        ],
        out_specs=pl.BlockSpec((tm, tn), lambda i, j: (i, j)),
        out_shape=jax.ShapeDtypeStruct((t, 2 * C_WIDTH), BF16),
        compiler_params=_params("parallel", "parallel"),
        name="in_proj_rope",
    )(h, w, *tables)


KEY_CHUNK = 256
SUBLANES = 8


MAX_KEY_BLOCK = 1024


def _key_blocks(s_new, has_cache):
    size = min(s_new, MAX_KEY_BLOCK)
    blocks = [(r, size) for r in range(0, s_new, size)]
    if has_cache:
        blocks.append((s_new, PAST_LEN))
    return blocks


def _attn_pipeline(units, blocks, s_scr, p_scr):
    n = s_scr.shape[2]
    st = [dict() for _ in units]

    def stage1(u, b):
        r0, size = blocks[b]
        if b == 0:
            st[u]["qq"] = units[u][0]()
        s = lax.dot_general(units[u][1](b), st[u]["qq"], (((1,), (1,)), ((), ())), preferred_element_type=F32)
        s_scr[u % 2, r0:r0 + size, :] = s
        m8 = s.reshape(size // SUBLANES, SUBLANES, n).max(axis=0)
        st[u]["m8"] = m8 if b == 0 else jnp.maximum(st[u]["m8"], m8)
        if b == len(blocks) - 1:
            st[u]["m"] = st[u]["m8"].max(axis=0, keepdims=True)

    def stage2(u, b):
        r0, size = blocks[b]
        for r in range(r0, r0 + size, KEY_CHUNK):
            p = jnp.exp2(s_scr[u % 2, r:r + KEY_CHUNK, :] - st[u]["m"])
            l8 = p.reshape(KEY_CHUNK // SUBLANES, SUBLANES, n).sum(axis=0)
            st[u]["l8"] = l8 if r == 0 else st[u]["l8"] + l8
            p_scr[u % 2, r:r + KEY_CHUNK, :] = p.astype(BF16)

    def stage3(u, b):
        r0, size = blocks[b]
        a = jnp.dot(units[u][2][:, r0:r0 + size], p_scr[u % 2, r0:r0 + size, :], preferred_element_type=F32)
        st[u]["acc"] = a if b == 0 else st[u]["acc"] + a
        if b == len(blocks) - 1:
            l = st[u]["l8"].sum(axis=0, keepdims=True)
            units[u][3](st[u]["acc"] * (1.0 / l))

    for slot in range(len(units) + 2):
        for b in range(len(blocks)):
            if 0 <= slot - 2 < len(units):
                stage3(slot - 2, b)
            if slot < len(units):
                stage1(slot, b)
            if 0 <= slot - 1 < len(units):
                stage2(slot - 1, b)


def _fill_vt(vt_scr, v_ref, vc_ref, n_kv):
    s_new = v_ref.shape[0]
    for j in range(n_kv):
        cols = slice(j * LANES, (j + 1) * LANES)
        vt_scr[j, :, 0:s_new] = v_ref[:, cols].astype(F32).T.astype(BF16)
        if vc_ref is not None:
            vt_scr[j, :, s_new:] = vc_ref[:, cols].astype(F32).T.astype(BF16)


def _attn_scratch(n_kv, s_tot, n):
    return [
        pltpu.VMEM((n_kv, LANES, s_tot), BF16),
        pltpu.VMEM((2, s_tot, n), F32),
        pltpu.VMEM((2, s_tot, n), BF16),
    ]


TQ = 256


def _gqa_kernel(*refs, has_cache, n_kv):
    q_ref, k_ref, v_ref = refs[:3]
    refs = refs[3:]
    kc_ref = vc_ref = None
    if has_cache:
        (kc_ref, vc_ref), refs = refs[:2], refs[2:]
    g_ref, o_ref, vt_scr, s_scr, p_scr = refs
    blocks = _key_blocks(k_ref.shape[0], has_cache)

    @pl.when(pl.program_id(2) == 0)
    def _():
        _fill_vt(vt_scr, v_ref, vc_ref, n_kv)

    def make_unit(rows, j, heads):
        kcols = slice(j * LANES, (j + 1) * LANES)

        def make_qq():
            return jnp.concatenate([q_ref[rows, h * LANES:(h + 1) * LANES] for h in heads], axis=0)

        def key_block(b):
            r0, size = blocks[b]
            if r0 < k_ref.shape[0]:
                return k_ref[r0:r0 + size, kcols].astype(BF16)
            return kc_ref[:, kcols].astype(BF16)

        def finish(ot):
            for i, h in enumerate(heads):
                cols = slice(h * LANES, (h + 1) * LANES)
                gate = g_ref[rows, cols].astype(F32)
                o_ref[rows, cols] = (ot[:, i * TQ:(i + 1) * TQ].T * _silu(gate)).astype(o_ref.dtype)

        return make_qq, key_block, vt_scr.at[j], finish

    units = [make_unit(slice(r, r + TQ), j, (j * B_GROUP + h, j * B_GROUP + h + 1))
             for r in range(0, q_ref.shape[0], TQ) for j in range(n_kv) for h in range(0, B_GROUP, 2)]
    _attn_pipeline(units, blocks, s_scr, p_scr)


def _gqa(qp, kp, vp, cache, gate, nb, t, tqs, n_kv):
    nq = t // tqs
    qw, kw = n_kv * B_GROUP * LANES, n_kv * LANES
    s_tot = t + (PAST_LEN if cache is not None else 0)
    in_specs = [
        pl.BlockSpec((tqs, qw), lambda b, h, i: (b * nq + i, h)),
        pl.BlockSpec((t, kw), lambda b, h, i: (b, h)),
        pl.BlockSpec((t, kw), lambda b, h, i: (b, h)),
    ]
    args = [qp, kp, vp]
    if cache is not None:
        in_specs += [pl.BlockSpec((PAST_LEN, kw), lambda b, h, i: (b, h))] * 2
        args += list(cache)
    in_specs.append(pl.BlockSpec((tqs, qw), lambda b, h, i: (b * nq + i, h)))
    args.append(gate)
    return pl.pallas_call(
        functools.partial(_gqa_kernel, has_cache=cache is not None, n_kv=n_kv),
        grid=(nb, B_KV_HEADS // n_kv, nq),
        in_specs=in_specs,
        out_specs=pl.BlockSpec((tqs, qw), lambda b, h, i: (b * nq + i, h)),
        out_shape=jax.ShapeDtypeStruct((nb * t, B_WIDTH), BF16),
        scratch_shapes=_attn_scratch(n_kv, s_tot, 2 * TQ),
        compiler_params=_params("parallel", "parallel", "arbitrary"),
        name="gqa_attn",
    )(*args)


def _diff_kernel(*refs, has_cache, hb, q_scale):
    q_ref, k_ref, v_ref = refs[:3]
    refs = refs[3:]
    kc_ref = vc_ref = None
    if has_cache:
        (kc_ref, vc_ref), refs = refs[:2], refs[2:]
    g_ref, lam_ref, sg_ref, o_ref, vt_scr, s_scr, p_scr = refs
    blocks = _key_blocks(k_ref.shape[0], has_cache)

    @pl.when(pl.program_id(2) == 0)
    def _():
        _fill_vt(vt_scr, v_ref, vc_ref, hb)

    lp = lam_ref[...]
    lam = (jnp.exp(jnp.sum(lp[0:1] * lp[1:2], axis=-1, keepdims=True))
           - jnp.exp(jnp.sum(lp[2:3] * lp[3:4], axis=-1, keepdims=True)) + LAMBDA_INIT_1)
    lane = lax.broadcasted_iota(jnp.int32, (TQ, LANES), 1)

    def make_unit(rows, j):
        cols = slice(j * LANES, (j + 1) * LANES)

        def make_qq():
            q = q_ref[rows, cols]
            if q_scale is not None:
                q = q.astype(F32) * q_scale
            zero = jnp.zeros_like(q)
            maps = [jnp.where(lane < C_QK_DIM, q, zero), jnp.where(lane >= C_QK_DIM, q, zero)]
            return jnp.concatenate(maps, axis=0).astype(BF16)

        def key_block(b):
            r0, size = blocks[b]
            if r0 < k_ref.shape[0]:
                return k_ref[r0:r0 + size, cols].astype(BF16)
            return kc_ref[:, cols].astype(BF16)

        def finish(ot):
            ot = ot[:, :TQ] - lam * ot[:, TQ:]
            ms = jnp.mean(ot * ot, axis=0, keepdims=True)
            o = (ot * lax.rsqrt(ms + EPS)).T * sg_ref[...] * (1.0 - LAMBDA_INIT_1)
            gate = g_ref[rows, cols].astype(F32)
            o_ref[rows, cols] = (o * _silu(gate)).astype(o_ref.dtype)

        return make_qq, key_block, vt_scr.at[j], finish

    units = [make_unit(slice(r, r + TQ), j) for j in range(hb) for r in range(0, q_ref.shape[0], TQ)]
    _attn_pipeline(units, blocks, s_scr, p_scr)


def _diff(q, k, v, g, cache, lam_p, sub_g, nb, t, tq, hb, q_scale):
    nq = t // tq
    w = hb * LANES
    s_tot = t + (PAST_LEN if cache is not None else 0)
    in_specs = [
        pl.BlockSpec((tq, w), lambda b, h, i: (b * nq + i, q[1] + h)),
        pl.BlockSpec((t, w), lambda b, h, i: (b, k[1] + h)),
        pl.BlockSpec((t, w), lambda b, h, i: (b, v[1] + h)),
    ]
    args = [q[0], k[0], v[0]]
    if cache is not None:
        in_specs += [pl.BlockSpec((PAST_LEN, w), lambda b, h, i: (b, h))] * 2
        args += list(cache)
    in_specs += [
        pl.BlockSpec((tq, w), lambda b, h, i: (b * nq + i, g[1] + h)),
        pl.BlockSpec((4, C_QK_DIM), lambda b, h, i: (0, 0)),
        pl.BlockSpec((1, LANES), lambda b, h, i: (0, 0)),
    ]
    args += [g[0], lam_p, sub_g.reshape(1, LANES)]
    return pl.pallas_call(
        functools.partial(_diff_kernel, has_cache=cache is not None, hb=hb, q_scale=q_scale),
        grid=(nb, C_HEADS // hb, nq),
        in_specs=in_specs,
        out_specs=pl.BlockSpec((tq, w), lambda b, h, i: (b * nq + i, h)),
        out_shape=jax.ShapeDtypeStruct((nb * t, C_WIDTH), BF16),
        scratch_shapes=_attn_scratch(hb, s_tot, 2 * TQ),
        compiler_params=_params("parallel", "parallel", "arbitrary"),
        name="diff_attn",
    )(*args)


def _out_kernel(*refs, n_in, final):
    y_refs = refs[:n_in]
    w_ref, x_ref, gate_ref = refs[n_in:n_in + 3]
    refs = refs[n_in + 3:]
    half = x_ref.shape[0] // 2
    for rows in (slice(0, half), slice(half, 2 * half)):
        acc = None
        k0 = 0
        for y_ref in y_refs:
            kk = y_ref.shape[1]
            a = jnp.dot(y_ref[rows, :], w_ref[k0:k0 + kk, :], preferred_element_type=F32)
            acc = a if acc is None else acc + a
            k0 += kk
        x = x_ref[rows, :] + gate_ref[...] * acc
        xn = x * lax.rsqrt(jnp.mean(x * x, axis=-1, keepdims=True) + EPS)
        if final:
            fg_ref, o_ref = refs
            o_ref[rows, :] = xn * fg_ref[...]
        else:
            g_ref, shift_ref, scale_ref, o_ref, h_ref = refs
            o_ref[rows, :] = x
            h_ref[rows, :] = (xn * g_ref[...] * (1.0 + scale_ref[...]) + shift_ref[...]).astype(h_ref.dtype)


def _out_proj(ys, w, x, mod3, row_of_tile, final_g=None, next_norm=None, tm=ROW_TILE):
    t = x.shape[0]
    final = final_g is not None
    vec = pl.BlockSpec((1, D_MODEL), lambda i: (0, 0))
    mod_row = lambda part: pl.BlockSpec((None, 1, D_MODEL), lambda i: (row_of_tile(i), 0, part))
    in_specs = [pl.BlockSpec((tm, y.shape[1]), lambda i: (i, 0)) for y in ys]
    in_specs += [
        pl.BlockSpec(w.shape, lambda i: (0, 0)),
        pl.BlockSpec((tm, D_MODEL), lambda i: (i, 0)),
        mod_row(2),
    ]
    args = list(ys) + [w, x, mod3]
    row_blk = pl.BlockSpec((tm, D_MODEL), lambda i: (i, 0))
    if final:
        in_specs.append(vec)
        args.append(final_g.reshape(1, D_MODEL))
        out_specs, out_shape = row_blk, jax.ShapeDtypeStruct((t, D_MODEL), F32)
    else:
        next_g, next_mod3 = next_norm
        in_specs += [vec, mod_row(0), mod_row(1)]
        args += [next_g.reshape(1, D_MODEL), next_mod3, next_mod3]
        out_specs = [row_blk, row_blk]
        out_shape = [jax.ShapeDtypeStruct((t, D_MODEL), F32), jax.ShapeDtypeStruct((t, D_MODEL), BF16)]
    return pl.pallas_call(
        functools.partial(_out_kernel, n_in=len(ys), final=final),
        grid=(t // tm,),
        in_specs=in_specs,
        out_specs=out_specs,
        out_shape=out_shape,
        compiler_params=_params("parallel"),
        name="out_proj",
    )(*args)


def kernel(x_prompt, x_sample, cache_k0, cache_v0, cache_k1, cache_v1, c, c_ctx, w_ada0, b_ada0, norm_g0, w_in0, w_s0, b_s0, q_norm_g0, k_norm_g0, w_out0, w_ada1, b_ada1, norm_g1, w_in1, lambda_q1, lambda_k1, lambda_q2, lambda_k2, subln_g1, w_out1, final_g):
    n_ctx, n_smp = x_prompt.shape[0], x_sample.shape[0]

    cond = jnp.concatenate([c_ctx[None], c, jnp.zeros((8 - 1 - n_smp, D_MODEL), F32)], axis=0)
    mod0 = _ada(cond, w_ada0, b_ada0).reshape(8, 1, 3 * D_MODEL)
    mod1 = _ada(cond, w_ada1, b_ada1).reshape(8, 1, 3 * D_MODEL)

    w_in0_b, w_out0_b = w_in0.astype(BF16), w_out0.astype(BF16)
    w_in1_b, w_out1_b = w_in1.astype(BF16), w_out1.astype(BF16)
    ws_b = w_s0.astype(BF16)
    bs_b = jnp.broadcast_to(b_s0[:, :, None], (A_GROUPS, CHUNK, LANES))
    lam_p = jnp.stack([lambda_q1, lambda_k1, lambda_q2, lambda_k2]).astype(F32)
    cos0, sin_a0, sin_b0 = _rope_tables(B_HEAD_DIM // 4)
    tab0 = (cos0, sin_a0 - sin_b0, _rope_partner_matrix(B_HEAD_DIM // 4))
    tab1 = _rope_tables(C_QK_DIM // 4)

    def run(x, nb, t, smp):
        def row_fn(tm):
            if not smp:
                return lambda i: 0
            return lambda i: 1 + (i * tm) // t

        front = (x, norm_g0, mod0, row_fn(ROW_TILE), w_in0_b, q_norm_g0, k_norm_g0)
        if smp:
            h, qp, kp, vp = _norm_mm_qkv0(*front, tab0, False)
            cache = (cache_k0.reshape(nb * PAST_LEN, B_KV_WIDTH), cache_v0.reshape(nb * PAST_LEN, B_KV_WIDTH))
            k0 = v0 = None
        else:
            h, qp, kp, vp, k0, v0 = _norm_mm_qkv0(*front, None, True)
            cache = None
        y_a = _mm_sgate(h, w_in0_b, ws_b, bs_b)
        gate_b = _mm(h, w_in0_b, BF16, 512, (G0_OFF // 512, 1, B_WIDTH // 512))
        y_b = _gqa(qp, kp, vp, cache, gate_b, nb, t, GQA_ROWS_SMP if smp else SEQ, 1 if smp else B_KV_HEADS)
        x1, h = _out_proj([y_a, y_b], w_out0_b, x, mod0, row_fn(ROW_TILE), next_norm=(norm_g1, mod1))

        if smp:
            qk = _mm_rope(h, w_in1_b, tab1, (C_QK_DIM ** -0.5) * LOG2E)
            vg = _mm(h, w_in1_b, BF16, 1024, (2 * C_WIDTH // 1024, 1, 2 * C_WIDTH // 1024))
            cache = (cache_k1.reshape(nb * PAST_LEN, C_WIDTH), cache_v1.reshape(nb * PAST_LEN, C_WIDTH))
            hb = 1
            second = C_WIDTH // (hb * LANES)
            y_c = _diff((qk, 0), (qk, second), (vg, 0), (vg, second), cache, lam_p, subln_g1,
                        nb, t, t, hb, None)
            k1 = v1 = None
        else:
            qg = _mm(h, w_in1_b, BF16, C_WIDTH, (0, 3, 2))
            kb, k1 = _mm_kt(h, w_in1_b, 1, nb)
            v1 = _mm(h, w_in1_b, F32, C_WIDTH, (2, 1, 1), tm=512)
            hb = DIFF_HEADS_CTX
            y_c = _diff((qg, 0), (kb, 0), (v1, 0), (qg, C_WIDTH // (hb * LANES)), None, lam_p, subln_g1,
                        nb, t, SEQ, hb, (C_QK_DIM ** -0.5) * LOG2E)
        y = _out_proj([y_c], w_out1_b, x1, mod1, row_fn(ROW_TILE), final_g)
        return y, k0, v0, k1, v1

    y_p, k0, v0, k1, v1 = run(x_prompt.reshape(n_ctx * SEQ, D_MODEL), n_ctx, SEQ, False)
    y_s, _, _, _, _ = run(x_sample.reshape(n_smp * DEC_SEQ, D_MODEL), n_smp, DEC_SEQ, True)

    return (
        y_p.reshape(n_ctx, SEQ, D_MODEL),
        y_s.reshape(n_smp, DEC_SEQ, D_MODEL),
        k0.reshape(n_ctx, SEQ, B_KV_HEADS, B_HEAD_DIM),
        v0.reshape(n_ctx, SEQ, B_KV_HEADS, B_HEAD_DIM),
        k1.reshape(n_ctx, C_HEADS, 2, C_QK_DIM, SEQ).transpose(0, 4, 1, 2, 3),
        v1.reshape(n_ctx, SEQ, C_HEADS, C_V_DIM),
    )
```

```python
import functools
import math

import jax
import jax.numpy as jnp
from jax import lax
from jax.experimental import pallas as pl
from jax.experimental.pallas import tpu as pltpu

F32 = jnp.float32
BF16 = jnp.bfloat16

D_MODEL = 2048
SEQ = 256
DEC_SEQ = 4096
PAST_LEN = 256
GRID_W = 64
CHUNK = 128
ROPE_THETA = 10000.0
EPS = 1e-6

A_GROUPS = 8
A_WIDTH = 1024
B_HEADS = 8
B_KV_HEADS = 2
B_GROUP = B_HEADS // B_KV_HEADS
B_HEAD_DIM = 128
B_WIDTH = 1024
B_KV_WIDTH = 256
IN0_WIDTH = 3 * A_WIDTH + 2 * B_WIDTH + 2 * B_KV_WIDTH
C_HEADS = 16
C_QK_DIM = 64
C_V_DIM = 128
C_WIDTH = 2048
IN1_WIDTH = 4 * C_WIDTH
LAMBDA_INIT_1 = 0.8 - 0.6 * math.exp(-0.3 * 1)

LANES = 128
LOG2E = math.log2(math.e)
VMEM_LIMIT = 56 * 1024 * 1024

ROW_TILE = 512
GQA_ROWS_SMP = 1024
DIFF_HEADS_CTX = 16

Q0_OFF = 3 * A_WIDTH
K0_OFF = Q0_OFF + B_WIDTH
V0_OFF = K0_OFF + B_KV_WIDTH
G0_OFF = V0_OFF + B_KV_WIDTH


def _params(*sem):
    return pltpu.CompilerParams(dimension_semantics=sem, vmem_limit_bytes=VMEM_LIMIT)


def _silu(x):
    return x * jax.nn.sigmoid(x)


def _ada_kernel(c_ref, w_ref, b_ref, o_ref):
    s = _silu(c_ref[...]).astype(BF16)
    o_ref[...] = jnp.dot(s, w_ref[...].astype(BF16), preferred_element_type=F32) + b_ref[...]


def _ada(cond, w_ada, b_ada):
    bn = 512
    n = w_ada.shape[1]
    return pl.pallas_call(
        _ada_kernel,
        grid=(n // bn,),
        in_specs=[
            pl.BlockSpec((8, D_MODEL), lambda j: (0, 0)),
            pl.BlockSpec((D_MODEL, bn), lambda j: (0, j)),
            pl.BlockSpec((1, bn), lambda j: (0, j)),
        ],
        out_specs=pl.BlockSpec((8, bn), lambda j: (0, j)),
        out_shape=jax.ShapeDtypeStruct((8, n), F32),
        compiler_params=_params("parallel"),
        name="ada",
    )(cond, w_ada, b_ada.reshape(1, n))


def _mm_kernel(h_ref, w_ref, o_ref):
    o_ref[...] = jnp.dot(h_ref[...], w_ref[...], preferred_element_type=F32).astype(o_ref.dtype)


def _mm(h, w, out_dtype, tn, col_blocks=None, tm=1024):
    t, k = h.shape
    first, stride, count = col_blocks if col_blocks is not None else (0, 1, w.shape[1] // tn)
    assert t % tm == 0 and w.shape[1] % tn == 0
    return pl.pallas_call(
        _mm_kernel,
        grid=(t // tm, count),
        in_specs=[
            pl.BlockSpec((tm, k), lambda i, j: (i, 0)),
            pl.BlockSpec((k, tn), lambda i, j: (0, first + stride * j)),
        ],
        out_specs=pl.BlockSpec((tm, tn), lambda i, j: (i, j)),
        out_shape=jax.ShapeDtypeStruct((t, count * tn), out_dtype),
        compiler_params=_params("parallel", "parallel"),
        name="in_proj",
    )(h, w)


def _mm_kt_kernel(h_ref, w_ref, kb_ref, kt_ref):
    acc = jnp.dot(h_ref[...], w_ref[...], preferred_element_type=F32)
    kb_ref[...] = acc.astype(kb_ref.dtype)
    for b in range(kt_ref.shape[0]):
        kt_ref[b] = acc[b * SEQ:(b + 1) * SEQ, :].T


def _mm_kt(h, w, col_block, n_batch, tb=2):
    t, k = h.shape
    tm = tb * SEQ
    return pl.pallas_call(
        _mm_kt_kernel,
        grid=(t // tm,),
        in_specs=[
            pl.BlockSpec((tm, k), lambda i: (i, 0)),
            pl.BlockSpec((k, C_WIDTH), lambda i: (0, col_block)),
        ],
        out_specs=[
            pl.BlockSpec((tm, C_WIDTH), lambda i: (i, 0)),
            pl.BlockSpec((tb, C_WIDTH, SEQ), lambda i: (i, 0, 0)),
        ],
        out_shape=[
            jax.ShapeDtypeStruct((t, C_WIDTH), BF16),
            jax.ShapeDtypeStruct((n_batch, C_WIDTH, SEQ), F32),
        ],
        compiler_params=_params("parallel"),
        name="in_proj_kt",
    )(h, w)


def _mm_sgate_kernel(h_ref, w_ref, ws_ref, bs_ref, o_ref):
    z = jnp.dot(h_ref[...], w_ref[...], preferred_element_type=F32)
    v = z[:, A_WIDTH:2 * A_WIDTH]
    mu = jnp.mean(v, axis=-1, keepdims=True)
    vc = v - mu
    var = jnp.mean(vc * vc, axis=-1, keepdims=True)
    vn = (vc * lax.rsqrt(var + EPS)).astype(BF16)
    for ch in range(z.shape[0] // CHUNK):
        rows = slice(ch * CHUNK, (ch + 1) * CHUNK)
        for g in range(A_GROUPS):
            cols = slice(g * LANES, (g + 1) * LANES)
            s = jnp.dot(ws_ref[g], vn[rows, cols], preferred_element_type=F32) + bs_ref[g]
            u = z[rows, g * LANES:(g + 1) * LANES]
            gate = z[rows, 2 * A_WIDTH + g * LANES:2 * A_WIDTH + (g + 1) * LANES]
            o_ref[rows, cols] = (u * s * _silu(gate)).astype(o_ref.dtype)


def _mm_sgate(h, w, ws_b, bs_b, tm=1024):
    t, k = h.shape
    full = pl.BlockSpec((A_GROUPS, CHUNK, CHUNK), lambda i: (0, 0, 0))
    return pl.pallas_call(
        _mm_sgate_kernel,
        grid=(t // tm,),
        in_specs=[
            pl.BlockSpec((tm, k), lambda i: (i, 0)),
            pl.BlockSpec((k, 3 * A_WIDTH), lambda i: (0, 0), pipeline_mode=pl.Buffered(1)),
            full, full,
        ],
        out_specs=pl.BlockSpec((tm, A_WIDTH), lambda i: (i, 0)),
        out_shape=jax.ShapeDtypeStruct((t, A_WIDTH), BF16),
        compiler_params=_params("parallel"),
        name="in_proj_sgate",
    )(h, w, ws_b, bs_b)


def _rope_tables(half):
    rows = DEC_SEQ // GRID_W
    row_pos = jnp.repeat(jnp.arange(rows, dtype=F32), GRID_W)
    col_pos = jnp.tile(jnp.arange(GRID_W, dtype=F32), rows)
    freqs = ROPE_THETA ** (-jnp.arange(half, dtype=F32) / half)
    ang_r = row_pos[:, None] * freqs[None, :]
    ang_c = col_pos[:, None] * freqs[None, :]
    cr, sr, cc, sc = jnp.cos(ang_r), jnp.sin(ang_r), jnp.cos(ang_c), jnp.sin(ang_c)
    z = jnp.zeros_like(sr)
    reps = LANES // (4 * half)
    cos_t = jnp.tile(jnp.concatenate([cr, cr, cc, cc], axis=-1), (1, reps))
    sin_a = jnp.tile(jnp.concatenate([z, sr, z, sc], axis=-1), (1, reps))
    sin_b = jnp.tile(jnp.concatenate([-sr, z, -sc, z], axis=-1), (1, reps))
    return cos_t, sin_a, sin_b


def _rope_partner_matrix(half):
    lane = jnp.arange(LANES)
    first = (lane % (2 * half)) < half
    src = jnp.where(first, lane + half, lane - half)
    sign = jnp.where(first, -1.0, 1.0)
    return jnp.zeros((LANES, LANES), F32).at[src, lane].set(sign).astype(BF16)


def _rope(x, cos_t, sin_a, sin_b, half):
    return x * cos_t + pltpu.roll(x, half, 1) * sin_a + pltpu.roll(x, LANES - half, 1) * sin_b


QKV0_SUB_ROWS = 256


def _mm_qkv0_kernel(*refs, rope, emit_f32):
    x_ref, ng_ref, shift_ref, scale_ref, w_ref, qg_ref, kg_ref = refs[:7]
    refs = refs[7:]
    if rope:
        cos_ref, sin_ref, perm_ref = refs[:3]
        refs = refs[3:]
    h_ref, qo_ref, ko_ref, vo_ref = refs[:4]
    if emit_f32:
        kf_ref, vf_ref = refs[4:6]

    def norm(x, g):
        ms = jnp.mean(x * x, axis=-1, keepdims=True)
        return x * lax.rsqrt(ms + EPS) * g

    def rot(x, rows):
        if not rope:
            return x
        partner = jnp.dot(x.astype(BF16), perm_ref[...], preferred_element_type=F32)
        return x * cos_ref[rows, :] + partner * sin_ref[rows, :]

    qscale = (B_HEAD_DIM ** -0.5) * LOG2E
    for r0 in range(0, x_ref.shape[0], QKV0_SUB_ROWS):
        rows = slice(r0, r0 + QKV0_SUB_ROWS)
        hs = (norm(x_ref[rows, :], ng_ref[...]) * (1.0 + scale_ref[...]) + shift_ref[...]).astype(BF16)
        h_ref[rows, :] = hs
        z = jnp.dot(hs, w_ref[...], preferred_element_type=F32)
        for h in range(B_HEADS):
            cols = slice(h * LANES, (h + 1) * LANES)
            qn = norm(z[:, cols], qg_ref[...])
            qo_ref[rows, cols] = (rot(qn, rows) * qscale).astype(qo_ref.dtype)
        for h in range(B_KV_HEADS):
            cols = slice(h * LANES, (h + 1) * LANES)
            kn = norm(z[:, B_WIDTH + h * LANES:B_WIDTH + (h + 1) * LANES], kg_ref[...])
            ko_ref[rows, cols] = rot(kn, rows).astype(ko_ref.dtype)
            if emit_f32:
                kf_ref[rows, cols] = kn
        v = z[:, B_WIDTH + B_KV_WIDTH:]
        vo_ref[rows, :] = v.astype(vo_ref.dtype)
        if emit_f32:
            vf_ref[rows, :] = v


def _norm_mm_qkv0(x, norm_g, mod3, row_of_tile, w, q_g, k_g, tables, emit_f32, tm=ROW_TILE):
    t, kdim = x.shape
    rope = tables is not None
    qkv_w = B_WIDTH + 2 * B_KV_WIDTH
    in_specs = [
        pl.BlockSpec((tm, kdim), lambda i: (i, 0)),
        pl.BlockSpec((1, kdim), lambda i: (0, 0)),
        pl.BlockSpec((None, 1, kdim), lambda i: (row_of_tile(i), 0, 0)),
        pl.BlockSpec((None, 1, kdim), lambda i: (row_of_tile(i), 0, 1)),
        pl.BlockSpec((kdim, qkv_w), lambda i: (0, Q0_OFF // qkv_w)),
        pl.BlockSpec((1, LANES), lambda i: (0, 0)),
        pl.BlockSpec((1, LANES), lambda i: (0, 0)),
    ]
    args = [x, norm_g.reshape(1, kdim), mod3, mod3, w, q_g.reshape(1, LANES), k_g.reshape(1, LANES)]
    if rope:
        nt = DEC_SEQ // tm
        in_specs += [pl.BlockSpec((tm, LANES), lambda i: (i % nt, 0))] * 2
        in_specs.append(pl.BlockSpec((LANES, LANES), lambda i: (0, 0)))
        args += list(tables)
    out_specs = [
        pl.BlockSpec((tm, kdim), lambda i: (i, 0)),
        pl.BlockSpec((tm, B_WIDTH), lambda i: (i, 0)),
        pl.BlockSpec((tm, B_KV_WIDTH), lambda i: (i, 0)),
        pl.BlockSpec((tm, B_KV_WIDTH), lambda i: (i, 0)),
    ]
    out_shape = [
        jax.ShapeDtypeStruct((t, kdim), BF16),
        jax.ShapeDtypeStruct((t, B_WIDTH), BF16),
        jax.ShapeDtypeStruct((t, B_KV_WIDTH), BF16),
        jax.ShapeDtypeStruct((t, B_KV_WIDTH), BF16),
    ]
    if emit_f32:
        out_specs += [pl.BlockSpec((tm, B_KV_WIDTH), lambda i: (i, 0))] * 2
        out_shape += [jax.ShapeDtypeStruct((t, B_KV_WIDTH), F32)] * 2
    assert Q0_OFF % qkv_w == 0
    return pl.pallas_call(
        functools.partial(_mm_qkv0_kernel, rope=rope, emit_f32=emit_f32),
        grid=(t // tm,),
        in_specs=in_specs,
        out_specs=out_specs,
        out_shape=out_shape,
        compiler_params=_params("parallel"),
        name="in_proj_qkv",
    )(*args)


ROPE_SUB_COLS = 1024


def _mm_rope_kernel(h_ref, w_ref, cos_ref, sa_ref, sb_ref, o_ref, *, q_tiles, q_scale):
    scale = jnp.where(pl.program_id(1) < q_tiles, q_scale, 1.0)
    cos_t, sa, sb = cos_ref[...], sa_ref[...], sb_ref[...]
    h = h_ref[...]
    for c0 in range(0, w_ref.shape[1], ROPE_SUB_COLS):
        acc = jnp.dot(h, w_ref[:, c0:c0 + ROPE_SUB_COLS], preferred_element_type=F32)
        for c in range(ROPE_SUB_COLS // LANES):
            x = acc[:, c * LANES:(c + 1) * LANES]
            cols = slice(c0 + c * LANES, c0 + (c + 1) * LANES)
            o_ref[:, cols] = (_rope(x, cos_t, sa, sb, C_QK_DIM // 4) * scale).astype(o_ref.dtype)


def _mm_rope(h, w, tables, q_scale, tm=1024, tn=C_WIDTH):
    t, k = h.shape
    nt = DEC_SEQ // tm
    tab = pl.BlockSpec((tm, LANES), lambda i, j: (i % nt, 0))
    return pl.pallas_call(
        functools.partial(_mm_rope_kernel, q_tiles=C_WIDTH // tn, q_scale=q_scale),
        grid=(t // tm, 2 * C_WIDTH // tn),
        in_specs=[
            pl.BlockSpec((tm, k), lambda i, j: (i, 0)),
            pl.BlockSpec((k, tn), lambda i, j: (0, j)),
            tab, tab, tab,
        ],
        out_specs=pl.BlockSpec((tm, tn), lambda i, j: (i, j)),
        out_shape=jax.ShapeDtypeStruct((t, 2 * C_WIDTH), BF16),
        compiler_params=_params("parallel", "parallel"),
        name="in_proj_rope",
    )(h, w, *tables)


KEY_CHUNK = 256
SUBLANES = 8


MAX_KEY_BLOCK = 1024


def _key_blocks(s_new, has_cache):
    size = min(s_new, MAX_KEY_BLOCK)
    blocks = [(r, size) for r in range(0, s_new, size)]
    if has_cache:
        blocks.append((s_new, PAST_LEN))
    return blocks


def _attn_pipeline(units, blocks, s_scr, p_scr):
    n = s_scr.shape[2]
    st = [dict() for _ in units]

    def stage1(u, b):
        r0, size = blocks[b]
        if b == 0:
            st[u]["qq"] = units[u][0]()
        s = lax.dot_general(units[u][1](b), st[u]["qq"], (((1,), (1,)), ((), ())), preferred_element_type=F32)
        s_scr[u % 2, r0:r0 + size, :] = s
        m8 = s.reshape(size // SUBLANES, SUBLANES, n).max(axis=0)
        st[u]["m8"] = m8 if b == 0 else jnp.maximum(st[u]["m8"], m8)
        if b == len(blocks) - 1:
            st[u]["m"] = st[u]["m8"].max(axis=0, keepdims=True)

    def stage2(u, b):
        r0, size = blocks[b]
        for r in range(r0, r0 + size, KEY_CHUNK):
            p = jnp.exp2(s_scr[u % 2, r:r + KEY_CHUNK, :] - st[u]["m"])
            l8 = p.reshape(KEY_CHUNK // SUBLANES, SUBLANES, n).sum(axis=0)
            st[u]["l8"] = l8 if r == 0 else st[u]["l8"] + l8
            p_scr[u % 2, r:r + KEY_CHUNK, :] = p.astype(BF16)

    def stage3(u, b):
        r0, size = blocks[b]
        a = jnp.dot(units[u][2][:, r0:r0 + size], p_scr[u % 2, r0:r0 + size, :], preferred_element_type=F32)
        st[u]["acc"] = a if b == 0 else st[u]["acc"] + a
        if b == len(blocks) - 1:
            l = st[u]["l8"].sum(axis=0, keepdims=True)
            units[u][3](st[u]["acc"] * (1.0 / l))

    for slot in range(len(units) + 2):
        for b in range(len(blocks)):
            if 0 <= slot - 2 < len(units):
                stage3(slot - 2, b)
            if slot < len(units):
                stage1(slot, b)
            if 0 <= slot - 1 < len(units):
                stage2(slot - 1, b)


def _fill_vt(vt_scr, v_ref, vc_ref, n_kv):
    s_new = v_ref.shape[0]
    for j in range(n_kv):
        cols = slice(j * LANES, (j + 1) * LANES)
        vt_scr[j, :, 0:s_new] = v_ref[:, cols].astype(F32).T.astype(BF16)
        if vc_ref is not None:
            vt_scr[j, :, s_new:] = vc_ref[:, cols].astype(F32).T.astype(BF16)


def _attn_scratch(n_kv, s_tot, n):
    return [
        pltpu.VMEM((n_kv, LANES, s_tot), BF16),
        pltpu.VMEM((2, s_tot, n), F32),
        pltpu.VMEM((2, s_tot, n), BF16),
    ]


TQ = 256


def _gqa_kernel(*refs, has_cache, n_kv):
    q_ref, k_ref, v_ref = refs[:3]
    refs = refs[3:]
    kc_ref = vc_ref = None
    if has_cache:
        (kc_ref, vc_ref), refs = refs[:2], refs[2:]
    g_ref, o_ref, vt_scr, s_scr, p_scr = refs
    blocks = _key_blocks(k_ref.shape[0], has_cache)

    @pl.when(pl.program_id(2) == 0)
    def _():
        _fill_vt(vt_scr, v_ref, vc_ref, n_kv)

    def make_unit(rows, j, heads):
        kcols = slice(j * LANES, (j + 1) * LANES)

        def make_qq():
            return jnp.concatenate([q_ref[rows, h * LANES:(h + 1) * LANES] for h in heads], axis=0)

        def key_block(b):
            r0, size = blocks[b]
            if r0 < k_ref.shape[0]:
                return k_ref[r0:r0 + size, kcols].astype(BF16)
            return kc_ref[:, kcols].astype(BF16)

        def finish(ot):
            for i, h in enumerate(heads):
                cols = slice(h * LANES, (h + 1) * LANES)
                gate = g_ref[rows, cols].astype(F32)
                o_ref[rows, cols] = (ot[:, i * TQ:(i + 1) * TQ].T * _silu(gate)).astype(o_ref.dtype)

        return make_qq, key_block, vt_scr.at[j], finish

    units = [make_unit(slice(r, r + TQ), j, (j * B_GROUP + h, j * B_GROUP + h + 1))
             for r in range(0, q_ref.shape[0], TQ) for j in range(n_kv) for h in range(0, B_GROUP, 2)]
    _attn_pipeline(units, blocks, s_scr, p_scr)


def _gqa(qp, kp, vp, cache, gate, nb, t, tqs, n_kv):
    nq = t // tqs
    qw, kw = n_kv * B_GROUP * LANES, n_kv * LANES
    s_tot = t + (PAST_LEN if cache is not None else 0)
    in_specs = [
        pl.BlockSpec((tqs, qw), lambda b, h, i: (b * nq + i, h)),
        pl.BlockSpec((t, kw), lambda b, h, i: (b, h)),
        pl.BlockSpec((t, kw), lambda b, h, i: (b, h)),
    ]
    args = [qp, kp, vp]
    if cache is not None:
        in_specs += [pl.BlockSpec((PAST_LEN, kw), lambda b, h, i: (b, h))] * 2
        args += list(cache)
    in_specs.append(pl.BlockSpec((tqs, qw), lambda b, h, i: (b * nq + i, h)))
    args.append(gate)
    return pl.pallas_call(
        functools.partial(_gqa_kernel, has_cache=cache is not None, n_kv=n_kv),
        grid=(nb, B_KV_HEADS // n_kv, nq),
        in_specs=in_specs,
        out_specs=pl.BlockSpec((tqs, qw), lambda b, h, i: (b * nq + i, h)),
        out_shape=jax.ShapeDtypeStruct((nb * t, B_WIDTH), BF16),
        scratch_shapes=_attn_scratch(n_kv, s_tot, 2 * TQ),
        compiler_params=_params("parallel", "parallel", "arbitrary"),
        name="gqa_attn",
    )(*args)


def _diff_kernel(*refs, has_cache, hb, q_scale):
    q_ref, k_ref, v_ref = refs[:3]
    refs = refs[3:]
    kc_ref = vc_ref = None
    if has_cache:
        (kc_ref, vc_ref), refs = refs[:2], refs[2:]
    g_ref, lam_ref, sg_ref, o_ref, vt_scr, s_scr, p_scr = refs
    blocks = _key_blocks(k_ref.shape[0], has_cache)

    @pl.when(pl.program_id(2) == 0)
    def _():
        _fill_vt(vt_scr, v_ref, vc_ref, hb)

    lp = lam_ref[...]
    lam = (jnp.exp(jnp.sum(lp[0:1] * lp[1:2], axis=-1, keepdims=True))
           - jnp.exp(jnp.sum(lp[2:3] * lp[3:4], axis=-1, keepdims=True)) + LAMBDA_INIT_1)
    lane = lax.broadcasted_iota(jnp.int32, (TQ, LANES), 1)

    def make_unit(rows, j):
        cols = slice(j * LANES, (j + 1) * LANES)

        def make_qq():
            q = q_ref[rows, cols]
            if q_scale is not None:
                q = q.astype(F32) * q_scale
            zero = jnp.zeros_like(q)
            maps = [jnp.where(lane < C_QK_DIM, q, zero), jnp.where(lane >= C_QK_DIM, q, zero)]
            return jnp.concatenate(maps, axis=0).astype(BF16)

        def key_block(b):
            r0, size = blocks[b]
            if r0 < k_ref.shape[0]:
                return k_ref[r0:r0 + size, cols].astype(BF16)
            return kc_ref[:, cols].astype(BF16)

        def finish(ot):
            ot = ot[:, :TQ] - lam * ot[:, TQ:]
            ms = jnp.mean(ot * ot, axis=0, keepdims=True)
            o = (ot * lax.rsqrt(ms + EPS)).T * sg_ref[...] * (1.0 - LAMBDA_INIT_1)
            gate = g_ref[rows, cols].astype(F32)
            o_ref[rows, cols] = (o * _silu(gate)).astype(o_ref.dtype)

        return make_qq, key_block, vt_scr.at[j], finish

    units = [make_unit(slice(r, r + TQ), j) for j in range(hb) for r in range(0, q_ref.shape[0], TQ)]
    _attn_pipeline(units, blocks, s_scr, p_scr)


def _diff(q, k, v, g, cache, lam_p, sub_g, nb, t, tq, hb, q_scale):
    nq = t // tq
    w = hb * LANES
    s_tot = t + (PAST_LEN if cache is not None else 0)
    in_specs = [
        pl.BlockSpec((tq, w), lambda b, h, i: (b * nq + i, q[1] + h)),
        pl.BlockSpec((t, w), lambda b, h, i: (b, k[1] + h)),
        pl.BlockSpec((t, w), lambda b, h, i: (b, v[1] + h)),
    ]
    args = [q[0], k[0], v[0]]
    if cache is not None:
        in_specs += [pl.BlockSpec((PAST_LEN, w), lambda b, h, i: (b, h))] * 2
        args += list(cache)
    in_specs += [
        pl.BlockSpec((tq, w), lambda b, h, i: (b * nq + i, g[1] + h)),
        pl.BlockSpec((4, C_QK_DIM), lambda b, h, i: (0, 0)),
        pl.BlockSpec((1, LANES), lambda b, h, i: (0, 0)),
    ]
    args += [g[0], lam_p, sub_g.reshape(1, LANES)]
    return pl.pallas_call(
        functools.partial(_diff_kernel, has_cache=cache is not None, hb=hb, q_scale=q_scale),
        grid=(nb, C_HEADS // hb, nq),
        in_specs=in_specs,
        out_specs=pl.BlockSpec((tq, w), lambda b, h, i: (b * nq + i, h)),
        out_shape=jax.ShapeDtypeStruct((nb * t, C_WIDTH), BF16),
        scratch_shapes=_attn_scratch(hb, s_tot, 2 * TQ),
        compiler_params=_params("parallel", "parallel", "arbitrary"),
        name="diff_attn",
    )(*args)


def _out_kernel(*refs, n_in, final):
    y_refs = refs[:n_in]
    w_ref, x_ref, gate_ref = refs[n_in:n_in + 3]
    refs = refs[n_in + 3:]
    half = x_ref.shape[0] // 2
    for rows in (slice(0, half), slice(half, 2 * half)):
        acc = None
        k0 = 0
        for y_ref in y_refs:
            kk = y_ref.shape[1]
            a = jnp.dot(y_ref[rows, :], w_ref[k0:k0 + kk, :], preferred_element_type=F32)
            acc = a if acc is None else acc + a
            k0 += kk
        x = x_ref[rows, :] + gate_ref[...] * acc
        xn = x * lax.rsqrt(jnp.mean(x * x, axis=-1, keepdims=True) + EPS)
        if final:
            fg_ref, o_ref = refs
            o_ref[rows, :] = xn * fg_ref[...]
        else:
            g_ref, shift_ref, scale_ref, o_ref, h_ref = refs
            o_ref[rows, :] = x
            h_ref[rows, :] = (xn * g_ref[...] * (1.0 + scale_ref[...]) + shift_ref[...]).astype(h_ref.dtype)


def _out_proj(ys, w, x, mod3, row_of_tile, final_g=None, next_norm=None, tm=ROW_TILE):
    t = x.shape[0]
    final = final_g is not None
    vec = pl.BlockSpec((1, D_MODEL), lambda i: (0, 0))
    mod_row = lambda part: pl.BlockSpec((None, 1, D_MODEL), lambda i: (row_of_tile(i), 0, part))
    in_specs = [pl.BlockSpec((tm, y.shape[1]), lambda i: (i, 0)) for y in ys]
    in_specs += [
        pl.BlockSpec(w.shape, lambda i: (0, 0)),
        pl.BlockSpec((tm, D_MODEL), lambda i: (i, 0)),
        mod_row(2),
    ]
    args = list(ys) + [w, x, mod3]
    row_blk = pl.BlockSpec((tm, D_MODEL), lambda i: (i, 0))
    if final:
        in_specs.append(vec)
        args.append(final_g.reshape(1, D_MODEL))
        out_specs, out_shape = row_blk, jax.ShapeDtypeStruct((t, D_MODEL), F32)
    else:
        next_g, next_mod3 = next_norm
        in_specs += [vec, mod_row(0), mod_row(1)]
        args += [next_g.reshape(1, D_MODEL), next_mod3, next_mod3]
        out_specs = [row_blk, row_blk]
        out_shape = [jax.ShapeDtypeStruct((t, D_MODEL), F32), jax.ShapeDtypeStruct((t, D_MODEL), BF16)]
    return pl.pallas_call(
        functools.partial(_out_kernel, n_in=len(ys), final=final),
        grid=(t // tm,),
        in_specs=in_specs,
        out_specs=out_specs,
        out_shape=out_shape,
        compiler_params=_params("parallel"),
        name="out_proj",
    )(*args)


def kernel(x_prompt, x_sample, cache_k0, cache_v0, cache_k1, cache_v1, c, c_ctx, w_ada0, b_ada0, norm_g0, w_in0, w_s0, b_s0, q_norm_g0, k_norm_g0, w_out0, w_ada1, b_ada1, norm_g1, w_in1, lambda_q1, lambda_k1, lambda_q2, lambda_k2, subln_g1, w_out1, final_g):
    n_ctx, n_smp = x_prompt.shape[0], x_sample.shape[0]

    cond = jnp.concatenate([c_ctx[None], c, jnp.zeros((8 - 1 - n_smp, D_MODEL), F32)], axis=0)
    mod0 = _ada(cond, w_ada0, b_ada0).reshape(8, 1, 3 * D_MODEL)
    mod1 = _ada(cond, w_ada1, b_ada1).reshape(8, 1, 3 * D_MODEL)

    w_in0_b, w_out0_b = w_in0.astype(BF16), w_out0.astype(BF16)
    w_in1_b, w_out1_b = w_in1.astype(BF16), w_out1.astype(BF16)
    ws_b = w_s0.astype(BF16)
    bs_b = jnp.broadcast_to(b_s0[:, :, None], (A_GROUPS, CHUNK, LANES))
    lam_p = jnp.stack([lambda_q1, lambda_k1, lambda_q2, lambda_k2]).astype(F32)
    cos0, sin_a0, sin_b0 = _rope_tables(B_HEAD_DIM // 4)
    tab0 = (cos0, sin_a0 - sin_b0, _rope_partner_matrix(B_HEAD_DIM // 4))
    tab1 = _rope_tables(C_QK_DIM // 4)

    def run(x, nb, t, smp):
        def row_fn(tm):
            if not smp:
                return lambda i: 0
            return lambda i: 1 + (i * tm) // t

        front = (x, norm_g0, mod0, row_fn(ROW_TILE), w_in0_b, q_norm_g0, k_norm_g0)
        if smp:
            h, qp, kp, vp = _norm_mm_qkv0(*front, tab0, False)
            cache = (cache_k0.reshape(nb * PAST_LEN, B_KV_WIDTH), cache_v0.reshape(nb * PAST_LEN, B_KV_WIDTH))
            k0 = v0 = None
        else:
            h, qp, kp, vp, k0, v0 = _norm_mm_qkv0(*front, None, True)
            cache = None
        y_a = _mm_sgate(h, w_in0_b, ws_b, bs_b)
        gate_b = _mm(h, w_in0_b, BF16, 512, (G0_OFF // 512, 1, B_WIDTH // 512))
        y_b = _gqa(qp, kp, vp, cache, gate_b, nb, t, GQA_ROWS_SMP if smp else SEQ, 1 if smp else B_KV_HEADS)
        x1, h = _out_proj([y_a, y_b], w_out0_b, x, mod0, row_fn(ROW_TILE), next_norm=(norm_g1, mod1))

        if smp:
            qk = _mm_rope(h, w_in1_b, tab1, (C_QK_DIM ** -0.5) * LOG2E)
            vg = _mm(h, w_in1_b, BF16, 1024, (2 * C_WIDTH // 1024, 1, 2 * C_WIDTH // 1024))
            cache = (cache_k1.reshape(nb * PAST_LEN, C_WIDTH), cache_v1.reshape(nb * PAST_LEN, C_WIDTH))
            hb = 1
            second = C_WIDTH // (hb * LANES)
            y_c = _diff((qk, 0), (qk, second), (vg, 0), (vg, second), cache, lam_p, subln_g1,
                        nb, t, t, hb, None)
            k1 = v1 = None
        else:
            qg = _mm(h, w_in1_b, BF16, C_WIDTH, (0, 3, 2))
            kb, k1 = _mm_kt(h, w_in1_b, 1, nb)
            v1 = _mm(h, w_in1_b, F32, C_WIDTH, (2, 1, 1), tm=512)
            hb = DIFF_HEADS_CTX
            y_c = _diff((qg, 0), (kb, 0), (v1, 0), (qg, C_WIDTH // (hb * LANES)), None, lam_p, subln_g1,
                        nb, t, SEQ, hb, (C_QK_DIM ** -0.5) * LOG2E)
        y = _out_proj([y_c], w_out1_b, x1, mod1, row_fn(ROW_TILE), final_g)
        return y, k0, v0, k1, v1

    y_p, k0, v0, k1, v1 = run(x_prompt.reshape(n_ctx * SEQ, D_MODEL), n_ctx, SEQ, False)
    y_s, _, _, _, _ = run(x_sample.reshape(n_smp * DEC_SEQ, D_MODEL), n_smp, DEC_SEQ, True)

    return (
        y_p.reshape(n_ctx, SEQ, D_MODEL),
        y_s.reshape(n_smp, DEC_SEQ, D_MODEL),
        k0.reshape(n_ctx, SEQ, B_KV_HEADS, B_HEAD_DIM),
        v0.reshape(n_ctx, SEQ, B_KV_HEADS, B_HEAD_DIM),
        k1.reshape(n_ctx, C_HEADS, 2, C_QK_DIM, SEQ).transpose(0, 4, 1, 2, 3),
        v1.reshape(n_ctx, SEQ, C_HEADS, C_V_DIM),
    )
```
